```python
import math
import jax, jax.numpy as jnp
from jax import lax
import numpy as np

D_MODEL = 2048
BATCH = 32
SEQ = 256
DEPTH = 4
DEC_BATCH = 2
DEC_SEQ = 1024
PAST_LEN = 512

GRID_W = 64
N_MIXERS = 2
N_GDN = (DEPTH + 1) // 2
N_POOL = DEPTH // 2
GDN_HEADS = 16
GDN_DK = 128
GDN_DV = 128
GDN_QK = GDN_HEADS * GDN_DK
GDN_VW = GDN_HEADS * GDN_DV
GDN_PROJ = 2 * GDN_QK + 2 * GDN_VW + 4 * GDN_HEADS
CONV_K = 5
CONV_PAD = CONV_K // 2
CHUNK = 64
POOL_WINDOWS = (2, 4, 8, 16)
POOL_GROUP = D_MODEL // len(POOL_WINDOWS)
D_FF = 128 * ((8 * D_MODEL // 3 + 127) // 128)
N_MOD = 9
RMS_EPS = 1e-6
L2_EPS = 1e-6
POS_BASE = 10000.0

kernel_name = 'hybrid_gdn_pool_macaron_diffusion_step'


def rmsnorm(x, g):
    xf = x.astype(jnp.float32)
    y = xf * lax.rsqrt(jnp.mean(xf * xf, axis=-1, keepdims=True) + RMS_EPS)
    return (y * g.astype(jnp.float32)).astype(x.dtype)


def mod_norm(x, mod, s, g):
    return rmsnorm(x, g) * (1 + mod[:, :, 3 * s + 1]) + mod[:, :, 3 * s]


def grid_pos_embed(n_tok, dtype):
    rows = n_tok // GRID_W
    idx = jnp.arange(rows * GRID_W)
    r = (idx // GRID_W).astype(jnp.float32)
    col = (idx % GRID_W).astype(jnp.float32)
    n_freq = D_MODEL // 4
    freqs = jnp.exp(-math.log(POS_BASE) * jnp.arange(n_freq, dtype=jnp.float32) / n_freq)
    ar = r[:, None] * freqs
    ac = col[:, None] * freqs
    return jnp.concatenate([jnp.sin(ar), jnp.cos(ar), jnp.sin(ac), jnp.cos(ac)], axis=-1).astype(dtype)


def swiglu(h, w_in, w_out):
    gu = jnp.einsum('btd,df->btf', h, w_in)
    gate, up = jnp.split(gu, 2, axis=-1)
    return jnp.einsum('btf,fd->btd', jax.nn.silu(gate) * up, w_out)


def centred_short_conv(u, w):
    t = u.shape[1]
    up = jnp.pad(u, ((0, 0), (CONV_PAD, CONV_PAD), (0, 0)))
    acc = up[:, 0:t] * w[0]
    for j in range(1, CONV_K):
        acc = acc + up[:, j:j + t] * w[j]
    return jax.nn.silu(acc)


def l2norm(x):
    return x * lax.rsqrt(jnp.sum(x * x, axis=-1, keepdims=True) + L2_EPS)


def gated_delta_chunked(q, k, v, g, beta, s0):
    b, t, h, _ = q.shape
    dv = v.shape[-1]
    n = t // CHUNK

    def blk(a):
        a = a.reshape((b, n, CHUNK, h) + a.shape[3:])
        return jnp.moveaxis(a, 3, 1)

    q, k, v, g, beta = blk(q), blk(k), blk(v), blk(g), blk(beta)
    gam = jnp.cumsum(g, axis=-1)
    diff = gam[..., :, None] - gam[..., None, :]
    idx = jnp.arange(CHUNK)
    incl = idx[:, None] >= idx[None, :]
    strict = idx[:, None] > idx[None, :]
    dec_incl = jnp.exp(jnp.where(incl, diff, -jnp.inf))
    dec_strict = jnp.where(strict, dec_incl, 0.0)
    a_kk = beta[..., None] * jnp.einsum('bhncd,bhnsd->bhncs', k, k) * dec_strict
    rhs_w = k * (beta * jnp.exp(gam))[..., None]
    rhs_u = v * beta[..., None]
    w_mat = lax.linalg.triangular_solve(a_kk, rhs_w, left_side=True, lower=True, unit_diagonal=True)
    u_mat = lax.linalg.triangular_solve(a_kk, rhs_u, left_side=True, lower=True, unit_diagonal=True)
    a_qk = jnp.einsum('bhncd,bhnsd->bhncs', q, k) * dec_incl
    q_g = q * jnp.exp(gam)[..., None]
    k_d = k * jnp.exp(gam[..., -1:] - gam)[..., None]
    c_dec = jnp.exp(gam[..., -1])
    xs = tuple(jnp.moveaxis(a, 2, 0) for a in (w_mat, u_mat, a_qk, q_g, k_d, c_dec))

    def step(s, inp):
        w_c, u_c, aqk_c, qg_c, kd_c, dec_c = inp
        v_new = u_c - jnp.einsum('bhcd,bhde->bhce', w_c, s)
        o_c = jnp.einsum('bhcd,bhde->bhce', qg_c, s) + jnp.einsum('bhcs,bhse->bhce', aqk_c, v_new)
        s = s * dec_c[..., None, None] + jnp.einsum('bhcd,bhce->bhde', kd_c, v_new)
        return s, o_c

    s_fin, o = lax.scan(step, s0, xs)
    o = jnp.transpose(o, (1, 0, 3, 2, 4)).reshape(b, t, h, dv)
    return o, s_fin


def gdn_mixer(h, s0, w_in, conv_w, a_log, dt_bias, norm_g, w_out):
    b, t, _ = h.shape
    f32 = jnp.float32
    proj = jnp.einsum('btd,dp->btp', h, w_in)
    qkv = centred_short_conv(proj[..., :2 * GDN_QK + GDN_VW], conv_w).astype(f32)
    q = l2norm(qkv[..., :GDN_QK].reshape(b, t, GDN_HEADS, GDN_DK)) * (GDN_DK ** -0.5)
    k = l2norm(qkv[..., GDN_QK:2 * GDN_QK].reshape(b, t, GDN_HEADS, GDN_DK))
    v = qkv[..., 2 * GDN_QK:].reshape(b, t, GDN_HEADS, GDN_DV)
    z = proj[..., 2 * GDN_QK + GDN_VW:2 * GDN_QK + 2 * GDN_VW].astype(f32).reshape(b, t, GDN_HEADS, GDN_DV)
    ab = proj[..., 2 * GDN_QK + 2 * GDN_VW:].astype(f32).reshape(b, t, 2, 2, GDN_HEADS)
    g = -jnp.exp(a_log.astype(f32)) * jax.nn.softplus(ab[..., 0, :] + dt_bias.astype(f32))
    beta = jax.nn.sigmoid(ab[..., 1, :])
    s0 = s0.astype(f32)
    o_f, s_f = gated_delta_chunked(q, k, v, g[:, :, 0], beta[:, :, 0], s0[:, 0])
    fl = lambda a: jnp.flip(a, axis=1)
    o_b, s_b = gated_delta_chunked(fl(q), fl(k), fl(v), fl(g[:, :, 1]), fl(beta[:, :, 1]), s0[:, 1])
    o = o_f + fl(o_b)
    o = o * lax.rsqrt(jnp.mean(o * o, axis=-1, keepdims=True) + RMS_EPS) * norm_g.astype(f32) * jax.nn.silu(z)
    out = jnp.einsum('btv,vd->btd', o.reshape(b, t, GDN_VW).astype(h.dtype), w_out)
    return out, jnp.stack([s_f, s_b], axis=1)


def pool_mixer(h, w, scale):
    b, t, _ = h.shape
    hf = h.astype(jnp.float32)
    cs = jnp.concatenate([jnp.zeros((b, 1, D_MODEL), jnp.float32), jnp.cumsum(hf, axis=1)], axis=1)
    pos = jnp.arange(t)
    outs = []
    for gi, win in enumerate(POOL_WINDOWS):
        lo = jnp.maximum(pos - win // 2, 0)
        hi = jnp.minimum(pos + win - win // 2, t)
        sl = slice(gi * POOL_GROUP, (gi + 1) * POOL_GROUP)
        csg = cs[..., sl]
        mean = (jnp.take(csg, hi, axis=1) - jnp.take(csg, lo, axis=1)) / (hi - lo).astype(jnp.float32)[None, :, None]
        outs.append(jnp.einsum('btc,cd->btd', (mean - hf[..., sl]).astype(h.dtype), w[gi]))
    return jnp.concatenate(outs, axis=-1) * scale


def setup_inputs(seed: int = 0) -> dict:
    key = jax.random.key(seed)
    ks = jax.random.split(key, 24)
    nrm = lambda k, shape, s: jax.random.normal(k, shape, jnp.float32) * s
    dt = jnp.exp(jax.random.uniform(ks[13], (N_GDN, 2, GDN_HEADS), jnp.float32, math.log(0.001), math.log(0.1)))
    return {
        'x_prompt': nrm(ks[0], (BATCH, SEQ, D_MODEL), 1.0),
        'x_sample': nrm(ks[1], (DEC_BATCH, DEC_SEQ, D_MODEL), 1.0),
        'state_gdn': nrm(ks[2], (DEC_BATCH, N_GDN, 2, GDN_HEADS, GDN_DK, GDN_DV), 0.5),
        'c': nrm(ks[3], (DEC_BATCH, D_MODEL), 1.0),
        'c_ctx': nrm(ks[4], (D_MODEL,), 1.0),
        'w_mod': nrm(ks[5], (DEPTH, D_MODEL, N_MOD * D_MODEL), 0.5 * D_MODEL ** -0.5),
        'b_mod': nrm(ks[6], (DEPTH, N_MOD * D_MODEL), 0.01),
        'norm_g': 1.0 + nrm(ks[7], (DEPTH, 3, D_MODEL), 0.02),
        'ffn_w_in': nrm(ks[8], (DEPTH, 2, D_MODEL, 2 * D_FF), D_MODEL ** -0.5),
        'ffn_w_out': nrm(ks[9], (DEPTH, 2, D_FF, D_MODEL), D_FF ** -0.5),
        'gdn_w_in': nrm(ks[10], (N_GDN, D_MODEL, GDN_PROJ), D_MODEL ** -0.5),
        'gdn_conv': nrm(ks[11], (N_GDN, CONV_K, 2 * GDN_QK + GDN_VW), CONV_K ** -0.5),
        'gdn_a_log': jnp.log(jax.random.uniform(ks[12], (N_GDN, 2, GDN_HEADS), jnp.float32, 1.0, 16.0)),
        'gdn_dt_bias': dt + jnp.log(-jnp.expm1(-dt)),
        'gdn_norm_g': 1.0 + nrm(ks[14], (N_GDN, GDN_DV), 0.02),
        'gdn_w_out': nrm(ks[15], (N_GDN, GDN_VW, D_MODEL), GDN_VW ** -0.5),
        'pool_w': nrm(ks[16], (N_POOL, len(POOL_WINDOWS), POOL_GROUP, POOL_GROUP), POOL_GROUP ** -0.5),
        'pool_scale': 1.0 + nrm(ks[17], (N_POOL, D_MODEL), 0.02),
        'final_g': 1.0 + nrm(ks[18], (D_MODEL,), 0.02),
    }


def reference(x_prompt, x_sample, state_gdn, c, c_ctx, w_mod, b_mod, norm_g, ffn_w_in, ffn_w_out,
              gdn_w_in, gdn_conv, gdn_a_log, gdn_dt_bias, gdn_norm_g, gdn_w_out, pool_w, pool_scale, final_g):
    xp = x_prompt
    xs = x_sample + grid_pos_embed(x_sample.shape[1], x_sample.dtype)[None]
    sc = jax.nn.silu(c)
    sctx = jax.nn.silu(c_ctx)[None]
    new_states = []
    for l in range(DEPTH):
        mp = (sctx @ w_mod[l] + b_mod[l]).reshape(1, 1, N_MOD, D_MODEL)
        ms = (sc @ w_mod[l] + b_mod[l]).reshape(-1, 1, N_MOD, D_MODEL)
        xp = xp + 0.5 * mp[:, :, 2] * swiglu(mod_norm(xp, mp, 0, norm_g[l, 0]), ffn_w_in[l, 0], ffn_w_out[l, 0])
        xs = xs + 0.5 * ms[:, :, 2] * swiglu(mod_norm(xs, ms, 0, norm_g[l, 0]), ffn_w_in[l, 0], ffn_w_out[l, 0])
        hp = mod_norm(xp, mp, 1, norm_g[l, 1])
        hs = mod_norm(xs, ms, 1, norm_g[l, 1])
        mi = l // N_MIXERS
        if l % N_MIXERS == 0:
            gw = (gdn_w_in[mi], gdn_conv[mi], gdn_a_log[mi], gdn_dt_bias[mi], gdn_norm_g[mi], gdn_w_out[mi])
            s_zero = jnp.zeros((xp.shape[0], 2, GDN_HEADS, GDN_DK, GDN_DV), jnp.float32)
            op, sp = gdn_mixer(hp, s_zero, *gw)
            os_, _ = gdn_mixer(hs, state_gdn[:, mi], *gw)
            new_states.append(sp)
        else:
            op = pool_mixer(hp, pool_w[mi], pool_scale[mi])
            os_ = pool_mixer(hs, pool_w[mi], pool_scale[mi])
        xp = xp + mp[:, :, 5] * op
        xs = xs + ms[:, :, 5] * os_
        xp = xp + 0.5 * mp[:, :, 8] * swiglu(mod_norm(xp, mp, 2, norm_g[l, 2]), ffn_w_in[l, 1], ffn_w_out[l, 1])
        xs = xs + 0.5 * ms[:, :, 8] * swiglu(mod_norm(xs, ms, 2, norm_g[l, 2]), ffn_w_in[l, 1], ffn_w_out[l, 1])
    y_prompt = rmsnorm(xp, final_g)
    y_sample = rmsnorm(xs, final_g)
    new_state_gdn = jnp.stack(new_states, axis=1).astype(state_gdn.dtype)
    return (y_prompt, y_sample, new_state_gdn)
```

```python
import functools
import math

import jax
import jax.numpy as jnp
from jax import lax
from jax.experimental import pallas as pl
from jax.experimental.pallas import tpu as pltpu

F32 = jnp.float32
BF16 = jnp.bfloat16

RMS_EPS = 1e-6
L2_EPS = 1e-6
CHUNK = 64
CONV_K = 5
CONV_PAD = CONV_K // 2
POOL_WINDOWS = (2, 4, 8, 16)
GRID_W = 64
POS_BASE = 10000.0
N_MOD = 9
MOD_ROWS = 8
LANES = 128
FF_TILE = 512
VMEM_LIMIT = 56 * 1024 * 1024


def _cparams(*sem):
    return pltpu.CompilerParams(dimension_semantics=sem, vmem_limit_bytes=VMEM_LIMIT)


def _sigmoid(x):
    return jax.nn.sigmoid(x)


def _silu(x):
    return x * _sigmoid(x)


def _dot(a, b):
    return jnp.dot(a.astype(BF16), b.astype(BF16), preferred_element_type=F32)


def _dot_nt(a, b):
    return lax.dot_general(a.astype(BF16), b.astype(BF16), (((1,), (1,)), ((), ())), preferred_element_type=F32)


def _split(a):
    hi = a.astype(BF16)
    lo = (a - hi.astype(F32)).astype(BF16)
    return hi, lo


def _dot_hi(a, b):
    ah, al = _split(a)
    bh, bl = _split(b)
    d = functools.partial(jnp.dot, preferred_element_type=F32)
    return d(ah, bh) + (d(ah, bl) + d(al, bh))


def _mod_norm(x, g, scale, shift):
    ms = jnp.mean(x * x, axis=-1, keepdims=True)
    return (x * lax.rsqrt(ms + RMS_EPS) * g) * (1.0 + scale) + shift


def _mod_kernel(c_ref, w_ref, b_ref, o_ref):
    s = _silu(c_ref[...])
    o_ref[...] = _dot_hi(s, w_ref[...]) + b_ref[...]


def _modulation(c8, w_mod, b_mod):
    depth, d, nd = w_mod.shape
    tn = next(c_ for c_ in (1024, 512, 256, 128) if nd % c_ == 0)
    return pl.pallas_call(
        _mod_kernel,
        grid=(depth, nd // tn),
        in_specs=[
            pl.BlockSpec((MOD_ROWS, d), lambda l, j: (0, 0)),
            pl.BlockSpec((None, d, tn), lambda l, j: (l, 0, j)),
            pl.BlockSpec((None, 1, tn), lambda l, j: (l, 0, j)),
        ],
        out_specs=pl.BlockSpec((None, MOD_ROWS, tn), lambda l, j: (l, 0, j)),
        out_shape=jax.ShapeDtypeStruct((depth, MOD_ROWS, nd), F32),
        compiler_params=_cparams("parallel", "parallel"),
        name="modulation",
    )(c8, w_mod, b_mod.reshape(depth, 1, nd))


def _add_kernel(x_ref, p_ref, o_ref):
    o_ref[...] = x_ref[...] + p_ref[...]


def _add_pos(xs, pos):
    b, t, d = xs.shape
    return pl.pallas_call(
        _add_kernel,
        grid=(b,),
        in_specs=[pl.BlockSpec((None, t, d), lambda i: (i, 0, 0)), pl.BlockSpec((t, d), lambda i: (0, 0))],
        out_specs=pl.BlockSpec((None, t, d), lambda i: (i, 0, 0)),
        out_shape=jax.ShapeDtypeStruct(xs.shape, xs.dtype),
        compiler_params=_cparams("parallel"),
        name="add_pos",
    )(xs, pos)


def _grid_pos_embed(n_tok, d, dtype):
    idx = jnp.arange(n_tok)
    r = (idx // GRID_W).astype(F32)
    col = (idx % GRID_W).astype(F32)
    n_freq = d // 4
    freqs = jnp.exp(-math.log(POS_BASE) * jnp.arange(n_freq, dtype=F32) / n_freq)
    ar = r[:, None] * freqs
    ac = col[:, None] * freqs
    return jnp.concatenate([jnp.sin(ar), jnp.cos(ar), jnp.sin(ac), jnp.cos(ac)], axis=-1).astype(dtype)


def _ffn_kernel(x_ref, mod_ref, g_ref, wg_ref, wu_ref, wo_ref, o_ref, h_ref, *, sub):
    j = pl.program_id(1)

    @pl.when(j == 0)
    def _():
        h = _mod_norm(x_ref[...], g_ref[...], mod_ref[3 * sub + 1:3 * sub + 2, :], mod_ref[3 * sub:3 * sub + 1, :])
        h_ref[...] = h.astype(BF16)
        o_ref[...] = jnp.zeros_like(o_ref)

    h = h_ref[...]
    gate = jnp.dot(h, wg_ref[...], preferred_element_type=F32)
    up = jnp.dot(h, wu_ref[...], preferred_element_type=F32)
    act = (_silu(gate) * up).astype(BF16)
    o_ref[...] += jnp.dot(act, wo_ref[...], preferred_element_type=F32)

    @pl.when(j == pl.num_programs(1) - 1)
    def _():
        o_ref[...] = x_ref[...] + (0.5 * mod_ref[3 * sub + 2:3 * sub + 3, :]) * o_ref[...]


def _ffn(x, mods_l, g, w_in_p, w_out_p, row_of_tile, tm, sub):
    m, d = x.shape
    fp = w_out_p.shape[0]
    nj = fp // FF_TILE
    return pl.pallas_call(
        functools.partial(_ffn_kernel, sub=sub),
        grid=(m // tm, nj),
        in_specs=[
            pl.BlockSpec((tm, d), lambda i, j: (i, 0)),
            pl.BlockSpec((None, N_MOD, d), lambda i, j: (row_of_tile(i), 0, 0)),
            pl.BlockSpec((1, d), lambda i, j: (0, 0)),
            pl.BlockSpec((d, FF_TILE), lambda i, j: (0, j)),
            pl.BlockSpec((d, FF_TILE), lambda i, j: (0, nj + j)),
            pl.BlockSpec((FF_TILE, d), lambda i, j: (j, 0)),
        ],
        out_specs=pl.BlockSpec((tm, d), lambda i, j: (i, 0)),
        out_shape=jax.ShapeDtypeStruct((m, d), F32),
        scratch_shapes=[pltpu.VMEM((tm, d), BF16)],
        compiler_params=_cparams("parallel", "arbitrary"),
        name="ffn",
    )(x, mods_l, g, w_in_p, w_in_p, w_out_p)


def _proj_kernel(x_ref, mod_ref, g_ref, w_ref, wab_ref, o_ref, ab_ref, h_ref):
    @pl.when(pl.program_id(1) == 0)
    def _():
        h = _mod_norm(x_ref[...], g_ref[...], mod_ref[4:5, :], mod_ref[3:4, :]).astype(BF16)
        h_ref[...] = h
        ab_ref[...] = jnp.dot(h, wab_ref[...], preferred_element_type=F32)

    o_ref[...] = jnp.dot(h_ref[...], w_ref[...], preferred_element_type=F32)


def _gdn_proj(x, mods_l, g, w_main, w_ab, row_of_tile, tm):
    m, d = x.shape
    n = w_main.shape[1]
    tn = 1024
    return pl.pallas_call(
        _proj_kernel,
        grid=(m // tm, n // tn),
        in_specs=[
            pl.BlockSpec((tm, d), lambda i, j: (i, 0)),
            pl.BlockSpec((None, N_MOD, d), lambda i, j: (row_of_tile(i), 0, 0)),
            pl.BlockSpec((1, d), lambda i, j: (0, 0)),
            pl.BlockSpec((d, tn), lambda i, j: (0, j)),
            pl.BlockSpec((d, LANES), lambda i, j: (0, 0)),
        ],
        out_specs=[pl.BlockSpec((tm, tn), lambda i, j: (i, j)), pl.BlockSpec((tm, LANES), lambda i, j: (i, 0))],
        out_shape=[jax.ShapeDtypeStruct((m, n), F32), jax.ShapeDtypeStruct((m, LANES), F32)],
        scratch_shapes=[pltpu.VMEM((tm, d), BF16)],
        compiler_params=_cparams("parallel", "arbitrary"),
        name="gdn_proj",
    )(x, mods_l, g, w_main, w_ab)


def _conv_silu(x_ref, w_ref, t):
    x = x_ref[...]
    row = lax.broadcasted_iota(jnp.int32, x.shape, 0)
    acc = x * w_ref[CONV_PAD:CONV_PAD + 1, :]
    for j in range(CONV_K):
        o = j - CONV_PAD
        if o == 0:
            continue
        xs = pltpu.roll(x, (-o) % t, 0)
        valid = (row + o >= 0) if o < 0 else (row + o < t)
        acc = acc + jnp.where(valid, xs, 0.0) * w_ref[j:j + 1, :]
    return _silu(acc)


def _chunk_cumsum(g, t, reverse):
    pos = lax.broadcasted_iota(jnp.int32, g.shape, 0) % CHUNK
    s = 1
    while s < CHUNK:
        if reverse:
            g = g + jnp.where(pos < CHUNK - s, pltpu.roll(g, t - s, 0), 0.0)
        else:
            g = g + jnp.where(pos >= s, pltpu.roll(g, s, 0), 0.0)
        s *= 2
    return g


def _gdn_kernel(*refs, t, hb, n_heads, has_s0, emit_state):
    (q_ref, k_ref, v_ref, z_ref, ab_ref, cq_ref, ck_ref, cv_ref, alog_ref, dtb_ref, ng_ref), refs = refs[:11], refs[11:]
    if has_s0:
        s0_ref, refs = refs[0], refs[1:]
    og_ref, refs = refs[0], refs[1:]
    if emit_state:
        sout_ref, refs = refs[0], refs[1:]
    qn_ref, kn_ref, vc_ref, gam_ref, beta_ref, s_ref, o_ref = refs

    dk = LANES
    n_chunks = t // CHUNK
    hg = pl.program_id(1)

    qc = _conv_silu(q_ref, cq_ref, t)
    kc = _conv_silu(k_ref, ck_ref, t)
    vc_ref[...] = _conv_silu(v_ref, cv_ref, t)
    for hh in range(hb):
        cs = slice(hh * dk, (hh + 1) * dk)
        qh = qc[:, cs]
        kh = kc[:, cs]
        qn_ref[:, cs] = qh * lax.rsqrt(jnp.sum(qh * qh, axis=-1, keepdims=True) + L2_EPS) * (dk ** -0.5)
        kn_ref[:, cs] = kh * lax.rsqrt(jnp.sum(kh * kh, axis=-1, keepdims=True) + L2_EPS)

    ab = ab_ref[...]
    pre = ab + dtb_ref[...]
    softplus = jnp.maximum(pre, 0.0) + jnp.log1p(jnp.exp(-jnp.abs(pre)))
    g_all = -jnp.exp(alog_ref[...]) * softplus
    beta_all = _sigmoid(ab)
    cum = (_chunk_cumsum(g_all, t, False), _chunk_cumsum(g_all, t, True))
    lane = lax.broadcasted_iota(jnp.int32, ab.shape, 1)
    for d in range(2):
        for hh in range(hb):
            col = d * 2 * n_heads + hg * hb + hh
            gsel = jnp.sum(jnp.where(lane == col, cum[d], 0.0), axis=-1, keepdims=True)
            bsel = jnp.sum(jnp.where(lane == col + n_heads, beta_all, 0.0), axis=-1, keepdims=True)
            gam_ref[d * hb + hh] = jnp.broadcast_to(gsel, ab.shape)
            beta_ref[d * hb + hh] = jnp.broadcast_to(bsel, ab.shape)
            s_ref[d * hb + hh] = s0_ref[d, hh] if has_s0 else jnp.zeros((dk, dk), F32)
    o_ref[...] = jnp.zeros_like(o_ref)

    ii = lax.broadcasted_iota(jnp.int32, (CHUNK, CHUNK), 0)
    jj = lax.broadcasted_iota(jnp.int32, (CHUNK, CHUNK), 1)
    eye = (ii == jj).astype(F32)
    pair_masks = []
    s_blk = 1
    while s_blk < CHUNK:
        pair_masks.append(((ii // (2 * s_blk)) == (jj // (2 * s_blk))) & ((ii // s_blk) != (jj // s_blk)))
        s_blk *= 2

    def step(s, carry):
        for d in range(2):
            n = s if d == 0 else n_chunks - 1 - s
            rows = pl.ds(pl.multiple_of(n * CHUNK, CHUNK), CHUNK)
            strict = (ii > jj) if d == 0 else (ii < jj)
            incl = (ii >= jj) if d == 0 else (ii <= jj)
            for hh in range(hb):
                idx = d * hb + hh
                cs = slice(hh * dk, (hh + 1) * dk)
                q = qn_ref[rows, cs]
                k = kn_ref[rows, cs]
                v = vc_ref[rows, cs]
                gc = gam_ref[idx, rows, :]
                bc = beta_ref[idx, rows, :]
                gr = jnp.concatenate([gc, gc], axis=0).T[:CHUNK, :CHUNK]
                diff = gc[:, :CHUNK] - gr
                dec_incl = jnp.where(incl, jnp.exp(jnp.where(incl, diff, 0.0)), 0.0)
                dec_strict = jnp.where(strict, dec_incl, 0.0)
                kk = _dot_nt(k, k)
                qk = _dot_nt(q, k)
                a = bc[:, :CHUNK] * kk * dec_strict
                x = eye - jnp.where(pair_masks[0], a, 0.0)
                for pm in pair_masks[1:]:
                    x = x - _dot_hi(_dot_hi(x, jnp.where(pm, a, 0.0)), x)
                eg = jnp.exp(gc)
                w_mat = _dot_hi(x, k * (bc * eg))
                u_mat = _dot_hi(x, v * bc)
                a_qk = qk * dec_incl
                q_g = q * eg
                g_last = gc[CHUNK - 1:CHUNK, :] if d == 0 else gc[0:1, :]
                k_d = k * jnp.exp(g_last - gc)
                c_dec = jnp.exp(g_last)
                st = s_ref[idx]
                v_new = u_mat - _dot(w_mat, st)
                o_c = _dot(q_g, st) + _dot(a_qk, v_new)
                s_ref[idx] = st * c_dec + _dot(k_d.T, v_new)
                o_ref[rows, cs] += o_c
        return carry

    lax.fori_loop(0, n_chunks, step, 0)

    for hh in range(hb):
        cs = slice(hh * dk, (hh + 1) * dk)
        o = o_ref[:, cs]
        o = o * lax.rsqrt(jnp.mean(o * o, axis=-1, keepdims=True) + RMS_EPS) * ng_ref[...] * _silu(z_ref[:, cs])
        og_ref[:, cs] = o.astype(og_ref.dtype)
    if emit_state:
        for d in range(2):
            for hh in range(hb):
                sout_ref[d, hh] = s_ref[d * hb + hh]


def _gdn_core(proj, ab, conv_w, alog_row, dtb_row, ng_row, s0, *, row0, n_seq, t, hb, n_heads, emit_state):
    dk = LANES
    wb = hb * dk
    n_hg = n_heads // hb
    rb = row0 // t
    has_s0 = s0 is not None

    def col_spec(part):
        return pl.BlockSpec((t, wb), lambda b, h: (rb + b, part * n_hg + h))

    def conv_spec(part):
        return pl.BlockSpec((CONV_K, wb), lambda b, h: (0, part * n_hg + h))

    row_spec = pl.BlockSpec((1, LANES), lambda b, h: (0, 0))
    in_specs = [col_spec(0), col_spec(1), col_spec(2), col_spec(3),
                pl.BlockSpec((t, LANES), lambda b, h: (rb + b, 0)),
                conv_spec(0), conv_spec(1), conv_spec(2), row_spec, row_spec, row_spec]
    args = [proj, proj, proj, proj, ab, conv_w, conv_w, conv_w, alog_row, dtb_row, ng_row]
    state_spec = pl.BlockSpec((None, 2, hb, dk, dk), lambda b, h: (b, 0, h, 0, 0))
    if has_s0:
        in_specs.append(state_spec)
        args.append(s0)
    out_specs = [pl.BlockSpec((t, wb), lambda b, h: (b, h))]
    out_shape = [jax.ShapeDtypeStruct((n_seq * t, n_heads * dk), BF16)]
    if emit_state:
        out_specs.append(state_spec)
        out_shape.append(jax.ShapeDtypeStruct((n_seq, 2, n_heads, dk, dk), F32))
    return pl.pallas_call(
        functools.partial(_gdn_kernel, t=t, hb=hb, n_heads=n_heads, has_s0=has_s0, emit_state=emit_state),
        grid=(n_seq, n_hg),
        in_specs=in_specs,
        out_specs=out_specs,
        out_shape=out_shape,
        scratch_shapes=[
            pltpu.VMEM((t, wb), F32), pltpu.VMEM((t, wb), F32), pltpu.VMEM((t, wb), F32),
            pltpu.VMEM((2 * hb, t, LANES), F32), pltpu.VMEM((2 * hb, t, LANES), F32),
            pltpu.VMEM((2 * hb, dk, dk), F32), pltpu.VMEM((t, wb), F32),
        ],
        compiler_params=_cparams("parallel", "parallel"),
        name="gdn_core",
    )(*args)


def _out_kernel(x_ref, mod_ref, a_ref, w_ref, o_ref):
    y = jnp.dot(a_ref[...], w_ref[...], preferred_element_type=F32)
    o_ref[...] = x_ref[...] + mod_ref[5:6, :] * y


def _gdn_out(x, mods_l, og, w_out, row_of_tile, tm):
    m, d = x.shape
    kdim = og.shape[1]
    return pl.pallas_call(
        _out_kernel,
        grid=(m // tm,),
        in_specs=[
            pl.BlockSpec((tm, d), lambda i: (i, 0)),
            pl.BlockSpec((None, N_MOD, d), lambda i: (row_of_tile(i), 0, 0)),
            pl.BlockSpec((tm, kdim), lambda i: (i, 0)),
            pl.BlockSpec((kdim, d), lambda i: (0, 0)),
        ],
        out_specs=pl.BlockSpec((tm, d), lambda i: (i, 0)),
        out_shape=jax.ShapeDtypeStruct((m, d), F32),
        compiler_params=_cparams("parallel"),
        name="gdn_out",
    )(x, mods_l, og, w_out)


def _pool_kernel(xf_ref, xg_ref, mod_ref, g_ref, w_ref, sc_ref, o_ref, rstd_ref, diff_ref, *, t):
    gi = pl.program_id(1)

    @pl.when(gi == 0)
    def _():
        xf = xf_ref[...]
        rstd_ref[...] = jnp.broadcast_to(lax.rsqrt(jnp.mean(xf * xf, axis=-1, keepdims=True) + RMS_EPS), rstd_ref.shape)

    xg = xg_ref[...]
    pg = xg.shape[1]
    rstd = rstd_ref[...]
    if pg > LANES:
        rstd = jnp.concatenate([rstd] * (pg // LANES), axis=1)
    h = (xg * rstd[:, :pg] * g_ref[...]) * (1.0 + mod_ref[4:5, :]) + mod_ref[3:4, :]
    row = lax.broadcasted_iota(jnp.int32, h.shape, 0)
    for k, win in enumerate(POOL_WINDOWS):
        @pl.when(gi == k)
        def _(win=win):
            half = win // 2
            acc = h
            for o in range(-half, win - half):
                if o == 0:
                    continue
                hs = pltpu.roll(h, (-o) % t, 0)
                valid = (row + o >= 0) if o < 0 else (row + o < t)
                acc = acc + jnp.where(valid, hs, 0.0)
            cnt = (jnp.minimum(row + (win - half), t) - jnp.maximum(row - half, 0)).astype(F32)
            diff_ref[...] = (acc / cnt - h).astype(BF16)

    y = jnp.dot(diff_ref[...], w_ref[...], preferred_element_type=F32) * sc_ref[...]
    o_ref[...] = xg + mod_ref[5:6, :] * y


def _pool(x, mods_l, g, w, scale, *, row0, n_seq, t, mod_row0, mod_per_seq):
    m, d = x.shape
    ng, pg, _ = w.shape
    rb = row0 // t
    mod_idx = (lambda b: mod_row0 + b) if mod_per_seq else (lambda b: mod_row0)
    return pl.pallas_call(
        functools.partial(_pool_kernel, t=t),
        grid=(n_seq, ng),
        in_specs=[
            pl.BlockSpec((t, d), lambda b, gi: (rb + b, 0)),
            pl.BlockSpec((t, pg), lambda b, gi: (rb + b, gi)),
            pl.BlockSpec((None, N_MOD, pg), lambda b, gi: (mod_idx(b), 0, gi)),
            pl.BlockSpec((1, pg), lambda b, gi: (0, gi)),
            pl.BlockSpec((None, pg, pg), lambda b, gi: (gi, 0, 0)),
            pl.BlockSpec((1, pg), lambda b, gi: (0, gi)),
        ],
        out_specs=pl.BlockSpec((t, pg), lambda b, gi: (b, gi)),
        out_shape=jax.ShapeDtypeStruct((n_seq * t, d), F32),
        scratch_shapes=[pltpu.VMEM((t, LANES), F32), pltpu.VMEM((t, pg), BF16)],
        compiler_params=_cparams("parallel", "arbitrary"),
        name="pool",
    )(x, x, mods_l, g, w, scale)


def _final_kernel(x_ref, g_ref, o_ref):
    x = x_ref[...]
    o_ref[...] = x * lax.rsqrt(jnp.mean(x * x, axis=-1, keepdims=True) + RMS_EPS) * g_ref[...]


def _final_norm(x, g, tm):
    m, d = x.shape
    return pl.pallas_call(
        _final_kernel,
        grid=(m // tm,),
        in_specs=[pl.BlockSpec((tm, d), lambda i: (i, 0)), pl.BlockSpec((1, d), lambda i: (0, 0))],
        out_specs=pl.BlockSpec((tm, d), lambda i: (i, 0)),
        out_shape=jax.ShapeDtypeStruct((m, d), F32),
        compiler_params=_cparams("parallel"),
        name="final_norm",
    )(x, g)


def _place_cols(vals, n_heads):
    row = jnp.zeros((LANES,), F32)
    for d in range(2):
        row = lax.dynamic_update_slice(row, vals[d].astype(F32), (d * 2 * n_heads,))
    return row.reshape(1, LANES)


def kernel(x_prompt, x_sample, state_gdn, c, c_ctx, w_mod, b_mod, norm_g, ffn_w_in, ffn_w_out, gdn_w_in, gdn_conv, gdn_a_log, gdn_dt_bias, gdn_norm_g, gdn_w_out, pool_w, pool_scale, final_g):
    batch, seq, d = x_prompt.shape
    dec_batch, dec_seq, _ = x_sample.shape
    depth = w_mod.shape[0]
    n_heads = gdn_a_log.shape[-1]
    dk = LANES
    qk = n_heads * dk
    f = ffn_w_out.shape[2]
    fp = FF_TILE * ((f + FF_TILE - 1) // FF_TILE)
    m_p = batch * seq
    m_s = dec_batch * dec_seq
    assert 1 + dec_batch <= MOD_ROWS and gdn_w_in.shape[2] == 4 * qk + 4 * n_heads and 4 * n_heads <= LANES
    tm = next(c_ for c_ in (512, 256, 128, 64) if m_p % c_ == 0 and dec_seq % c_ == 0)
    n_pt = m_p // tm

    def row_of_tile(i):
        return jnp.where(i < n_pt, 0, 1 + ((i - n_pt) * tm) // dec_seq)

    c8 = jnp.zeros((MOD_ROWS, d), F32).at[0].set(c_ctx).at[1:1 + dec_batch].set(c)
    mods = _modulation(c8, w_mod, b_mod).reshape(depth, MOD_ROWS, N_MOD, d)

    xs = _add_pos(x_sample, _grid_pos_embed(dec_seq, d, x_sample.dtype))
    x = jnp.concatenate([x_prompt.reshape(m_p, d), xs.reshape(m_s, d)], axis=0)

    new_states = []
    for l in range(depth):
        mods_l = mods[l]
        ng = norm_g[l]

        def ffn(x, i, sub):
            w_in = ffn_w_in[l, i]
            w_in_p = jnp.concatenate(
                [jnp.pad(w_in[:, :f], ((0, 0), (0, fp - f))), jnp.pad(w_in[:, f:], ((0, 0), (0, fp - f)))],
                axis=1).astype(BF16)
            w_out_p = jnp.pad(ffn_w_out[l, i], ((0, fp - f), (0, 0))).astype(BF16)
            return _ffn(x, mods_l, ng[sub:sub + 1], w_in_p, w_out_p, row_of_tile, tm, sub)

        x = ffn(x, 0, 0)
        mi = l // 2
        if l % 2 == 0:
            w_in = gdn_w_in[mi]
            w_main = w_in[:, :4 * qk].astype(BF16)
            w_ab = jnp.pad(w_in[:, 4 * qk:], ((0, 0), (0, LANES - 4 * n_heads))).astype(BF16)
            proj, ab = _gdn_proj(x, mods_l, ng[1:2], w_main, w_ab, row_of_tile, tm)
            alog_row = _place_cols(gdn_a_log[mi], n_heads)
            dtb_row = _place_cols(gdn_dt_bias[mi], n_heads)
            ng_row = gdn_norm_g[mi].reshape(1, dk).astype(F32)
            core = functools.partial(_gdn_core, proj, ab, gdn_conv[mi], alog_row, dtb_row, ng_row, n_heads=n_heads)
            og_p, s_new = core(None, row0=0, n_seq=batch, t=seq, hb=min(4, n_heads), emit_state=True)
            (og_s,) = core(state_gdn[:, mi], row0=m_p, n_seq=dec_batch, t=dec_seq, hb=min(2, n_heads),
                           emit_state=False)
            new_states.append(s_new)
            og = jnp.concatenate([og_p, og_s], axis=0)
            x = _gdn_out(x, mods_l, og, gdn_w_out[mi].astype(BF16), row_of_tile, tm)
        else:
            pw = pool_w[mi].astype(BF16)
            ps = pool_scale[mi].reshape(1, d)
            pool = functools.partial(_pool, x, mods_l, ng[1:2], pw, ps)
            x_p = pool(row0=0, n_seq=batch, t=seq, mod_row0=0, mod_per_seq=False)
            x_s = pool(row0=m_p, n_seq=dec_batch, t=dec_seq, mod_row0=1, mod_per_seq=True)
            x = jnp.concatenate([x_p, x_s], axis=0)
        x = ffn(x, 1, 2)

    y = _final_norm(x, final_g.reshape(1, d), tm)
    y_prompt = y[:m_p].reshape(batch, seq, d)
    y_sample = y[m_p:].reshape(dec_batch, dec_seq, d)
    new_state_gdn = jnp.stack(new_states, axis=1).astype(state_gdn.dtype)
    return (y_prompt, y_sample, new_state_gdn)
```

```python
import functools
import math

import jax
import jax.numpy as jnp
from jax import lax
from jax.experimental import pallas as pl
from jax.experimental.pallas import tpu as pltpu

F32 = jnp.float32
BF16 = jnp.bfloat16

RMS_EPS = 1e-6
L2_EPS = 1e-6
CHUNK = 64
CONV_K = 5
CONV_PAD = CONV_K // 2
POOL_WINDOWS = (2, 4, 8, 16)
GRID_W = 64
POS_BASE = 10000.0
N_MOD = 9
MOD_ROWS = 8
LANES = 128
FF_TILE = 512
VMEM_LIMIT = 56 * 1024 * 1024


def _cparams(*sem):
    return pltpu.CompilerParams(dimension_semantics=sem, vmem_limit_bytes=VMEM_LIMIT)


def _sigmoid(x):
    return jax.nn.sigmoid(x)


def _silu(x):
    return x * _sigmoid(x)


def _dot(a, b):
    return jnp.dot(a.astype(BF16), b.astype(BF16), preferred_element_type=F32)


def _dot_nt(a, b):
    return lax.dot_general(a.astype(BF16), b.astype(BF16), (((1,), (1,)), ((), ())), preferred_element_type=F32)


def _split(a):
    hi = a.astype(BF16)
    lo = (a - hi.astype(F32)).astype(BF16)
    return hi, lo


def _dot_hi(a, b):
    ah, al = _split(a)
    bh, bl = _split(b)
    d = functools.partial(jnp.dot, preferred_element_type=F32)
    return d(ah, bh) + (d(ah, bl) + d(al, bh))


def _mod_norm(x, g, scale, shift):
    ms = jnp.mean(x * x, axis=-1, keepdims=True)
    return (x * lax.rsqrt(ms + RMS_EPS) * g) * (1.0 + scale) + shift


def _mod_kernel(c_ref, w_ref, b_ref, o_ref):
    s = _silu(c_ref[...])
    o_ref[...] = _dot_hi(s, w_ref[...]) + b_ref[...]


def _modulation(c8, w_mod, b_mod):
    depth, d, nd = w_mod.shape
    tn = next(c_ for c_ in (1024, 512, 256, 128) if nd % c_ == 0)
    return pl.pallas_call(
        _mod_kernel,
        grid=(depth, nd // tn),
        in_specs=[
            pl.BlockSpec((MOD_ROWS, d), lambda l, j: (0, 0)),
            pl.BlockSpec((None, d, tn), lambda l, j: (l, 0, j)),
            pl.BlockSpec((None, 1, tn), lambda l, j: (l, 0, j)),
        ],
        out_specs=pl.BlockSpec((None, MOD_ROWS, tn), lambda l, j: (l, 0, j)),
        out_shape=jax.ShapeDtypeStruct((depth, MOD_ROWS, nd), F32),
        compiler_params=_cparams("parallel", "parallel"),
        name="modulation",
    )(c8, w_mod, b_mod.reshape(depth, 1, nd))


def _add_kernel(x_ref, p_ref, o_ref):
    o_ref[...] = x_ref[...] + p_ref[...]


def _add_pos(xs, pos):
    b, t, d = xs.shape
    return pl.pallas_call(
        _add_kernel,
        grid=(b,),
        in_specs=[pl.BlockSpec((None, t, d), lambda i: (i, 0, 0)), pl.BlockSpec((t, d), lambda i: (0, 0))],
        out_specs=pl.BlockSpec((None, t, d), lambda i: (i, 0, 0)),
        out_shape=jax.ShapeDtypeStruct(xs.shape, xs.dtype),
        compiler_params=_cparams("parallel"),
        name="add_pos",
    )(xs, pos)


def _grid_pos_embed(n_tok, d, dtype):
    idx = jnp.arange(n_tok)
    r = (idx // GRID_W).astype(F32)
    col = (idx % GRID_W).astype(F32)
    n_freq = d // 4
    freqs = jnp.exp(-math.log(POS_BASE) * jnp.arange(n_freq, dtype=F32) / n_freq)
    ar = r[:, None] * freqs
    ac = col[:, None] * freqs
    return jnp.concatenate([jnp.sin(ar), jnp.cos(ar), jnp.sin(ac), jnp.cos(ac)], axis=-1).astype(dtype)


def _ffn_kernel(x_ref, mod_ref, g_ref, wg_ref, wu_ref, wo_ref, o_ref, h_ref, *, sub):
    j = pl.program_id(1)

    @pl.when(j == 0)
    def _():
        h = _mod_norm(x_ref[...], g_ref[...], mod_ref[3 * sub + 1:3 * sub + 2, :], mod_ref[3 * sub:3 * sub + 1, :])
        h_ref[...] = h.astype(BF16)
        o_ref[...] = jnp.zeros_like(o_ref)

    h = h_ref[...]
    gate = jnp.dot(h, wg_ref[...], preferred_element_type=F32)
    up = jnp.dot(h, wu_ref[...], preferred_element_type=F32)
    act = (_silu(gate) * up).astype(BF16)
    o_ref[...] += jnp.dot(act, wo_ref[...], preferred_element_type=F32)

    @pl.when(j == pl.num_programs(1) - 1)
    def _():
        o_ref[...] = x_ref[...] + (0.5 * mod_ref[3 * sub + 2:3 * sub + 3, :]) * o_ref[...]


def _ffn(x, mods_l, g, w_in_p, w_out_p, row_of_tile, tm, sub):
    m, d = x.shape
    fp = w_out_p.shape[0]
    nj = fp // FF_TILE
    return pl.pallas_call(
        functools.partial(_ffn_kernel, sub=sub),
        grid=(m // tm, nj),
        in_specs=[
            pl.BlockSpec((tm, d), lambda i, j: (i, 0)),
            pl.BlockSpec((None, N_MOD, d), lambda i, j: (row_of_tile(i), 0, 0)),
            pl.BlockSpec((1, d), lambda i, j: (0, 0)),
            pl.BlockSpec((d, FF_TILE), lambda i, j: (0, j)),
            pl.BlockSpec((d, FF_TILE), lambda i, j: (0, nj + j)),
            pl.BlockSpec((FF_TILE, d), lambda i, j: (j, 0)),
        ],
        out_specs=pl.BlockSpec((tm, d), lambda i, j: (i, 0)),
        out_shape=jax.ShapeDtypeStruct((m, d), F32),
        scratch_shapes=[pltpu.VMEM((tm, d), BF16)],
        compiler_params=_cparams("parallel", "arbitrary"),
        name="ffn",
    )(x, mods_l, g, w_in_p, w_in_p, w_out_p)


def _proj_kernel(x_ref, mod_ref, g_ref, w_ref, wab_ref, o_ref, ab_ref, h_ref):
    @pl.when(pl.program_id(1) == 0)
    def _():
        h = _mod_norm(x_ref[...], g_ref[...], mod_ref[4:5, :], mod_ref[3:4, :]).astype(BF16)
        h_ref[...] = h
        ab_ref[...] = jnp.dot(h, wab_ref[...], preferred_element_type=F32)

    o_ref[...] = jnp.dot(h_ref[...], w_ref[...], preferred_element_type=F32)


def _gdn_proj(x, mods_l, g, w_main, w_ab, row_of_tile, tm):
    m, d = x.shape
    n = w_main.shape[1]
    tn = 1024
    return pl.pallas_call(
        _proj_kernel,
        grid=(m // tm, n // tn),
        in_specs=[
            pl.BlockSpec((tm, d), lambda i, j: (i, 0)),
            pl.BlockSpec((None, N_MOD, d), lambda i, j: (row_of_tile(i), 0, 0)),
            pl.BlockSpec((1, d), lambda i, j: (0, 0)),
            pl.BlockSpec((d, tn), lambda i, j: (0, j)),
            pl.BlockSpec((d, LANES), lambda i, j: (0, 0)),
        ],
        out_specs=[pl.BlockSpec((tm, tn), lambda i, j: (i, j)), pl.BlockSpec((tm, LANES), lambda i, j: (i, 0))],
        out_shape=[jax.ShapeDtypeStruct((m, n), F32), jax.ShapeDtypeStruct((m, LANES), F32)],
        scratch_shapes=[pltpu.VMEM((tm, d), BF16)],
        compiler_params=_cparams("parallel", "arbitrary"),
        name="gdn_proj",
    )(x, mods_l, g, w_main, w_ab)


def _conv_silu(x_ref, w_ref, t):
    x = x_ref[...]
    row = lax.broadcasted_iota(jnp.int32, x.shape, 0)
    acc = x * w_ref[CONV_PAD:CONV_PAD + 1, :]
    for j in range(CONV_K):
        o = j - CONV_PAD
        if o == 0:
            continue
        xs = pltpu.roll(x, (-o) % t, 0)
        valid = (row + o >= 0) if o < 0 else (row + o < t)
        acc = acc + jnp.where(valid, xs, 0.0) * w_ref[j:j + 1, :]
    return _silu(acc)


def _chunk_cumsum(g, t, reverse):
    pos = lax.broadcasted_iota(jnp.int32, g.shape, 0) % CHUNK
    s = 1
    while s < CHUNK:
        if reverse:
            g = g + jnp.where(pos < CHUNK - s, pltpu.roll(g, t - s, 0), 0.0)
        else:
            g = g + jnp.where(pos >= s, pltpu.roll(g, s, 0), 0.0)
        s *= 2
    return g


def _gdn_kernel(*refs, t, hb, n_heads, has_s0, emit_state):
    (q_ref, k_ref, v_ref, z_ref, ab_ref, cq_ref, ck_ref, cv_ref, alog_ref, dtb_ref, ng_ref), refs = refs[:11], refs[11:]
    if has_s0:
        s0_ref, refs = refs[0], refs[1:]
    og_ref, refs = refs[0], refs[1:]
    if emit_state:
        sout_ref, refs = refs[0], refs[1:]
    qn_ref, kn_ref, vc_ref, gam_ref, beta_ref, s_ref, o_ref = refs

    dk = LANES
    n_chunks = t // CHUNK
    hg = pl.program_id(1)

    qc = _conv_silu(q_ref, cq_ref, t)
    kc = _conv_silu(k_ref, ck_ref, t)
    vc_ref[...] = _conv_silu(v_ref, cv_ref, t)
    for hh in range(hb):
        cs = slice(hh * dk, (hh + 1) * dk)
        qh = qc[:, cs]
        kh = kc[:, cs]
        qn_ref[:, cs] = qh * lax.rsqrt(jnp.sum(qh * qh, axis=-1, keepdims=True) + L2_EPS) * (dk ** -0.5)
        kn_ref[:, cs] = kh * lax.rsqrt(jnp.sum(kh * kh, axis=-1, keepdims=True) + L2_EPS)

    ab = ab_ref[...]
    pre = ab + dtb_ref[...]
    softplus = jnp.maximum(pre, 0.0) + jnp.log1p(jnp.exp(-jnp.abs(pre)))
    g_all = -jnp.exp(alog_ref[...]) * softplus
    beta_all = _sigmoid(ab)
    cum = (_chunk_cumsum(g_all, t, False), _chunk_cumsum(g_all, t, True))
    lane = lax.broadcasted_iota(jnp.int32, ab.shape, 1)
    for d in range(2):
        for hh in range(hb):
            col = d * 2 * n_heads + hg * hb + hh
            gsel = jnp.sum(jnp.where(lane == col, cum[d], 0.0), axis=-1, keepdims=True)
            bsel = jnp.sum(jnp.where(lane == col + n_heads, beta_all, 0.0), axis=-1, keepdims=True)
            gam_ref[d * hb + hh] = jnp.broadcast_to(gsel, ab.shape)
            beta_ref[d * hb + hh] = jnp.broadcast_to(bsel, ab.shape)
            s_ref[d * hb + hh] = s0_ref[d, hh] if has_s0 else jnp.zeros((dk, dk), F32)
    ii = lax.broadcasted_iota(jnp.int32, (CHUNK, CHUNK), 0)
    jj = lax.broadcasted_iota(jnp.int32, (CHUNK, CHUNK), 1)
    eye = (ii == jj).astype(F32)
    pair_masks = []
    s_blk = 1
    while s_blk < CHUNK:
        pair_masks.append(((ii // (2 * s_blk)) == (jj // (2 * s_blk))) & ((ii // s_blk) != (jj // s_blk)))
        s_blk *= 2

    def step(s, carry):
        insts = []
        for d in range(2):
            n = s if d == 0 else n_chunks - 1 - s
            rows = pl.ds(pl.multiple_of(n * CHUNK, CHUNK), CHUNK)
            for hh in range(hb):
                insts.append((d, rows, slice(hh * dk, (hh + 1) * dk), d * hb + hh))
        ld = [(qn_ref[rows, cs], kn_ref[rows, cs], vc_ref[rows, cs], gam_ref[idx, rows, :], beta_ref[idx, rows, :])
              for d, rows, cs, idx in insts]
        a_l, x_l, dec_l, qk_l = [], [], [], []
        for (d, _, _, _), (q, k, v, gc, bc) in zip(insts, ld):
            strict = (ii > jj) if d == 0 else (ii < jj)
            incl = (ii >= jj) if d == 0 else (ii <= jj)
            gr = jnp.concatenate([gc, gc], axis=0).T[:CHUNK, :CHUNK]
            diff = gc[:, :CHUNK] - gr
            dec_incl = jnp.where(incl, jnp.exp(jnp.where(incl, diff, 0.0)), 0.0)
            a = bc[:, :CHUNK] * _dot_nt(k, k) * jnp.where(strict, dec_incl, 0.0)
            a_l.append(a)
            dec_l.append(dec_incl)
            qk_l.append(_dot_nt(q, k))
            x_l.append(eye - jnp.where(pair_masks[0], a, 0.0))
        for pm in pair_masks[1:]:
            t_l = [_dot_hi(x, jnp.where(pm, a, 0.0)) for x, a in zip(x_l, a_l)]
            x_l = [x - _dot_hi(tx, x) for x, tx in zip(x_l, t_l)]
        eg_l = [jnp.exp(gc) for (_, _, _, gc, _) in ld]
        w_l = [_dot_hi(x, k * (bc * eg)) for x, eg, (q, k, v, gc, bc) in zip(x_l, eg_l, ld)]
        u_l = [_dot_hi(x, v * bc) for x, (q, k, v, gc, bc) in zip(x_l, ld)]
        st_l = [s_ref[idx] for _, _, _, idx in insts]
        vn_l = [u - _dot(w, st) for u, w, st in zip(u_l, w_l, st_l)]
        for (d, rows, cs, idx), (q, k, v, gc, bc), eg, qk, dec_incl, st, v_new in zip(
                insts, ld, eg_l, qk_l, dec_l, st_l, vn_l):
            g_last = gc[CHUNK - 1:CHUNK, :] if d == 0 else gc[0:1, :]
            k_d = k * jnp.exp(g_last - gc)
            o_ref[d, rows, cs] = _dot(q * eg, st) + _dot(qk * dec_incl, v_new)
            s_ref[idx] = st * jnp.exp(g_last) + _dot(k_d.T, v_new)
        return carry

    lax.fori_loop(0, n_chunks, step, 0)

    for hh in range(hb):
        cs = slice(hh * dk, (hh + 1) * dk)
        o = o_ref[0, :, cs] + o_ref[1, :, cs]
        o = o * lax.rsqrt(jnp.mean(o * o, axis=-1, keepdims=True) + RMS_EPS) * ng_ref[...] * _silu(z_ref[:, cs])
        og_ref[:, cs] = o.astype(og_ref.dtype)
    if emit_state:
        for d in range(2):
            for hh in range(hb):
                sout_ref[d, hh] = s_ref[d * hb + hh]


def _gdn_core(proj, ab, conv_w, alog_row, dtb_row, ng_row, s0, *, row0, n_seq, t, hb, n_heads, emit_state):
    dk = LANES
    wb = hb * dk
    n_hg = n_heads // hb
    rb = row0 // t
    has_s0 = s0 is not None

    def col_spec(part):
        return pl.BlockSpec((t, wb), lambda b, h: (rb + b, part * n_hg + h))

    def conv_spec(part):
        return pl.BlockSpec((CONV_K, wb), lambda b, h: (0, part * n_hg + h))

    row_spec = pl.BlockSpec((1, LANES), lambda b, h: (0, 0))
    in_specs = [col_spec(0), col_spec(1), col_spec(2), col_spec(3),
                pl.BlockSpec((t, LANES), lambda b, h: (rb + b, 0)),
                conv_spec(0), conv_spec(1), conv_spec(2), row_spec, row_spec, row_spec]
    args = [proj, proj, proj, proj, ab, conv_w, conv_w, conv_w, alog_row, dtb_row, ng_row]
    state_spec = pl.BlockSpec((None, 2, hb, dk, dk), lambda b, h: (b, 0, h, 0, 0))
    if has_s0:
        in_specs.append(state_spec)
        args.append(s0)
    out_specs = [pl.BlockSpec((t, wb), lambda b, h: (b, h))]
    out_shape = [jax.ShapeDtypeStruct((n_seq * t, n_heads * dk), BF16)]
    if emit_state:
        out_specs.append(state_spec)
        out_shape.append(jax.ShapeDtypeStruct((n_seq, 2, n_heads, dk, dk), F32))
    return pl.pallas_call(
        functools.partial(_gdn_kernel, t=t, hb=hb, n_heads=n_heads, has_s0=has_s0, emit_state=emit_state),
        grid=(n_seq, n_hg),
        in_specs=in_specs,
        out_specs=out_specs,
        out_shape=out_shape,
        scratch_shapes=[
            pltpu.VMEM((t, wb), F32), pltpu.VMEM((t, wb), F32), pltpu.VMEM((t, wb), F32),
            pltpu.VMEM((2 * hb, t, LANES), F32), pltpu.VMEM((2 * hb, t, LANES), F32),
            pltpu.VMEM((2 * hb, dk, dk), F32), pltpu.VMEM((2, t, wb), F32),
        ],
        compiler_params=_cparams("parallel", "parallel"),
        name="gdn_core",
    )(*args)


def _out_kernel(x_ref, mod_ref, a_ref, w_ref, o_ref):
    y = jnp.dot(a_ref[...], w_ref[...], preferred_element_type=F32)
    o_ref[...] = x_ref[...] + mod_ref[5:6, :] * y


def _gdn_out(x, mods_l, og, w_out, row_of_tile, tm):
    m, d = x.shape
    kdim = og.shape[1]
    return pl.pallas_call(
        _out_kernel,
        grid=(m // tm,),
        in_specs=[
            pl.BlockSpec((tm, d), lambda i: (i, 0)),
            pl.BlockSpec((None, N_MOD, d), lambda i: (row_of_tile(i), 0, 0)),
            pl.BlockSpec((tm, kdim), lambda i: (i, 0)),
            pl.BlockSpec((kdim, d), lambda i: (0, 0)),
        ],
        out_specs=pl.BlockSpec((tm, d), lambda i: (i, 0)),
        out_shape=jax.ShapeDtypeStruct((m, d), F32),
        compiler_params=_cparams("parallel"),
        name="gdn_out",
    )(x, mods_l, og, w_out)


def _pool_kernel(xf_ref, xg_ref, mod_ref, g_ref, w_ref, sc_ref, o_ref, rstd_ref, diff_ref, *, t):
    gi = pl.program_id(1)

    @pl.when(gi == 0)
    def _():
        xf = xf_ref[...]
        rstd_ref[...] = jnp.broadcast_to(lax.rsqrt(jnp.mean(xf * xf, axis=-1, keepdims=True) + RMS_EPS), rstd_ref.shape)

    xg = xg_ref[...]
    pg = xg.shape[1]
    rstd = rstd_ref[...]
    if pg > LANES:
        rstd = jnp.concatenate([rstd] * (pg // LANES), axis=1)
    h = (xg * rstd[:, :pg] * g_ref[...]) * (1.0 + mod_ref[4:5, :]) + mod_ref[3:4, :]
    row = lax.broadcasted_iota(jnp.int32, h.shape, 0)
    for k, win in enumerate(POOL_WINDOWS):
        @pl.when(gi == k)
        def _(win=win):
            half = win // 2
            acc = h
            for o in range(-half, win - half):
                if o == 0:
                    continue
                hs = pltpu.roll(h, (-o) % t, 0)
                valid = (row + o >= 0) if o < 0 else (row + o < t)
                acc = acc + jnp.where(valid, hs, 0.0)
            cnt = (jnp.minimum(row + (win - half), t) - jnp.maximum(row - half, 0)).astype(F32)
            diff_ref[...] = (acc / cnt - h).astype(BF16)

    y = jnp.dot(diff_ref[...], w_ref[...], preferred_element_type=F32) * sc_ref[...]
    o_ref[...] = xg + mod_ref[5:6, :] * y


def _pool(x, mods_l, g, w, scale, *, row0, n_seq, t, mod_row0, mod_per_seq):
    m, d = x.shape
    ng, pg, _ = w.shape
    rb = row0 // t
    mod_idx = (lambda b: mod_row0 + b) if mod_per_seq else (lambda b: mod_row0)
    return pl.pallas_call(
        functools.partial(_pool_kernel, t=t),
        grid=(n_seq, ng),
        in_specs=[
            pl.BlockSpec((t, d), lambda b, gi: (rb + b, 0)),
            pl.BlockSpec((t, pg), lambda b, gi: (rb + b, gi)),
            pl.BlockSpec((None, N_MOD, pg), lambda b, gi: (mod_idx(b), 0, gi)),
            pl.BlockSpec((1, pg), lambda b, gi: (0, gi)),
            pl.BlockSpec((None, pg, pg), lambda b, gi: (gi, 0, 0)),
            pl.BlockSpec((1, pg), lambda b, gi: (0, gi)),
        ],
        out_specs=pl.BlockSpec((t, pg), lambda b, gi: (b, gi)),
        out_shape=jax.ShapeDtypeStruct((n_seq * t, d), F32),
        scratch_shapes=[pltpu.VMEM((t, LANES), F32), pltpu.VMEM((t, pg), BF16)],
        compiler_params=_cparams("parallel", "arbitrary"),
        name="pool",
    )(x, x, mods_l, g, w, scale)


def _final_kernel(x_ref, g_ref, o_ref):
    x = x_ref[...]
    o_ref[...] = x * lax.rsqrt(jnp.mean(x * x, axis=-1, keepdims=True) + RMS_EPS) * g_ref[...]


def _final_norm(x, g, tm):
    m, d = x.shape
    return pl.pallas_call(
        _final_kernel,
        grid=(m // tm,),
        in_specs=[pl.BlockSpec((tm, d), lambda i: (i, 0)), pl.BlockSpec((1, d), lambda i: (0, 0))],
        out_specs=pl.BlockSpec((tm, d), lambda i: (i, 0)),
        out_shape=jax.ShapeDtypeStruct((m, d), F32),
        compiler_params=_cparams("parallel"),
        name="final_norm",
    )(x, g)


def _place_cols(vals, n_heads):
    row = jnp.zeros((LANES,), F32)
    for d in range(2):
        row = lax.dynamic_update_slice(row, vals[d].astype(F32), (d * 2 * n_heads,))
    return row.reshape(1, LANES)


def kernel(x_prompt, x_sample, state_gdn, c, c_ctx, w_mod, b_mod, norm_g, ffn_w_in, ffn_w_out, gdn_w_in, gdn_conv, gdn_a_log, gdn_dt_bias, gdn_norm_g, gdn_w_out, pool_w, pool_scale, final_g):
    batch, seq, d = x_prompt.shape
    dec_batch, dec_seq, _ = x_sample.shape
    depth = w_mod.shape[0]
    n_heads = gdn_a_log.shape[-1]
    dk = LANES
    qk = n_heads * dk
    f = ffn_w_out.shape[2]
    fp = FF_TILE * ((f + FF_TILE - 1) // FF_TILE)
    m_p = batch * seq
    m_s = dec_batch * dec_seq
    assert 1 + dec_batch <= MOD_ROWS and gdn_w_in.shape[2] == 4 * qk + 4 * n_heads and 4 * n_heads <= LANES
    tm = next(c_ for c_ in (512, 256, 128, 64) if m_p % c_ == 0 and dec_seq % c_ == 0)
    n_pt = m_p // tm

    def row_of_tile(i):
        return jnp.where(i < n_pt, 0, 1 + ((i - n_pt) * tm) // dec_seq)

    c8 = jnp.zeros((MOD_ROWS, d), F32).at[0].set(c_ctx).at[1:1 + dec_batch].set(c)
    mods = _modulation(c8, w_mod, b_mod).reshape(depth, MOD_ROWS, N_MOD, d)

    xs = _add_pos(x_sample, _grid_pos_embed(dec_seq, d, x_sample.dtype))
    x = jnp.concatenate([x_prompt.reshape(m_p, d), xs.reshape(m_s, d)], axis=0)

    new_states = []
    for l in range(depth):
        mods_l = mods[l]
        ng = norm_g[l]

        def ffn(x, i, sub):
            w_in = ffn_w_in[l, i]
            w_in_p = jnp.concatenate(
                [jnp.pad(w_in[:, :f], ((0, 0), (0, fp - f))), jnp.pad(w_in[:, f:], ((0, 0), (0, fp - f)))],
                axis=1).astype(BF16)
            w_out_p = jnp.pad(ffn_w_out[l, i], ((0, fp - f), (0, 0))).astype(BF16)
            return _ffn(x, mods_l, ng[sub:sub + 1], w_in_p, w_out_p, row_of_tile, tm, sub)

        x = ffn(x, 0, 0)
        mi = l // 2
        if l % 2 == 0:
            w_in = gdn_w_in[mi]
            w_main = w_in[:, :4 * qk].astype(BF16)
            w_ab = jnp.pad(w_in[:, 4 * qk:], ((0, 0), (0, LANES - 4 * n_heads))).astype(BF16)
            proj, ab = _gdn_proj(x, mods_l, ng[1:2], w_main, w_ab, row_of_tile, tm)
            alog_row = _place_cols(gdn_a_log[mi], n_heads)
            dtb_row = _place_cols(gdn_dt_bias[mi], n_heads)
            ng_row = gdn_norm_g[mi].reshape(1, dk).astype(F32)
            core = functools.partial(_gdn_core, proj, ab, gdn_conv[mi], alog_row, dtb_row, ng_row, n_heads=n_heads)
            og_p, s_new = core(None, row0=0, n_seq=batch, t=seq, hb=min(4, n_heads), emit_state=True)
            (og_s,) = core(state_gdn[:, mi], row0=m_p, n_seq=dec_batch, t=dec_seq, hb=min(4, n_heads),
                           emit_state=False)
            new_states.append(s_new)
            og = jnp.concatenate([og_p, og_s], axis=0)
            x = _gdn_out(x, mods_l, og, gdn_w_out[mi].astype(BF16), row_of_tile, tm)
        else:
            pw = pool_w[mi].astype(BF16)
            ps = pool_scale[mi].reshape(1, d)
            pool = functools.partial(_pool, x, mods_l, ng[1:2], pw, ps)
            x_p = pool(row0=0, n_seq=batch, t=seq, mod_row0=0, mod_per_seq=False)
            x_s = pool(row0=m_p, n_seq=dec_batch, t=dec_seq, mod_row0=1, mod_per_seq=True)
            x = jnp.concatenate([x_p, x_s], axis=0)
        x = ffn(x, 1, 2)

    y = _final_norm(x, final_g.reshape(1, d), tm)
    y_prompt = y[:m_p].reshape(batch, seq, d)
    y_sample = y[m_p:].reshape(dec_batch, dec_seq, d)
    new_state_gdn = jnp.stack(new_states, axis=1).astype(state_gdn.dtype)
    return (y_prompt, y_sample, new_state_gdn)
```

```python
import functools
import math

import jax
import jax.numpy as jnp
from jax import lax
from jax.experimental import pallas as pl
from jax.experimental.pallas import tpu as pltpu

F32 = jnp.float32
BF16 = jnp.bfloat16

RMS_EPS = 1e-6
L2_EPS = 1e-6
CHUNK = 64
CONV_K = 5
CONV_PAD = CONV_K // 2
POOL_WINDOWS = (2, 4, 8, 16)
GRID_W = 64
POS_BASE = 10000.0
N_MOD = 9
MOD_ROWS = 8
LANES = 128
SUBLANES = 8
FF_TILE = 512
VMEM_LIMIT = 56 * 1024 * 1024


def _cparams(*sem):
    return pltpu.CompilerParams(dimension_semantics=sem, vmem_limit_bytes=VMEM_LIMIT)


def _sigmoid(x):
    return jax.nn.sigmoid(x)


def _silu(x):
    return x * _sigmoid(x)


def _dot(a, b):
    return jnp.dot(a.astype(BF16), b.astype(BF16), preferred_element_type=F32)


def _dot_nt(a, b):
    return lax.dot_general(a.astype(BF16), b.astype(BF16), (((1,), (1,)), ((), ())), preferred_element_type=F32)


def _split(a):
    hi = a.astype(BF16)
    lo = (a - hi.astype(F32)).astype(BF16)
    return hi, lo


def _dot_hi(a, b):
    ah, al = _split(a)
    bh, bl = _split(b)
    d = functools.partial(jnp.dot, preferred_element_type=F32)
    return d(ah, bh) + (d(ah, bl) + d(al, bh))


def _mod_norm(x, g, scale, shift):
    ms = jnp.mean(x * x, axis=-1, keepdims=True)
    return (x * lax.rsqrt(ms + RMS_EPS) * g) * (1.0 + scale) + shift


def _mod_kernel(c_ref, w_ref, b_ref, o_ref):
    s = _silu(c_ref[...])
    o_ref[...] = _dot_hi(s, w_ref[...]) + b_ref[...]


def _modulation(c8, w_mod, b_mod):
    depth, d, nd = w_mod.shape
    tn = next(c_ for c_ in (1024, 512, 256, 128) if nd % c_ == 0)
    return pl.pallas_call(
        _mod_kernel,
        grid=(depth, nd // tn),
        in_specs=[
            pl.BlockSpec((MOD_ROWS, d), lambda l, j: (0, 0)),
            pl.BlockSpec((None, d, tn), lambda l, j: (l, 0, j)),
            pl.BlockSpec((None, 1, tn), lambda l, j: (l, 0, j)),
        ],
        out_specs=pl.BlockSpec((None, MOD_ROWS, tn), lambda l, j: (l, 0, j)),
        out_shape=jax.ShapeDtypeStruct((depth, MOD_ROWS, nd), F32),
        compiler_params=_cparams("parallel", "parallel"),
        name="modulation",
    )(c8, w_mod, b_mod.reshape(depth, 1, nd))


def _add_kernel(x_ref, p_ref, o_ref):
    o_ref[...] = x_ref[...] + p_ref[...]


def _add_pos(xs, pos):
    b, t, d = xs.shape
    return pl.pallas_call(
        _add_kernel,
        grid=(b,),
        in_specs=[pl.BlockSpec((None, t, d), lambda i: (i, 0, 0)), pl.BlockSpec((t, d), lambda i: (0, 0))],
        out_specs=pl.BlockSpec((None, t, d), lambda i: (i, 0, 0)),
        out_shape=jax.ShapeDtypeStruct(xs.shape, xs.dtype),
        compiler_params=_cparams("parallel"),
        name="add_pos",
    )(xs, pos)


def _grid_pos_embed(n_tok, d, dtype):
    idx = jnp.arange(n_tok)
    r = (idx // GRID_W).astype(F32)
    col = (idx % GRID_W).astype(F32)
    n_freq = d // 4
    freqs = jnp.exp(-math.log(POS_BASE) * jnp.arange(n_freq, dtype=F32) / n_freq)
    ar = r[:, None] * freqs
    ac = col[:, None] * freqs
    return jnp.concatenate([jnp.sin(ar), jnp.cos(ar), jnp.sin(ac), jnp.cos(ac)], axis=-1).astype(dtype)


def _ffn_kernel(x_ref, mod_ref, g_ref, wg_ref, wu_ref, wo_ref, o_ref, h_ref, *, sub):
    j = pl.program_id(1)

    @pl.when(j == 0)
    def _():
        h = _mod_norm(x_ref[...], g_ref[...], mod_ref[3 * sub + 1:3 * sub + 2, :], mod_ref[3 * sub:3 * sub + 1, :])
        h_ref[...] = h.astype(BF16)
        o_ref[...] = jnp.zeros_like(o_ref)

    h = h_ref[...]
    gate = jnp.dot(h, wg_ref[...], preferred_element_type=F32)
    up = jnp.dot(h, wu_ref[...], preferred_element_type=F32)
    act = (_silu(gate) * up).astype(BF16)
    o_ref[...] += jnp.dot(act, wo_ref[...], preferred_element_type=F32)

    @pl.when(j == pl.num_programs(1) - 1)
    def _():
        o_ref[...] = x_ref[...] + (0.5 * mod_ref[3 * sub + 2:3 * sub + 3, :]) * o_ref[...]


def _ffn(x, mods_l, g, w_in_p, w_out_p, l, k, row_of_tile, tm, sub):
    m, d = x.shape
    fp = w_out_p.shape[2]
    nj = fp // FF_TILE
    return pl.pallas_call(
        functools.partial(_ffn_kernel, sub=sub),
        grid=(m // tm, nj),
        in_specs=[
            pl.BlockSpec((tm, d), lambda i, j: (i, 0)),
            pl.BlockSpec((None, N_MOD, d), lambda i, j: (row_of_tile(i), 0, 0)),
            pl.BlockSpec((1, d), lambda i, j: (0, 0)),
            pl.BlockSpec((None, None, d, FF_TILE), lambda i, j: (l, k, 0, j)),
            pl.BlockSpec((None, None, d, FF_TILE), lambda i, j: (l, k, 0, nj + j)),
            pl.BlockSpec((None, None, FF_TILE, d), lambda i, j: (l, k, j, 0)),
        ],
        out_specs=pl.BlockSpec((tm, d), lambda i, j: (i, 0)),
        out_shape=jax.ShapeDtypeStruct((m, d), F32),
        scratch_shapes=[pltpu.VMEM((tm, d), BF16)],
        compiler_params=_cparams("parallel", "arbitrary"),
        name="ffn",
    )(x, mods_l, g, w_in_p, w_in_p, w_out_p)


def _proj_kernel(x_ref, mod_ref, g_ref, w_ref, wab_ref, o_ref, ab_ref, h_ref):
    @pl.when(pl.program_id(1) == 0)
    def _():
        h = _mod_norm(x_ref[...], g_ref[...], mod_ref[4:5, :], mod_ref[3:4, :]).astype(BF16)
        h_ref[...] = h
        ab_ref[...] = jnp.dot(h, wab_ref[...], preferred_element_type=F32)

    o_ref[...] = jnp.dot(h_ref[...], w_ref[...], preferred_element_type=F32)


def _gdn_proj(x, mods_l, g, w_main, w_ab, row_of_tile, tm):
    m, d = x.shape
    n = w_main.shape[1]
    tn = 1024
    return pl.pallas_call(
        _proj_kernel,
        grid=(m // tm, n // tn),
        in_specs=[
            pl.BlockSpec((tm, d), lambda i, j: (i, 0)),
            pl.BlockSpec((None, N_MOD, d), lambda i, j: (row_of_tile(i), 0, 0)),
            pl.BlockSpec((1, d), lambda i, j: (0, 0)),
            pl.BlockSpec((d, tn), lambda i, j: (0, j)),
            pl.BlockSpec((d, LANES), lambda i, j: (0, 0)),
        ],
        out_specs=[pl.BlockSpec((tm, tn), lambda i, j: (i, j)), pl.BlockSpec((tm, LANES), lambda i, j: (i, 0))],
        out_shape=[jax.ShapeDtypeStruct((m, n), F32), jax.ShapeDtypeStruct((m, LANES), F32)],
        scratch_shapes=[pltpu.VMEM((tm, d), BF16)],
        compiler_params=_cparams("parallel", "arbitrary"),
        name="gdn_proj",
    )(x, mods_l, g, w_main, w_ab)


def _row_sum_lanes(x):
    hi, lo = _split(x)
    ones = jnp.ones((LANES, LANES), BF16)
    return jnp.dot(hi, ones, preferred_element_type=F32) + jnp.dot(lo, ones, preferred_element_type=F32)


def _conv_silu(x_ref, w_ref, o_ref, pad_ref, t):
    zeros = jnp.zeros((SUBLANES, x_ref.shape[1]), F32)
    pad_ref[0:SUBLANES, :] = zeros
    pad_ref[t + SUBLANES:t + 2 * SUBLANES, :] = zeros
    pad_ref[SUBLANES:t + SUBLANES, :] = x_ref[...]
    acc = x_ref[...] * w_ref[CONV_PAD:CONV_PAD + 1, :]
    for j in range(CONV_K):
        if j != CONV_PAD:
            r0 = SUBLANES + j - CONV_PAD
            acc = acc + pad_ref[r0:r0 + t, :] * w_ref[j:j + 1, :]
    o_ref[...] = _silu(acc)


def _chunk_cumsum(g, t, reverse):
    pos = lax.broadcasted_iota(jnp.int32, g.shape, 0) % CHUNK
    s = 1
    while s < CHUNK:
        if reverse:
            g = g + jnp.where(pos < CHUNK - s, pltpu.roll(g, t - s, 0), 0.0)
        else:
            g = g + jnp.where(pos >= s, pltpu.roll(g, s, 0), 0.0)
        s *= 2
    return g


def _gdn_kernel(*refs, t, hb, n_heads, has_s0, emit_state):
    (q_ref, k_ref, v_ref, z_ref, ab_ref, cq_ref, ck_ref, cv_ref, alog_ref, dtb_ref, ng_ref), refs = refs[:11], refs[11:]
    if has_s0:
        s0_ref, refs = refs[0], refs[1:]
    og_ref, refs = refs[0], refs[1:]
    if emit_state:
        sout_ref, refs = refs[0], refs[1:]
    qn_ref, kn_ref, vc_ref, gam_ref, beta_ref, s_ref, o_ref, pad_ref = refs

    dk = LANES
    n_chunks = t // CHUNK
    hg = pl.program_id(1)

    _conv_silu(q_ref, cq_ref, qn_ref, pad_ref, t)
    _conv_silu(k_ref, ck_ref, kn_ref, pad_ref, t)
    _conv_silu(v_ref, cv_ref, vc_ref, pad_ref, t)
    for hh in range(hb):
        cs = slice(hh * dk, (hh + 1) * dk)
        qh = qn_ref[:, cs]
        kh = kn_ref[:, cs]
        qn_ref[:, cs] = qh * lax.rsqrt(_row_sum_lanes(qh * qh) + L2_EPS) * (dk ** -0.5)
        kn_ref[:, cs] = kh * lax.rsqrt(_row_sum_lanes(kh * kh) + L2_EPS)

    ab = ab_ref[...]
    pre = ab + dtb_ref[...]
    softplus = jnp.maximum(pre, 0.0) + jnp.log1p(jnp.exp(-jnp.abs(pre)))
    g_all = -jnp.exp(alog_ref[...]) * softplus
    beta_all = _sigmoid(ab)
    cum = (_chunk_cumsum(g_all, t, False), _chunk_cumsum(g_all, t, True))
    lane = lax.broadcasted_iota(jnp.int32, ab.shape, 1)
    for d in range(2):
        for hh in range(hb):
            col = d * 2 * n_heads + hg * hb + hh
            gsel = jnp.sum(jnp.where(lane == col, cum[d], 0.0), axis=-1, keepdims=True)
            bsel = jnp.sum(jnp.where(lane == col + n_heads, beta_all, 0.0), axis=-1, keepdims=True)
            gam_ref[d * hb + hh] = jnp.broadcast_to(gsel, ab.shape)
            beta_ref[d * hb + hh] = jnp.broadcast_to(bsel, ab.shape)
            s_ref[d * hb + hh] = s0_ref[d, hh] if has_s0 else jnp.zeros((dk, dk), F32)
    ii = lax.broadcasted_iota(jnp.int32, (CHUNK, LANES), 0)
    lane = lax.broadcasted_iota(jnp.int32, (CHUNK, LANES), 1)
    is_f = lane < CHUNK
    jj = jnp.where(is_f, lane, lane - CHUNK)
    ahead = jnp.where(is_f, ii - jj, jj - ii)
    incl = ahead >= 0
    strict = ahead > 0
    eye = (ii == jj).astype(F32)
    pair_masks = []
    s_blk = 1
    while s_blk < CHUNK:
        pair_masks.append(((ii // (2 * s_blk)) == (jj // (2 * s_blk))) & ((ii // s_blk) != (jj // s_blk)))
        s_blk *= 2

    def block_diag(m):
        zero = jnp.zeros_like(m)
        return jnp.concatenate([jnp.where(is_f, m, zero), jnp.where(is_f, zero, m)], axis=0)

    def diag2(top, bottom):
        return jnp.concatenate([jnp.concatenate([top, jnp.zeros_like(bottom)], axis=1),
                                jnp.concatenate([jnp.zeros_like(top), bottom], axis=1)], axis=0)

    def dot_hi_rhs(a, bh, bl):
        ah, al = _split(a)
        m = a.shape[0]
        r = jnp.dot(jnp.concatenate([ah, al], axis=0), bh, preferred_element_type=F32)
        return r[:m] + (r[m:] + jnp.dot(ah, bl, preferred_element_type=F32))

    def dot_hi_bd(a, b):
        return dot_hi_rhs(a, *_split(block_diag(b)))

    def step(s, carry):
        rows_f = pl.ds(pl.multiple_of(s * CHUNK, CHUNK), CHUNK)
        rows_b = pl.ds(pl.multiple_of((n_chunks - 1 - s) * CHUNK, CHUNK), CHUNK)
        heads = range(hb)
        cs_l = [slice(hh * dk, (hh + 1) * dk) for hh in heads]
        qf = [qn_ref[rows_f, cs] for cs in cs_l]
        kf = [kn_ref[rows_f, cs] for cs in cs_l]
        vf = [vc_ref[rows_f, cs] for cs in cs_l]
        qb = [qn_ref[rows_b, cs] for cs in cs_l]
        kb = [kn_ref[rows_b, cs] for cs in cs_l]
        vb = [vc_ref[rows_b, cs] for cs in cs_l]
        gf = [gam_ref[hh, rows_f, :] for hh in heads]
        gb = [gam_ref[hb + hh, rows_b, :] for hh in heads]
        bf = [beta_ref[hh, rows_f, :] for hh in heads]
        bb = [beta_ref[hb + hh, rows_b, :] for hh in heads]

        a_l, x_l, aqk_l = [], [], []
        for hh in heads:
            gr = jnp.concatenate([gf[hh], gb[hh]], axis=0).T[:CHUNK, :]
            diff = jnp.where(is_f, gf[hh], gb[hh]) - gr
            dec_incl = jnp.where(incl, jnp.exp(jnp.where(incl, diff, 0.0)), 0.0)
            lhs = jnp.concatenate([jnp.concatenate([kf[hh], kb[hh]], axis=1),
                                   jnp.concatenate([qf[hh], qb[hh]], axis=1)], axis=0)
            kq = _dot_nt(lhs, diag2(kf[hh], kb[hh]))
            a = jnp.where(is_f, bf[hh], bb[hh]) * kq[:CHUNK] * jnp.where(strict, dec_incl, 0.0)
            a_l.append(a)
            aqk_l.append(kq[CHUNK:] * dec_incl)
            x_l.append(eye - jnp.where(pair_masks[0], a, 0.0))
        for pm in pair_masks[1:]:
            t_l = [dot_hi_bd(x, jnp.where(pm, a, 0.0)) for x, a in zip(x_l, a_l)]
            x_l = [x - dot_hi_bd(tx, x) for x, tx in zip(x_l, t_l)]
        egf = [jnp.exp(g) for g in gf]
        egb = [jnp.exp(g) for g in gb]
        wu_l = []
        for hh in heads:
            rf_h, rf_l = _split(jnp.concatenate([kf[hh] * (bf[hh] * egf[hh]), vf[hh] * bf[hh]], axis=1))
            rb_h, rb_l = _split(jnp.concatenate([kb[hh] * (bb[hh] * egb[hh]), vb[hh] * bb[hh]], axis=1))
            wu_l.append(dot_hi_rhs(x_l[hh], diag2(rf_h, rb_h), diag2(rf_l, rb_l)))
        stf = [s_ref[hh] for hh in heads]
        stb = [s_ref[hb + hh] for hh in heads]
        wqf = [_dot(jnp.concatenate([wu[:, :dk], q * eg], axis=0), st) for wu, q, eg, st in zip(wu_l, qf, egf, stf)]
        wqb = [_dot(jnp.concatenate([wu[:, 2 * dk:3 * dk], q * eg], axis=0), st)
               for wu, q, eg, st in zip(wu_l, qb, egb, stb)]
        for hh in heads:
            vn_f = wu_l[hh][:, dk:2 * dk] - wqf[hh][:CHUNK]
            vn_b = wu_l[hh][:, 3 * dk:] - wqb[hh][:CHUNK]
            glf = gf[hh][CHUNK - 1:CHUNK, :]
            glb = gb[hh][0:1, :]
            kd = jnp.concatenate([kf[hh] * jnp.exp(glf - gf[hh]), kb[hh] * jnp.exp(glb - gb[hh])], axis=0)
            res = _dot(jnp.concatenate([aqk_l[hh], kd.T], axis=0), diag2(vn_f, vn_b))
            o_ref[0, rows_f, cs_l[hh]] = wqf[hh][CHUNK:] + res[:CHUNK, :dk]
            o_ref[1, rows_b, cs_l[hh]] = wqb[hh][CHUNK:] + res[:CHUNK, dk:]
            s_ref[hh] = stf[hh] * jnp.exp(glf) + res[CHUNK:, :dk]
            s_ref[hb + hh] = stb[hh] * jnp.exp(glb) + res[CHUNK:, dk:]
        return carry

    lax.fori_loop(0, n_chunks, step, 0)

    for hh in range(hb):
        cs = slice(hh * dk, (hh + 1) * dk)
        o = o_ref[0, :, cs] + o_ref[1, :, cs]
        o = o * lax.rsqrt(jnp.mean(o * o, axis=-1, keepdims=True) + RMS_EPS) * ng_ref[...] * _silu(z_ref[:, cs])
        og_ref[:, cs] = o.astype(og_ref.dtype)
    if emit_state:
        for d in range(2):
            for hh in range(hb):
                sout_ref[d, hh] = s_ref[d * hb + hh]


def _gdn_core(proj, ab, conv_w, alog_row, dtb_row, ng_row, s0, *, row0, n_seq, t, hb, n_heads, emit_state):
    dk = LANES
    wb = hb * dk
    n_hg = n_heads // hb
    rb = row0 // t
    has_s0 = s0 is not None

    def col_spec(part):
        return pl.BlockSpec((t, wb), lambda b, h: (rb + b, part * n_hg + h))

    def conv_spec(part):
        return pl.BlockSpec((CONV_K, wb), lambda b, h: (0, part * n_hg + h))

    row_spec = pl.BlockSpec((1, LANES), lambda b, h: (0, 0))
    in_specs = [col_spec(0), col_spec(1), col_spec(2), col_spec(3),
                pl.BlockSpec((t, LANES), lambda b, h: (rb + b, 0)),
                conv_spec(0), conv_spec(1), conv_spec(2), row_spec, row_spec, row_spec]
    args = [proj, proj, proj, proj, ab, conv_w, conv_w, conv_w, alog_row, dtb_row, ng_row]
    state_spec = pl.BlockSpec((None, 2, hb, dk, dk), lambda b, h: (b, 0, h, 0, 0))
    if has_s0:
        in_specs.append(state_spec)
        args.append(s0)
    out_specs = [pl.BlockSpec((t, wb), lambda b, h: (b, h))]
    out_shape = [jax.ShapeDtypeStruct((n_seq * t, n_heads * dk), BF16)]
    if emit_state:
        out_specs.append(state_spec)
        out_shape.append(jax.ShapeDtypeStruct((n_seq, 2, n_heads, dk, dk), F32))
    return pl.pallas_call(
        functools.partial(_gdn_kernel, t=t, hb=hb, n_heads=n_heads, has_s0=has_s0, emit_state=emit_state),
        grid=(n_seq, n_hg),
        in_specs=in_specs,
        out_specs=out_specs,
        out_shape=out_shape,
        scratch_shapes=[
            pltpu.VMEM((t, wb), F32), pltpu.VMEM((t, wb), F32), pltpu.VMEM((t, wb), F32),
            pltpu.VMEM((2 * hb, t, LANES), F32), pltpu.VMEM((2 * hb, t, LANES), F32),
            pltpu.VMEM((2 * hb, dk, dk), F32), pltpu.VMEM((2, t, wb), F32),
            pltpu.VMEM((t + 2 * SUBLANES, wb), F32),
        ],
        compiler_params=_cparams("parallel", "parallel"),
        name="gdn_core",
    )(*args)


def _out_kernel(x_ref, mod_ref, a_ref, w_ref, o_ref):
    y = jnp.dot(a_ref[...], w_ref[...], preferred_element_type=F32)
    o_ref[...] = x_ref[...] + mod_ref[5:6, :] * y


def _gdn_out(x, mods_l, og, w_out, row_of_tile, tm):
    m, d = x.shape
    kdim = og.shape[1]
    return pl.pallas_call(
        _out_kernel,
        grid=(m // tm,),
        in_specs=[
            pl.BlockSpec((tm, d), lambda i: (i, 0)),
            pl.BlockSpec((None, N_MOD, d), lambda i: (row_of_tile(i), 0, 0)),
            pl.BlockSpec((tm, kdim), lambda i: (i, 0)),
            pl.BlockSpec((kdim, d), lambda i: (0, 0)),
        ],
        out_specs=pl.BlockSpec((tm, d), lambda i: (i, 0)),
        out_shape=jax.ShapeDtypeStruct((m, d), F32),
        compiler_params=_cparams("parallel"),
        name="gdn_out",
    )(x, mods_l, og, w_out)


def _pool_kernel(xf_ref, xg_ref, mod_ref, g_ref, w_ref, sc_ref, o_ref, rstd_ref, diff_ref, *, t):
    gi = pl.program_id(1)

    @pl.when(gi == 0)
    def _():
        xf = xf_ref[...]
        rstd_ref[...] = jnp.broadcast_to(lax.rsqrt(jnp.mean(xf * xf, axis=-1, keepdims=True) + RMS_EPS), rstd_ref.shape)

    xg = xg_ref[...]
    pg = xg.shape[1]
    rstd = rstd_ref[...]
    if pg > LANES:
        rstd = jnp.concatenate([rstd] * (pg // LANES), axis=1)
    h = (xg * rstd[:, :pg] * g_ref[...]) * (1.0 + mod_ref[4:5, :]) + mod_ref[3:4, :]
    row = lax.broadcasted_iota(jnp.int32, h.shape, 0)
    for k, win in enumerate(POOL_WINDOWS):
        @pl.when(gi == k)
        def _(win=win):
            half = win // 2
            acc = h
            for o in range(-half, win - half):
                if o == 0:
                    continue
                hs = pltpu.roll(h, (-o) % t, 0)
                valid = (row + o >= 0) if o < 0 else (row + o < t)
                acc = acc + jnp.where(valid, hs, 0.0)
            cnt = (jnp.minimum(row + (win - half), t) - jnp.maximum(row - half, 0)).astype(F32)
            diff_ref[...] = (acc / cnt - h).astype(BF16)

    y = jnp.dot(diff_ref[...], w_ref[...], preferred_element_type=F32) * sc_ref[...]
    o_ref[...] = xg + mod_ref[5:6, :] * y


def _pool(x, mods_l, g, w, scale, *, row0, n_seq, t, mod_row0, mod_per_seq):
    m, d = x.shape
    ng, pg, _ = w.shape
    rb = row0 // t
    mod_idx = (lambda b: mod_row0 + b) if mod_per_seq else (lambda b: mod_row0)
    return pl.pallas_call(
        functools.partial(_pool_kernel, t=t),
        grid=(n_seq, ng),
        in_specs=[
            pl.BlockSpec((t, d), lambda b, gi: (rb + b, 0)),
            pl.BlockSpec((t, pg), lambda b, gi: (rb + b, gi)),
            pl.BlockSpec((None, N_MOD, pg), lambda b, gi: (mod_idx(b), 0, gi)),
            pl.BlockSpec((1, pg), lambda b, gi: (0, gi)),
            pl.BlockSpec((None, pg, pg), lambda b, gi: (gi, 0, 0)),
            pl.BlockSpec((1, pg), lambda b, gi: (0, gi)),
        ],
        out_specs=pl.BlockSpec((t, pg), lambda b, gi: (b, gi)),
        out_shape=jax.ShapeDtypeStruct((n_seq * t, d), F32),
        scratch_shapes=[pltpu.VMEM((t, LANES), F32), pltpu.VMEM((t, pg), BF16)],
        compiler_params=_cparams("parallel", "arbitrary"),
        name="pool",
    )(x, x, mods_l, g, w, scale)


def _final_kernel(x_ref, g_ref, o_ref):
    x = x_ref[...]
    o_ref[...] = x * lax.rsqrt(jnp.mean(x * x, axis=-1, keepdims=True) + RMS_EPS) * g_ref[...]


def _final_norm(x, g, tm):
    m, d = x.shape
    return pl.pallas_call(
        _final_kernel,
        grid=(m // tm,),
        in_specs=[pl.BlockSpec((tm, d), lambda i: (i, 0)), pl.BlockSpec((1, d), lambda i: (0, 0))],
        out_specs=pl.BlockSpec((tm, d), lambda i: (i, 0)),
        out_shape=jax.ShapeDtypeStruct((m, d), F32),
        compiler_params=_cparams("parallel"),
        name="final_norm",
    )(x, g)


def _place_cols(vals, n_heads):
    row = jnp.zeros((LANES,), F32)
    for d in range(2):
        row = lax.dynamic_update_slice(row, vals[d].astype(F32), (d * 2 * n_heads,))
    return row.reshape(1, LANES)


def kernel(x_prompt, x_sample, state_gdn, c, c_ctx, w_mod, b_mod, norm_g, ffn_w_in, ffn_w_out, gdn_w_in, gdn_conv, gdn_a_log, gdn_dt_bias, gdn_norm_g, gdn_w_out, pool_w, pool_scale, final_g):
    batch, seq, d = x_prompt.shape
    dec_batch, dec_seq, _ = x_sample.shape
    depth = w_mod.shape[0]
    n_heads = gdn_a_log.shape[-1]
    dk = LANES
    qk = n_heads * dk
    f = ffn_w_out.shape[2]
    fp = FF_TILE * ((f + FF_TILE - 1) // FF_TILE)
    m_p = batch * seq
    m_s = dec_batch * dec_seq
    assert 1 + dec_batch <= MOD_ROWS and gdn_w_in.shape[2] == 4 * qk + 4 * n_heads and 4 * n_heads <= LANES
    tm = next(c_ for c_ in (512, 256, 128, 64) if m_p % c_ == 0 and dec_seq % c_ == 0)
    n_pt = m_p // tm

    def row_of_tile(i):
        return jnp.where(i < n_pt, 0, 1 + ((i - n_pt) * tm) // dec_seq)

    c8 = jnp.zeros((MOD_ROWS, d), F32).at[0].set(c_ctx).at[1:1 + dec_batch].set(c)
    mods = _modulation(c8, w_mod, b_mod).reshape(depth, MOD_ROWS, N_MOD, d)

    xs = _add_pos(x_sample, _grid_pos_embed(dec_seq, d, x_sample.dtype))
    x = jnp.concatenate([x_prompt.reshape(m_p, d), xs.reshape(m_s, d)], axis=0)

    w_in_p = jnp.pad(ffn_w_in.astype(BF16).reshape(depth, 2, d, 2, f), ((0, 0),) * 4 + ((0, fp - f),))
    w_in_p = w_in_p.reshape(depth, 2, d, 2 * fp)
    w_out_p = jnp.pad(ffn_w_out.astype(BF16), ((0, 0), (0, 0), (0, fp - f), (0, 0)))

    new_states = []
    for l in range(depth):
        mods_l = mods[l]
        ng = norm_g[l]

        def ffn(x, i, sub):
            return _ffn(x, mods_l, ng[sub:sub + 1], w_in_p, w_out_p, l, i, row_of_tile, tm, sub)

        x = ffn(x, 0, 0)
        mi = l // 2
        if l % 2 == 0:
            w_in = gdn_w_in[mi]
            w_main = w_in[:, :4 * qk].astype(BF16)
            w_ab = jnp.pad(w_in[:, 4 * qk:], ((0, 0), (0, LANES - 4 * n_heads))).astype(BF16)
            proj, ab = _gdn_proj(x, mods_l, ng[1:2], w_main, w_ab, row_of_tile, tm)
            alog_row = _place_cols(gdn_a_log[mi], n_heads)
            dtb_row = _place_cols(gdn_dt_bias[mi], n_heads)
            ng_row = gdn_norm_g[mi].reshape(1, dk).astype(F32)
            core = functools.partial(_gdn_core, proj, ab, gdn_conv[mi], alog_row, dtb_row, ng_row, n_heads=n_heads)
            og_p, s_new = core(None, row0=0, n_seq=batch, t=seq, hb=min(8, n_heads), emit_state=True)
            (og_s,) = core(state_gdn[:, mi], row0=m_p, n_seq=dec_batch, t=dec_seq, hb=min(4, n_heads),
                           emit_state=False)
            new_states.append(s_new)
            og = jnp.concatenate([og_p, og_s], axis=0)
            x = _gdn_out(x, mods_l, og, gdn_w_out[mi].astype(BF16), row_of_tile, tm)
        else:
            pw = pool_w[mi].astype(BF16)
            ps = pool_scale[mi].reshape(1, d)
            pool = functools.partial(_pool, x, mods_l, ng[1:2], pw, ps)
            x_p = pool(row0=0, n_seq=batch, t=seq, mod_row0=0, mod_per_seq=False)
            x_s = pool(row0=m_p, n_seq=dec_batch, t=dec_seq, mod_row0=1, mod_per_seq=True)
            x = jnp.concatenate([x_p, x_s], axis=0)
        x = ffn(x, 1, 2)

    y = _final_norm(x, final_g.reshape(1, d), tm)
    y_prompt = y[:m_p].reshape(batch, seq, d)
    y_sample = y[m_p:].reshape(dec_batch, dec_seq, d)
    new_state_gdn = jnp.stack(new_states, axis=1).astype(state_gdn.dtype)
    return (y_prompt, y_sample, new_state_gdn)
```

```python
import functools
import math

import jax
import jax.numpy as jnp
from jax import lax
from jax.experimental import pallas as pl
from jax.experimental.pallas import tpu as pltpu

F32 = jnp.float32
BF16 = jnp.bfloat16

RMS_EPS = 1e-6
L2_EPS = 1e-6
CHUNK = 64
CONV_K = 5
CONV_PAD = CONV_K // 2
POOL_WINDOWS = (2, 4, 8, 16)
GRID_W = 64
POS_BASE = 10000.0
N_MOD = 9
MOD_ROWS = 8
LANES = 128
SUBLANES = 8
FF_TILE = 512
VMEM_LIMIT = 56 * 1024 * 1024


def _cparams(*sem):
    return pltpu.CompilerParams(dimension_semantics=sem, vmem_limit_bytes=VMEM_LIMIT)


def _sigmoid(x):
    return jax.nn.sigmoid(x)


def _silu(x):
    return x * _sigmoid(x)


def _dot(a, b):
    return jnp.dot(a.astype(BF16), b.astype(BF16), preferred_element_type=F32)


def _dot_nt(a, b):
    return lax.dot_general(a.astype(BF16), b.astype(BF16), (((1,), (1,)), ((), ())), preferred_element_type=F32)


def _split(a):
    hi = a.astype(BF16)
    lo = (a - hi.astype(F32)).astype(BF16)
    return hi, lo


def _dot_hi(a, b):
    ah, al = _split(a)
    bh, bl = _split(b)
    d = functools.partial(jnp.dot, preferred_element_type=F32)
    return d(ah, bh) + (d(ah, bl) + d(al, bh))


def _mod_norm(x, g, scale, shift):
    ms = jnp.mean(x * x, axis=-1, keepdims=True)
    return (x * lax.rsqrt(ms + RMS_EPS) * g) * (1.0 + scale) + shift


def _mod_kernel(c_ref, w_ref, b_ref, o_ref):
    s = _silu(c_ref[...])
    o_ref[...] = _dot_hi(s, w_ref[...]) + b_ref[...]


def _modulation(c8, w_mod, b_mod):
    depth, d, nd = w_mod.shape
    tn = next(c_ for c_ in (1024, 512, 256, 128) if nd % c_ == 0)
    return pl.pallas_call(
        _mod_kernel,
        grid=(depth, nd // tn),
        in_specs=[
            pl.BlockSpec((MOD_ROWS, d), lambda l, j: (0, 0)),
            pl.BlockSpec((None, d, tn), lambda l, j: (l, 0, j)),
            pl.BlockSpec((None, 1, tn), lambda l, j: (l, 0, j)),
        ],
        out_specs=pl.BlockSpec((None, MOD_ROWS, tn), lambda l, j: (l, 0, j)),
        out_shape=jax.ShapeDtypeStruct((depth, MOD_ROWS, nd), F32),
        compiler_params=_cparams("parallel", "parallel"),
        name="modulation",
    )(c8, w_mod, b_mod.reshape(depth, 1, nd))


def _add_kernel(x_ref, p_ref, o_ref):
    o_ref[...] = x_ref[...] + p_ref[...]


def _add_pos(xs, pos):
    b, t, d = xs.shape
    return pl.pallas_call(
        _add_kernel,
        grid=(b,),
        in_specs=[pl.BlockSpec((None, t, d), lambda i: (i, 0, 0)), pl.BlockSpec((t, d), lambda i: (0, 0))],
        out_specs=pl.BlockSpec((None, t, d), lambda i: (i, 0, 0)),
        out_shape=jax.ShapeDtypeStruct(xs.shape, xs.dtype),
        compiler_params=_cparams("parallel"),
        name="add_pos",
    )(xs, pos)


def _grid_pos_embed(n_tok, d, dtype):
    idx = jnp.arange(n_tok)
    r = (idx // GRID_W).astype(F32)
    col = (idx % GRID_W).astype(F32)
    n_freq = d // 4
    freqs = jnp.exp(-math.log(POS_BASE) * jnp.arange(n_freq, dtype=F32) / n_freq)
    ar = r[:, None] * freqs
    ac = col[:, None] * freqs
    return jnp.concatenate([jnp.sin(ar), jnp.cos(ar), jnp.sin(ac), jnp.cos(ac)], axis=-1).astype(dtype)


def _ffn_kernel(x_ref, mod_ref, g_ref, wg_ref, wu_ref, wo_ref, o_ref, h_ref, *, sub):
    j = pl.program_id(1)

    @pl.when(j == 0)
    def _():
        h = _mod_norm(x_ref[...], g_ref[...], mod_ref[3 * sub + 1:3 * sub + 2, :], mod_ref[3 * sub:3 * sub + 1, :])
        h_ref[...] = h.astype(BF16)
        o_ref[...] = jnp.zeros_like(o_ref)

    h = h_ref[...]
    gate = jnp.dot(h, wg_ref[...], preferred_element_type=F32)
    up = jnp.dot(h, wu_ref[...], preferred_element_type=F32)
    act = (_silu(gate) * up).astype(BF16)
    o_ref[...] += jnp.dot(act, wo_ref[...], preferred_element_type=F32)

    @pl.when(j == pl.num_programs(1) - 1)
    def _():
        o_ref[...] = x_ref[...] + (0.5 * mod_ref[3 * sub + 2:3 * sub + 3, :]) * o_ref[...]


def _ffn(x, mods_l, g, w_gate_p, w_up_p, w_out_p, l, k, row_of_tile, tm, sub):
    m, d = x.shape
    fp = w_out_p.shape[2]
    nj = fp // FF_TILE
    return pl.pallas_call(
        functools.partial(_ffn_kernel, sub=sub),
        grid=(m // tm, nj),
        in_specs=[
            pl.BlockSpec((tm, d), lambda i, j: (i, 0)),
            pl.BlockSpec((None, N_MOD, d), lambda i, j: (row_of_tile(i), 0, 0)),
            pl.BlockSpec((1, d), lambda i, j: (0, 0)),
            pl.BlockSpec((None, None, d, FF_TILE), lambda i, j: (l, k, 0, j)),
            pl.BlockSpec((None, None, d, FF_TILE), lambda i, j: (l, k, 0, j)),
            pl.BlockSpec((None, None, FF_TILE, d), lambda i, j: (l, k, j, 0)),
        ],
        out_specs=pl.BlockSpec((tm, d), lambda i, j: (i, 0)),
        out_shape=jax.ShapeDtypeStruct((m, d), F32),
        scratch_shapes=[pltpu.VMEM((tm, d), BF16)],
        compiler_params=_cparams("parallel", "arbitrary"),
        name="ffn",
    )(x, mods_l, g, w_gate_p, w_up_p, w_out_p)


def _proj_kernel(x_ref, mod_ref, g_ref, w_ref, wab_ref, o_ref, ab_ref, h_ref):
    @pl.when(pl.program_id(1) == 0)
    def _():
        h = _mod_norm(x_ref[...], g_ref[...], mod_ref[4:5, :], mod_ref[3:4, :]).astype(BF16)
        h_ref[...] = h
        ab_ref[...] = jnp.dot(h, wab_ref[...], preferred_element_type=F32)

    o_ref[...] = jnp.dot(h_ref[...], w_ref[...], preferred_element_type=F32)


def _gdn_proj(x, mods_l, g, w_main, w_ab, row_of_tile, tm):
    m, d = x.shape
    n = w_main.shape[1]
    tn = 1024
    return pl.pallas_call(
        _proj_kernel,
        grid=(m // tm, n // tn),
        in_specs=[
            pl.BlockSpec((tm, d), lambda i, j: (i, 0)),
            pl.BlockSpec((None, N_MOD, d), lambda i, j: (row_of_tile(i), 0, 0)),
            pl.BlockSpec((1, d), lambda i, j: (0, 0)),
            pl.BlockSpec((d, tn), lambda i, j: (0, j)),
            pl.BlockSpec((d, LANES), lambda i, j: (0, 0)),
        ],
        out_specs=[pl.BlockSpec((tm, tn), lambda i, j: (i, j)), pl.BlockSpec((tm, LANES), lambda i, j: (i, 0))],
        out_shape=[jax.ShapeDtypeStruct((m, n), F32), jax.ShapeDtypeStruct((m, LANES), F32)],
        scratch_shapes=[pltpu.VMEM((tm, d), BF16)],
        compiler_params=_cparams("parallel", "arbitrary"),
        name="gdn_proj",
    )(x, mods_l, g, w_main, w_ab)


def _row_sum_lanes(x):
    hi, lo = _split(x)
    ones = jnp.ones((LANES, LANES), BF16)
    return jnp.dot(hi, ones, preferred_element_type=F32) + jnp.dot(lo, ones, preferred_element_type=F32)


def _conv_silu(x_ref, w_ref, o_ref, pad_ref, t):
    zeros = jnp.zeros((SUBLANES, x_ref.shape[1]), F32)
    pad_ref[0:SUBLANES, :] = zeros
    pad_ref[t + SUBLANES:t + 2 * SUBLANES, :] = zeros
    pad_ref[SUBLANES:t + SUBLANES, :] = x_ref[...]
    acc = x_ref[...] * w_ref[CONV_PAD:CONV_PAD + 1, :]
    for j in range(CONV_K):
        if j != CONV_PAD:
            r0 = SUBLANES + j - CONV_PAD
            acc = acc + pad_ref[r0:r0 + t, :] * w_ref[j:j + 1, :]
    o_ref[...] = _silu(acc)


def _chunk_cumsum(g, t, reverse):
    pos = lax.broadcasted_iota(jnp.int32, g.shape, 0) % CHUNK
    s = 1
    while s < CHUNK:
        if reverse:
            g = g + jnp.where(pos < CHUNK - s, pltpu.roll(g, t - s, 0), 0.0)
        else:
            g = g + jnp.where(pos >= s, pltpu.roll(g, s, 0), 0.0)
        s *= 2
    return g


def _gdn_kernel(*refs, t, hb, n_heads, has_s0, emit_state):
    (q_ref, k_ref, v_ref, z_ref, ab_ref, cq_ref, ck_ref, cv_ref, alog_ref, dtb_ref, ng_ref), refs = refs[:11], refs[11:]
    if has_s0:
        s0_ref, refs = refs[0], refs[1:]
    og_ref, refs = refs[0], refs[1:]
    if emit_state:
        sout_ref, refs = refs[0], refs[1:]
    qn_ref, kn_ref, vc_ref, gam_ref, beta_ref, s_ref, o_ref, pad_ref = refs

    dk = LANES
    n_chunks = t // CHUNK
    hg = pl.program_id(1)

    _conv_silu(q_ref, cq_ref, qn_ref, pad_ref, t)
    _conv_silu(k_ref, ck_ref, kn_ref, pad_ref, t)
    _conv_silu(v_ref, cv_ref, vc_ref, pad_ref, t)
    for hh in range(hb):
        cs = slice(hh * dk, (hh + 1) * dk)
        qh = qn_ref[:, cs]
        kh = kn_ref[:, cs]
        qn_ref[:, cs] = qh * lax.rsqrt(_row_sum_lanes(qh * qh) + L2_EPS) * (dk ** -0.5)
        kn_ref[:, cs] = kh * lax.rsqrt(_row_sum_lanes(kh * kh) + L2_EPS)

    ab = ab_ref[...]
    pre = ab + dtb_ref[...]
    softplus = jnp.maximum(pre, 0.0) + jnp.log1p(jnp.exp(-jnp.abs(pre)))
    g_all = -jnp.exp(alog_ref[...]) * softplus
    beta_all = _sigmoid(ab)
    cum = (_chunk_cumsum(g_all, t, False), _chunk_cumsum(g_all, t, True))
    lane = lax.broadcasted_iota(jnp.int32, ab.shape, 1)
    for d in range(2):
        for hh in range(hb):
            col = d * 2 * n_heads + hg * hb + hh
            gsel = jnp.sum(jnp.where(lane == col, cum[d], 0.0), axis=-1, keepdims=True)
            bsel = jnp.sum(jnp.where(lane == col + n_heads, beta_all, 0.0), axis=-1, keepdims=True)
            gam_ref[d * hb + hh] = jnp.broadcast_to(gsel, ab.shape)
            beta_ref[d * hb + hh] = jnp.broadcast_to(bsel, ab.shape)
            s_ref[d * hb + hh] = s0_ref[d, hh] if has_s0 else jnp.zeros((dk, dk), F32)
    ii = lax.broadcasted_iota(jnp.int32, (CHUNK, LANES), 0)
    lane = lax.broadcasted_iota(jnp.int32, (CHUNK, LANES), 1)
    is_f = lane < CHUNK
    jj = jnp.where(is_f, lane, lane - CHUNK)
    ahead = jnp.where(is_f, ii - jj, jj - ii)
    incl = ahead >= 0
    strict = ahead > 0
    eye = (ii == jj).astype(F32)
    pair_masks = []
    s_blk = 1
    while s_blk < CHUNK:
        pair_masks.append(((ii // (2 * s_blk)) == (jj // (2 * s_blk))) & ((ii // s_blk) != (jj // s_blk)))
        s_blk *= 2

    def block_diag(m):
        zero = jnp.zeros_like(m)
        return jnp.concatenate([jnp.where(is_f, m, zero), jnp.where(is_f, zero, m)], axis=0)

    def diag2(top, bottom):
        return jnp.concatenate([jnp.concatenate([top, jnp.zeros_like(bottom)], axis=1),
                                jnp.concatenate([jnp.zeros_like(top), bottom], axis=1)], axis=0)

    def dot_lhs_hi(a, b):
        m = a.shape[0]
        r = jnp.dot(jnp.concatenate(_split(a), axis=0), b, preferred_element_type=F32)
        return r[:m] + r[m:]

    def dot_rhs_hi(a, b):
        n = b.shape[1]
        r = jnp.dot(a, jnp.concatenate(_split(b), axis=1), preferred_element_type=F32)
        return r[:, :n] + r[:, n:]

    def step(s, carry):
        rows_f = pl.ds(pl.multiple_of(s * CHUNK, CHUNK), CHUNK)
        rows_b = pl.ds(pl.multiple_of((n_chunks - 1 - s) * CHUNK, CHUNK), CHUNK)
        heads = range(hb)
        cs_l = [slice(hh * dk, (hh + 1) * dk) for hh in heads]
        qf = [qn_ref[rows_f, cs] for cs in cs_l]
        kf = [kn_ref[rows_f, cs] for cs in cs_l]
        vf = [vc_ref[rows_f, cs] for cs in cs_l]
        qb = [qn_ref[rows_b, cs] for cs in cs_l]
        kb = [kn_ref[rows_b, cs] for cs in cs_l]
        vb = [vc_ref[rows_b, cs] for cs in cs_l]
        gf = [gam_ref[hh, rows_f, :] for hh in heads]
        gb = [gam_ref[hb + hh, rows_b, :] for hh in heads]
        bf = [beta_ref[hh, rows_f, :] for hh in heads]
        bb = [beta_ref[hb + hh, rows_b, :] for hh in heads]

        a_l, x_l, aqk_l = [], [], []
        for hh in heads:
            gr = jnp.concatenate([gf[hh], gb[hh]], axis=0).T[:CHUNK, :]
            diff = jnp.where(is_f, gf[hh], gb[hh]) - gr
            dec_incl = jnp.where(incl, jnp.exp(jnp.where(incl, diff, 0.0)), 0.0)
            lhs = jnp.concatenate([jnp.concatenate([kf[hh], kb[hh]], axis=1),
                                   jnp.concatenate([qf[hh], qb[hh]], axis=1)], axis=0)
            kq = _dot_nt(lhs, diag2(kf[hh], kb[hh]))
            a = jnp.where(is_f, bf[hh], bb[hh]) * kq[:CHUNK] * jnp.where(strict, dec_incl, 0.0)
            a = a.astype(BF16).astype(F32)
            a_l.append(a)
            aqk_l.append(kq[CHUNK:] * dec_incl)
            x_l.append(eye - jnp.where(pair_masks[0], a, 0.0))
        for pm in pair_masks[1:]:
            t_l = [dot_lhs_hi(x, block_diag(jnp.where(pm, a, 0.0)).astype(BF16)) for x, a in zip(x_l, a_l)]
            x_l = [x - dot_rhs_hi(tx.astype(BF16), block_diag(x)) for x, tx in zip(x_l, t_l)]
        egf = [jnp.exp(g) for g in gf]
        egb = [jnp.exp(g) for g in gb]
        wu_l = []
        for hh in heads:
            r_f = jnp.concatenate([kf[hh] * (bf[hh] * egf[hh]), vf[hh] * bf[hh]], axis=1).astype(BF16)
            r_b = jnp.concatenate([kb[hh] * (bb[hh] * egb[hh]), vb[hh] * bb[hh]], axis=1).astype(BF16)
            wu_l.append(dot_lhs_hi(x_l[hh], diag2(r_f, r_b)))
        stf = [s_ref[hh] for hh in heads]
        stb = [s_ref[hb + hh] for hh in heads]
        wqf = [_dot(jnp.concatenate([wu[:, :dk], q * eg], axis=0), st) for wu, q, eg, st in zip(wu_l, qf, egf, stf)]
        wqb = [_dot(jnp.concatenate([wu[:, 2 * dk:3 * dk], q * eg], axis=0), st)
               for wu, q, eg, st in zip(wu_l, qb, egb, stb)]
        for hh in heads:
            vn_f = wu_l[hh][:, dk:2 * dk] - wqf[hh][:CHUNK]
            vn_b = wu_l[hh][:, 3 * dk:] - wqb[hh][:CHUNK]
            glf = gf[hh][CHUNK - 1:CHUNK, :]
            glb = gb[hh][0:1, :]
            kd = jnp.concatenate([kf[hh] * jnp.exp(glf - gf[hh]), kb[hh] * jnp.exp(glb - gb[hh])], axis=0)
            res = _dot(jnp.concatenate([aqk_l[hh], kd.T], axis=0), diag2(vn_f, vn_b))
            o_ref[0, rows_f, cs_l[hh]] = wqf[hh][CHUNK:] + res[:CHUNK, :dk]
            o_ref[1, rows_b, cs_l[hh]] = wqb[hh][CHUNK:] + res[:CHUNK, dk:]
            s_ref[hh] = stf[hh] * jnp.exp(glf) + res[CHUNK:, :dk]
            s_ref[hb + hh] = stb[hh] * jnp.exp(glb) + res[CHUNK:, dk:]
        return carry

    lax.fori_loop(0, n_chunks, step, 0)

    for hh in range(hb):
        cs = slice(hh * dk, (hh + 1) * dk)
        o = o_ref[0, :, cs] + o_ref[1, :, cs]
        o = o * lax.rsqrt(jnp.mean(o * o, axis=-1, keepdims=True) + RMS_EPS) * ng_ref[...] * _silu(z_ref[:, cs])
        og_ref[:, cs] = o.astype(og_ref.dtype)
    if emit_state:
        for d in range(2):
            for hh in range(hb):
                sout_ref[d, hh] = s_ref[d * hb + hh]


def _gdn_core(proj, ab, conv_w, alog_row, dtb_row, ng_row, s0, *, row0, n_seq, t, hb, n_heads, emit_state):
    dk = LANES
    wb = hb * dk
    n_hg = n_heads // hb
    rb = row0 // t
    has_s0 = s0 is not None

    def col_spec(part):
        return pl.BlockSpec((t, wb), lambda b, h: (rb + b, part * n_hg + h))

    def conv_spec(part):
        return pl.BlockSpec((CONV_K, wb), lambda b, h: (0, part * n_hg + h))

    row_spec = pl.BlockSpec((1, LANES), lambda b, h: (0, 0))
    in_specs = [col_spec(0), col_spec(1), col_spec(2), col_spec(3),
                pl.BlockSpec((t, LANES), lambda b, h: (rb + b, 0)),
                conv_spec(0), conv_spec(1), conv_spec(2), row_spec, row_spec, row_spec]
    args = [proj, proj, proj, proj, ab, conv_w, conv_w, conv_w, alog_row, dtb_row, ng_row]
    state_spec = pl.BlockSpec((None, 2, hb, dk, dk), lambda b, h: (b, 0, h, 0, 0))
    if has_s0:
        in_specs.append(state_spec)
        args.append(s0)
    out_specs = [pl.BlockSpec((t, wb), lambda b, h: (b, h))]
    out_shape = [jax.ShapeDtypeStruct((n_seq * t, n_heads * dk), BF16)]
    if emit_state:
        out_specs.append(state_spec)
        out_shape.append(jax.ShapeDtypeStruct((n_seq, 2, n_heads, dk, dk), F32))
    return pl.pallas_call(
        functools.partial(_gdn_kernel, t=t, hb=hb, n_heads=n_heads, has_s0=has_s0, emit_state=emit_state),
        grid=(n_seq, n_hg),
        in_specs=in_specs,
        out_specs=out_specs,
        out_shape=out_shape,
        scratch_shapes=[
            pltpu.VMEM((t, wb), F32), pltpu.VMEM((t, wb), F32), pltpu.VMEM((t, wb), F32),
            pltpu.VMEM((2 * hb, t, LANES), F32), pltpu.VMEM((2 * hb, t, LANES), F32),
            pltpu.VMEM((2 * hb, dk, dk), F32), pltpu.VMEM((2, t, wb), F32),
            pltpu.VMEM((t + 2 * SUBLANES, wb), F32),
        ],
        compiler_params=_cparams("parallel", "parallel"),
        name="gdn_core",
    )(*args)


def _out_kernel(x_ref, mod_ref, a_ref, w_ref, o_ref):
    y = jnp.dot(a_ref[...], w_ref[...], preferred_element_type=F32)
    o_ref[...] = x_ref[...] + mod_ref[5:6, :] * y


def _gdn_out(x, mods_l, og, w_out, row_of_tile, tm):
    m, d = x.shape
    kdim = og.shape[1]
    return pl.pallas_call(
        _out_kernel,
        grid=(m // tm,),
        in_specs=[
            pl.BlockSpec((tm, d), lambda i: (i, 0)),
            pl.BlockSpec((None, N_MOD, d), lambda i: (row_of_tile(i), 0, 0)),
            pl.BlockSpec((tm, kdim), lambda i: (i, 0)),
            pl.BlockSpec((kdim, d), lambda i: (0, 0)),
        ],
        out_specs=pl.BlockSpec((tm, d), lambda i: (i, 0)),
        out_shape=jax.ShapeDtypeStruct((m, d), F32),
        compiler_params=_cparams("parallel"),
        name="gdn_out",
    )(x, mods_l, og, w_out)


def _pool_kernel(xf_ref, xg_ref, mod_ref, g_ref, w_ref, sc_ref, o_ref, rstd_ref, diff_ref, *, t):
    gi = pl.program_id(1)

    @pl.when(gi == 0)
    def _():
        xf = xf_ref[...]
        rstd_ref[...] = jnp.broadcast_to(lax.rsqrt(jnp.mean(xf * xf, axis=-1, keepdims=True) + RMS_EPS), rstd_ref.shape)

    xg = xg_ref[...]
    pg = xg.shape[1]
    rstd = rstd_ref[...]
    if pg > LANES:
        rstd = jnp.concatenate([rstd] * (pg // LANES), axis=1)
    h = (xg * rstd[:, :pg] * g_ref[...]) * (1.0 + mod_ref[4:5, :]) + mod_ref[3:4, :]
    row = lax.broadcasted_iota(jnp.int32, h.shape, 0)
    for k, win in enumerate(POOL_WINDOWS):
        @pl.when(gi == k)
        def _(win=win):
            half = win // 2
            acc = h
            for o in range(-half, win - half):
                if o == 0:
                    continue
                hs = pltpu.roll(h, (-o) % t, 0)
                valid = (row + o >= 0) if o < 0 else (row + o < t)
                acc = acc + jnp.where(valid, hs, 0.0)
            cnt = (jnp.minimum(row + (win - half), t) - jnp.maximum(row - half, 0)).astype(F32)
            diff_ref[...] = (acc / cnt - h).astype(BF16)

    y = jnp.dot(diff_ref[...], w_ref[...], preferred_element_type=F32) * sc_ref[...]
    o_ref[...] = xg + mod_ref[5:6, :] * y


def _pool(x, mods_l, g, w, scale, *, row0, n_seq, t, mod_row0, mod_per_seq):
    m, d = x.shape
    ng, pg, _ = w.shape
    rb = row0 // t
    mod_idx = (lambda b: mod_row0 + b) if mod_per_seq else (lambda b: mod_row0)
    return pl.pallas_call(
        functools.partial(_pool_kernel, t=t),
        grid=(n_seq, ng),
        in_specs=[
            pl.BlockSpec((t, d), lambda b, gi: (rb + b, 0)),
            pl.BlockSpec((t, pg), lambda b, gi: (rb + b, gi)),
            pl.BlockSpec((None, N_MOD, pg), lambda b, gi: (mod_idx(b), 0, gi)),
            pl.BlockSpec((1, pg), lambda b, gi: (0, gi)),
            pl.BlockSpec((None, pg, pg), lambda b, gi: (gi, 0, 0)),
            pl.BlockSpec((1, pg), lambda b, gi: (0, gi)),
        ],
        out_specs=pl.BlockSpec((t, pg), lambda b, gi: (b, gi)),
        out_shape=jax.ShapeDtypeStruct((n_seq * t, d), F32),
        scratch_shapes=[pltpu.VMEM((t, LANES), F32), pltpu.VMEM((t, pg), BF16)],
        compiler_params=_cparams("parallel", "arbitrary"),
        name="pool",
    )(x, x, mods_l, g, w, scale)


def _final_kernel(x_ref, g_ref, o_ref):
    x = x_ref[...]
    o_ref[...] = x * lax.rsqrt(jnp.mean(x * x, axis=-1, keepdims=True) + RMS_EPS) * g_ref[...]


def _final_norm(x, g, tm):
    m, d = x.shape
    return pl.pallas_call(
        _final_kernel,
        grid=(m // tm,),
        in_specs=[pl.BlockSpec((tm, d), lambda i: (i, 0)), pl.BlockSpec((1, d), lambda i: (0, 0))],
        out_specs=pl.BlockSpec((tm, d), lambda i: (i, 0)),
        out_shape=jax.ShapeDtypeStruct((m, d), F32),
        compiler_params=_cparams("parallel"),
        name="final_norm",
    )(x, g)


def _place_cols(vals, n_heads):
    row = jnp.zeros((LANES,), F32)
    for d in range(2):
        row = lax.dynamic_update_slice(row, vals[d].astype(F32), (d * 2 * n_heads,))
    return row.reshape(1, LANES)


def kernel(x_prompt, x_sample, state_gdn, c, c_ctx, w_mod, b_mod, norm_g, ffn_w_in, ffn_w_out, gdn_w_in, gdn_conv, gdn_a_log, gdn_dt_bias, gdn_norm_g, gdn_w_out, pool_w, pool_scale, final_g):
    batch, seq, d = x_prompt.shape
    dec_batch, dec_seq, _ = x_sample.shape
    depth = w_mod.shape[0]
    n_heads = gdn_a_log.shape[-1]
    dk = LANES
    qk = n_heads * dk
    f = ffn_w_out.shape[2]
    fp = FF_TILE * ((f + FF_TILE - 1) // FF_TILE)
    m_p = batch * seq
    m_s = dec_batch * dec_seq
    assert 1 + dec_batch <= MOD_ROWS and gdn_w_in.shape[2] == 4 * qk + 4 * n_heads and 4 * n_heads <= LANES
    tm = next(c_ for c_ in (512, 256, 128, 64) if m_p % c_ == 0 and dec_seq % c_ == 0)
    n_pt = m_p // tm

    def row_of_tile(i):
        return jnp.where(i < n_pt, 0, 1 + ((i - n_pt) * tm) // dec_seq)

    c8 = jnp.zeros((MOD_ROWS, d), F32).at[0].set(c_ctx).at[1:1 + dec_batch].set(c)
    mods = _modulation(c8, w_mod, b_mod).reshape(depth, MOD_ROWS, N_MOD, d)

    xs = _add_pos(x_sample, _grid_pos_embed(dec_seq, d, x_sample.dtype))
    x = jnp.concatenate([x_prompt.reshape(m_p, d), xs.reshape(m_s, d)], axis=0)

    pad_f = ((0, 0), (0, 0), (0, 0), (0, fp - f))
    w_gate_p = jnp.pad(ffn_w_in[..., :f], pad_f).astype(BF16)
    w_up_p = jnp.pad(ffn_w_in[..., f:], pad_f).astype(BF16)
    w_out_p = jnp.pad(ffn_w_out, ((0, 0), (0, 0), (0, fp - f), (0, 0))).astype(BF16)

    new_states = []
    for l in range(depth):
        mods_l = mods[l]
        ng = norm_g[l]

        def ffn(x, i, sub):
            return _ffn(x, mods_l, ng[sub:sub + 1], w_gate_p, w_up_p, w_out_p, l, i, row_of_tile, tm, sub)

        x = ffn(x, 0, 0)
        mi = l // 2
        if l % 2 == 0:
            w_in = gdn_w_in[mi]
            w_main = w_in[:, :4 * qk].astype(BF16)
            w_ab = jnp.pad(w_in[:, 4 * qk:], ((0, 0), (0, LANES - 4 * n_heads))).astype(BF16)
            proj, ab = _gdn_proj(x, mods_l, ng[1:2], w_main, w_ab, row_of_tile, tm)
            alog_row = _place_cols(gdn_a_log[mi], n_heads)
            dtb_row = _place_cols(gdn_dt_bias[mi], n_heads)
            ng_row = gdn_norm_g[mi].reshape(1, dk).astype(F32)
            core = functools.partial(_gdn_core, proj, ab, gdn_conv[mi], alog_row, dtb_row, ng_row, n_heads=n_heads)
            og_p, s_new = core(None, row0=0, n_seq=batch, t=seq, hb=min(16, n_heads), emit_state=True)
            (og_s,) = core(state_gdn[:, mi], row0=m_p, n_seq=dec_batch, t=dec_seq, hb=min(4, n_heads),
                           emit_state=False)
            new_states.append(s_new)
            og = jnp.concatenate([og_p, og_s], axis=0)
            x = _gdn_out(x, mods_l, og, gdn_w_out[mi].astype(BF16), row_of_tile, tm)
        else:
            pw = pool_w[mi].astype(BF16)
            ps = pool_scale[mi].reshape(1, d)
            pool = functools.partial(_pool, x, mods_l, ng[1:2], pw, ps)
            x_p = pool(row0=0, n_seq=batch, t=seq, mod_row0=0, mod_per_seq=False)
            x_s = pool(row0=m_p, n_seq=dec_batch, t=dec_seq, mod_row0=1, mod_per_seq=True)
            x = jnp.concatenate([x_p, x_s], axis=0)
        x = ffn(x, 1, 2)

    y = _final_norm(x, final_g.reshape(1, d), tm)
    y_prompt = y[:m_p].reshape(batch, seq, d)
    y_sample = y[m_p:].reshape(dec_batch, dec_seq, d)
    new_state_gdn = jnp.stack(new_states, axis=1).astype(state_gdn.dtype)
    return (y_prompt, y_sample, new_state_gdn)
```

```python
import functools
import math

import jax
import jax.numpy as jnp
from jax import lax
from jax.experimental import pallas as pl
from jax.experimental.pallas import tpu as pltpu

F32 = jnp.float32
BF16 = jnp.bfloat16

RMS_EPS = 1e-6
L2_EPS = 1e-6
CHUNK = 64
CONV_K = 5
CONV_PAD = CONV_K // 2
POOL_WINDOWS = (2, 4, 8, 16)
GRID_W = 64
POS_BASE = 10000.0
N_MOD = 9
MOD_ROWS = 8
LANES = 128
SUBLANES = 8
FF_TILE = 512
VMEM_LIMIT = 56 * 1024 * 1024


def _cparams(*sem):
    return pltpu.CompilerParams(dimension_semantics=sem, vmem_limit_bytes=VMEM_LIMIT)


def _sigmoid(x):
    return jax.nn.sigmoid(x)


def _silu(x):
    return x * _sigmoid(x)


def _dot(a, b):
    return jnp.dot(a.astype(BF16), b.astype(BF16), preferred_element_type=F32)


def _dot_nt(a, b):
    return lax.dot_general(a.astype(BF16), b.astype(BF16), (((1,), (1,)), ((), ())), preferred_element_type=F32)


def _split(a):
    hi = a.astype(BF16)
    lo = (a - hi.astype(F32)).astype(BF16)
    return hi, lo


def _dot_hi(a, b):
    ah, al = _split(a)
    bh, bl = _split(b)
    d = functools.partial(jnp.dot, preferred_element_type=F32)
    return d(ah, bh) + (d(ah, bl) + d(al, bh))


def _mod_norm(x, g, scale, shift):
    ms = jnp.mean(x * x, axis=-1, keepdims=True)
    return (x * lax.rsqrt(ms + RMS_EPS) * g) * (1.0 + scale) + shift


def _mod_kernel(c_ref, w_ref, b_ref, o_ref):
    s = _silu(c_ref[...])
    o_ref[...] = _dot_hi(s, w_ref[...]) + b_ref[...]


def _modulation(c8, w_mod, b_mod):
    depth, d, nd = w_mod.shape
    tn = next(c_ for c_ in (1024, 512, 256, 128) if nd % c_ == 0)
    return pl.pallas_call(
        _mod_kernel,
        grid=(depth, nd // tn),
        in_specs=[
            pl.BlockSpec((MOD_ROWS, d), lambda l, j: (0, 0)),
            pl.BlockSpec((None, d, tn), lambda l, j: (l, 0, j)),
            pl.BlockSpec((None, 1, tn), lambda l, j: (l, 0, j)),
        ],
        out_specs=pl.BlockSpec((None, MOD_ROWS, tn), lambda l, j: (l, 0, j)),
        out_shape=jax.ShapeDtypeStruct((depth, MOD_ROWS, nd), F32),
        compiler_params=_cparams("parallel", "parallel"),
        name="modulation",
    )(c8, w_mod, b_mod.reshape(depth, 1, nd))


def _add_kernel(x_ref, p_ref, o_ref):
    o_ref[...] = x_ref[...] + p_ref[...]


def _add_pos(xs, pos):
    b, t, d = xs.shape
    return pl.pallas_call(
        _add_kernel,
        grid=(b,),
        in_specs=[pl.BlockSpec((None, t, d), lambda i: (i, 0, 0)), pl.BlockSpec((t, d), lambda i: (0, 0))],
        out_specs=pl.BlockSpec((None, t, d), lambda i: (i, 0, 0)),
        out_shape=jax.ShapeDtypeStruct(xs.shape, xs.dtype),
        compiler_params=_cparams("parallel"),
        name="add_pos",
    )(xs, pos)


def _grid_pos_embed(n_tok, d, dtype):
    idx = jnp.arange(n_tok)
    r = (idx // GRID_W).astype(F32)
    col = (idx % GRID_W).astype(F32)
    n_freq = d // 4
    freqs = jnp.exp(-math.log(POS_BASE) * jnp.arange(n_freq, dtype=F32) / n_freq)
    ar = r[:, None] * freqs
    ac = col[:, None] * freqs
    return jnp.concatenate([jnp.sin(ar), jnp.cos(ar), jnp.sin(ac), jnp.cos(ac)], axis=-1).astype(dtype)


def _ffn_kernel(x_ref, mod_ref, g_ref, wg_ref, wu_ref, wo_ref, o_ref, h_ref, *, sub):
    j = pl.program_id(1)

    @pl.when(j == 0)
    def _():
        h = _mod_norm(x_ref[...], g_ref[...], mod_ref[3 * sub + 1:3 * sub + 2, :], mod_ref[3 * sub:3 * sub + 1, :])
        h_ref[...] = h.astype(BF16)
        o_ref[...] = jnp.zeros_like(o_ref)

    h = h_ref[...]
    gate = jnp.dot(h, wg_ref[...], preferred_element_type=F32)
    up = jnp.dot(h, wu_ref[...], preferred_element_type=F32)
    act = (_silu(gate) * up).astype(BF16)
    o_ref[...] += jnp.dot(act, wo_ref[...], preferred_element_type=F32)

    @pl.when(j == pl.num_programs(1) - 1)
    def _():
        o_ref[...] = x_ref[...] + (0.5 * mod_ref[3 * sub + 2:3 * sub + 3, :]) * o_ref[...]


def _ffn(x, mods_l, g, w_gate_p, w_up_p, w_out_p, l, k, row_of_tile, tm, sub):
    m, d = x.shape
    fp = w_out_p.shape[2]
    nj = fp // FF_TILE
    return pl.pallas_call(
        functools.partial(_ffn_kernel, sub=sub),
        grid=(m // tm, nj),
        in_specs=[
            pl.BlockSpec((tm, d), lambda i, j: (i, 0)),
            pl.BlockSpec((None, N_MOD, d), lambda i, j: (row_of_tile(i), 0, 0)),
            pl.BlockSpec((1, d), lambda i, j: (0, 0)),
            pl.BlockSpec((None, None, d, FF_TILE), lambda i, j: (l, k, 0, j)),
            pl.BlockSpec((None, None, d, FF_TILE), lambda i, j: (l, k, 0, j)),
            pl.BlockSpec((None, None, FF_TILE, d), lambda i, j: (l, k, j, 0)),
        ],
        out_specs=pl.BlockSpec((tm, d), lambda i, j: (i, 0)),
        out_shape=jax.ShapeDtypeStruct((m, d), F32),
        scratch_shapes=[pltpu.VMEM((tm, d), BF16)],
        compiler_params=_cparams("parallel", "arbitrary"),
        name="ffn",
    )(x, mods_l, g, w_gate_p, w_up_p, w_out_p)


def _proj_kernel(x_ref, mod_ref, g_ref, w_ref, wab_ref, o_ref, ab_ref, h_ref):
    @pl.when(pl.program_id(1) == 0)
    def _():
        h = _mod_norm(x_ref[...], g_ref[...], mod_ref[4:5, :], mod_ref[3:4, :]).astype(BF16)
        h_ref[...] = h
        ab_ref[...] = jnp.dot(h, wab_ref[...], preferred_element_type=F32)

    o_ref[...] = jnp.dot(h_ref[...], w_ref[...], preferred_element_type=F32)


def _gdn_proj(x, mods_l, g, w_main, w_ab, row_of_tile, tm):
    m, d = x.shape
    n = w_main.shape[1]
    tn = 1024
    return pl.pallas_call(
        _proj_kernel,
        grid=(m // tm, n // tn),
        in_specs=[
            pl.BlockSpec((tm, d), lambda i, j: (i, 0)),
            pl.BlockSpec((None, N_MOD, d), lambda i, j: (row_of_tile(i), 0, 0)),
            pl.BlockSpec((1, d), lambda i, j: (0, 0)),
            pl.BlockSpec((d, tn), lambda i, j: (0, j)),
            pl.BlockSpec((d, LANES), lambda i, j: (0, 0)),
        ],
        out_specs=[pl.BlockSpec((tm, tn), lambda i, j: (i, j)), pl.BlockSpec((tm, LANES), lambda i, j: (i, 0))],
        out_shape=[jax.ShapeDtypeStruct((m, n), F32), jax.ShapeDtypeStruct((m, LANES), F32)],
        scratch_shapes=[pltpu.VMEM((tm, d), BF16)],
        compiler_params=_cparams("parallel", "arbitrary"),
        name="gdn_proj",
    )(x, mods_l, g, w_main, w_ab)


def _row_sum_lanes(x):
    hi, lo = _split(x)
    ones = jnp.ones((LANES, LANES), BF16)
    return jnp.dot(hi, ones, preferred_element_type=F32) + jnp.dot(lo, ones, preferred_element_type=F32)


def _conv_silu(x_ref, w_ref, o_ref, pad_ref, t):
    zeros = jnp.zeros((SUBLANES, x_ref.shape[1]), F32)
    pad_ref[0:SUBLANES, :] = zeros
    pad_ref[t + SUBLANES:t + 2 * SUBLANES, :] = zeros
    pad_ref[SUBLANES:t + SUBLANES, :] = x_ref[...]
    acc = x_ref[...] * w_ref[CONV_PAD:CONV_PAD + 1, :]
    for j in range(CONV_K):
        if j != CONV_PAD:
            r0 = SUBLANES + j - CONV_PAD
            acc = acc + pad_ref[r0:r0 + t, :] * w_ref[j:j + 1, :]
    o_ref[...] = _silu(acc)


def _chunk_cumsum(g, t, reverse):
    pos = lax.broadcasted_iota(jnp.int32, g.shape, 0) % CHUNK
    s = 1
    while s < CHUNK:
        if reverse:
            g = g + jnp.where(pos < CHUNK - s, pltpu.roll(g, t - s, 0), 0.0)
        else:
            g = g + jnp.where(pos >= s, pltpu.roll(g, s, 0), 0.0)
        s *= 2
    return g


def _gdn_kernel(*refs, t, hb, n_heads, has_s0, n_alias, emit_state):
    (q_ref, k_ref, v_ref, z_ref, ab_ref, cq_ref, ck_ref, cv_ref, alog_ref, dtb_ref, ng_ref), refs = refs[:11], refs[11:]
    if has_s0:
        s0_ref, refs = refs[0], refs[1:]
    refs = refs[n_alias:]
    og_ref, refs = refs[0], refs[1:]
    if emit_state:
        sout_ref, refs = refs[0], refs[1:]
    qn_ref, kn_ref, vc_ref, gam_ref, beta_ref, s_ref, o_ref, pad_ref = refs

    dk = LANES
    n_chunks = t // CHUNK
    hg = pl.program_id(1)

    _conv_silu(q_ref, cq_ref, qn_ref, pad_ref, t)
    _conv_silu(k_ref, ck_ref, kn_ref, pad_ref, t)
    _conv_silu(v_ref, cv_ref, vc_ref, pad_ref, t)
    for hh in range(hb):
        cs = slice(hh * dk, (hh + 1) * dk)
        qh = qn_ref[:, cs]
        kh = kn_ref[:, cs]
        qn_ref[:, cs] = qh * lax.rsqrt(_row_sum_lanes(qh * qh) + L2_EPS) * (dk ** -0.5)
        kn_ref[:, cs] = kh * lax.rsqrt(_row_sum_lanes(kh * kh) + L2_EPS)

    ab = ab_ref[...]
    pre = ab + dtb_ref[...]
    softplus = jnp.maximum(pre, 0.0) + jnp.log1p(jnp.exp(-jnp.abs(pre)))
    g_all = -jnp.exp(alog_ref[...]) * softplus
    beta_all = _sigmoid(ab)
    cum = (_chunk_cumsum(g_all, t, False), _chunk_cumsum(g_all, t, True))
    lane = lax.broadcasted_iota(jnp.int32, ab.shape, 1)
    for d in range(2):
        for hh in range(hb):
            col = d * 2 * n_heads + hg * hb + hh
            gsel = jnp.sum(jnp.where(lane == col, cum[d], 0.0), axis=-1, keepdims=True)
            bsel = jnp.sum(jnp.where(lane == col + n_heads, beta_all, 0.0), axis=-1, keepdims=True)
            gam_ref[d * hb + hh] = jnp.broadcast_to(gsel, ab.shape)
            beta_ref[d * hb + hh] = jnp.broadcast_to(bsel, ab.shape)
            s_ref[d * hb + hh] = s0_ref[d, hh] if has_s0 else jnp.zeros((dk, dk), F32)
    ii = lax.broadcasted_iota(jnp.int32, (CHUNK, LANES), 0)
    lane = lax.broadcasted_iota(jnp.int32, (CHUNK, LANES), 1)
    is_f = lane < CHUNK
    jj = jnp.where(is_f, lane, lane - CHUNK)
    ahead = jnp.where(is_f, ii - jj, jj - ii)
    incl = ahead >= 0
    strict = ahead > 0
    eye = (ii == jj).astype(F32)
    pair_masks = []
    s_blk = 1
    while s_blk < CHUNK:
        pair_masks.append(((ii // (2 * s_blk)) == (jj // (2 * s_blk))) & ((ii // s_blk) != (jj // s_blk)))
        s_blk *= 2

    def block_diag(m):
        zero = jnp.zeros_like(m)
        return jnp.concatenate([jnp.where(is_f, m, zero), jnp.where(is_f, zero, m)], axis=0)

    def diag2(top, bottom):
        return jnp.concatenate([jnp.concatenate([top, jnp.zeros_like(bottom)], axis=1),
                                jnp.concatenate([jnp.zeros_like(top), bottom], axis=1)], axis=0)

    def dot_lhs_hi(a, b):
        m = a.shape[0]
        r = jnp.dot(jnp.concatenate(_split(a), axis=0), b, preferred_element_type=F32)
        return r[:m] + r[m:]

    def dot_rhs_hi(a, b):
        n = b.shape[1]
        r = jnp.dot(a, jnp.concatenate(_split(b), axis=1), preferred_element_type=F32)
        return r[:, :n] + r[:, n:]

    def step(s, carry):
        rows_f = pl.ds(pl.multiple_of(s * CHUNK, CHUNK), CHUNK)
        rows_b = pl.ds(pl.multiple_of((n_chunks - 1 - s) * CHUNK, CHUNK), CHUNK)
        heads = range(hb)
        cs_l = [slice(hh * dk, (hh + 1) * dk) for hh in heads]
        qf = [qn_ref[rows_f, cs] for cs in cs_l]
        kf = [kn_ref[rows_f, cs] for cs in cs_l]
        vf = [vc_ref[rows_f, cs] for cs in cs_l]
        qb = [qn_ref[rows_b, cs] for cs in cs_l]
        kb = [kn_ref[rows_b, cs] for cs in cs_l]
        vb = [vc_ref[rows_b, cs] for cs in cs_l]
        gf = [gam_ref[hh, rows_f, :] for hh in heads]
        gb = [gam_ref[hb + hh, rows_b, :] for hh in heads]
        bf = [beta_ref[hh, rows_f, :] for hh in heads]
        bb = [beta_ref[hb + hh, rows_b, :] for hh in heads]

        a_l, x_l, aqk_l = [], [], []
        for hh in heads:
            gr = jnp.concatenate([gf[hh], gb[hh]], axis=0).T[:CHUNK, :]
            diff = jnp.where(is_f, gf[hh], gb[hh]) - gr
            dec_incl = jnp.where(incl, jnp.exp(jnp.where(incl, diff, 0.0)), 0.0)
            lhs = jnp.concatenate([jnp.concatenate([kf[hh], kb[hh]], axis=1),
                                   jnp.concatenate([qf[hh], qb[hh]], axis=1)], axis=0)
            kq = _dot_nt(lhs, diag2(kf[hh], kb[hh]))
            a = jnp.where(is_f, bf[hh], bb[hh]) * kq[:CHUNK] * jnp.where(strict, dec_incl, 0.0)
            a = a.astype(BF16).astype(F32)
            a_l.append(a)
            aqk_l.append(kq[CHUNK:] * dec_incl)
            x_l.append(eye - jnp.where(pair_masks[0], a, 0.0))
        for pm in pair_masks[1:]:
            t_l = [dot_lhs_hi(x, block_diag(jnp.where(pm, a, 0.0)).astype(BF16)) for x, a in zip(x_l, a_l)]
            x_l = [x - dot_rhs_hi(tx.astype(BF16), block_diag(x)) for x, tx in zip(x_l, t_l)]
        egf = [jnp.exp(g) for g in gf]
        egb = [jnp.exp(g) for g in gb]
        wu_l = []
        for hh in heads:
            r_f = jnp.concatenate([kf[hh] * (bf[hh] * egf[hh]), vf[hh] * bf[hh]], axis=1).astype(BF16)
            r_b = jnp.concatenate([kb[hh] * (bb[hh] * egb[hh]), vb[hh] * bb[hh]], axis=1).astype(BF16)
            wu_l.append(dot_lhs_hi(x_l[hh], diag2(r_f, r_b)))
        stf = [s_ref[hh] for hh in heads]
        stb = [s_ref[hb + hh] for hh in heads]
        wqf = [_dot(jnp.concatenate([wu[:, :dk], q * eg], axis=0), st) for wu, q, eg, st in zip(wu_l, qf, egf, stf)]
        wqb = [_dot(jnp.concatenate([wu[:, 2 * dk:3 * dk], q * eg], axis=0), st)
               for wu, q, eg, st in zip(wu_l, qb, egb, stb)]
        for hh in heads:
            vn_f = wu_l[hh][:, dk:2 * dk] - wqf[hh][:CHUNK]
            vn_b = wu_l[hh][:, 3 * dk:] - wqb[hh][:CHUNK]
            glf = gf[hh][CHUNK - 1:CHUNK, :]
            glb = gb[hh][0:1, :]
            kd = jnp.concatenate([kf[hh] * jnp.exp(glf - gf[hh]), kb[hh] * jnp.exp(glb - gb[hh])], axis=0)
            res = _dot(jnp.concatenate([aqk_l[hh], kd.T], axis=0), diag2(vn_f, vn_b))
            o_ref[0, rows_f, cs_l[hh]] = wqf[hh][CHUNK:] + res[:CHUNK, :dk]
            o_ref[1, rows_b, cs_l[hh]] = wqb[hh][CHUNK:] + res[:CHUNK, dk:]
            s_ref[hh] = stf[hh] * jnp.exp(glf) + res[CHUNK:, :dk]
            s_ref[hb + hh] = stb[hh] * jnp.exp(glb) + res[CHUNK:, dk:]
        return carry

    lax.fori_loop(0, n_chunks, step, 0)

    for hh in range(hb):
        cs = slice(hh * dk, (hh + 1) * dk)
        o = o_ref[0, :, cs] + o_ref[1, :, cs]
        o = o * lax.rsqrt(jnp.mean(o * o, axis=-1, keepdims=True) + RMS_EPS) * ng_ref[...] * _silu(z_ref[:, cs])
        og_ref[:, cs] = o.astype(og_ref.dtype)
    if emit_state:
        for d in range(2):
            for hh in range(hb):
                sout_ref[d, hh] = s_ref[d * hb + hh]


def _gdn_core(proj, ab, conv_w, alog_row, dtb_row, ng_row, s0, og_prev, state_prev, *, row0, n_seq, t, hb,
              n_heads, state_slot):
    dk = LANES
    wb = hb * dk
    n_hg = n_heads // hb
    rb = row0 // t
    has_s0 = s0 is not None
    emit_state = state_slot is not None

    def col_spec(part):
        return pl.BlockSpec((t, wb), lambda b, h: (rb + b, part * n_hg + h))

    def conv_spec(part):
        return pl.BlockSpec((CONV_K, wb), lambda b, h: (0, part * n_hg + h))

    row_spec = pl.BlockSpec((1, LANES), lambda b, h: (0, 0))
    in_specs = [col_spec(0), col_spec(1), col_spec(2), col_spec(3),
                pl.BlockSpec((t, LANES), lambda b, h: (rb + b, 0)),
                conv_spec(0), conv_spec(1), conv_spec(2), row_spec, row_spec, row_spec]
    args = [proj, proj, proj, proj, ab, conv_w, conv_w, conv_w, alog_row, dtb_row, ng_row]
    if has_s0:
        in_specs.append(pl.BlockSpec((None, 2, hb, dk, dk), lambda b, h: (b, 0, h, 0, 0)))
        args.append(s0)
    out_specs = [pl.BlockSpec((t, wb), lambda b, h: (rb + b, h))]
    out_shape = [jax.ShapeDtypeStruct((proj.shape[0], n_heads * dk), BF16)]
    if emit_state:
        mi, n_mix = state_slot
        out_specs.append(pl.BlockSpec((None, None, 2, hb, dk, dk), lambda b, h: (b, mi, 0, h, 0, 0)))
        out_shape.append(jax.ShapeDtypeStruct((n_seq, n_mix, 2, n_heads, dk, dk), F32))
    aliases = {}
    for out_idx, prev in enumerate((og_prev, state_prev)):
        if prev is not None:
            aliases[len(args)] = out_idx
            in_specs.append(pl.BlockSpec(memory_space=pl.ANY))
            args.append(prev)
    return pl.pallas_call(
        functools.partial(_gdn_kernel, t=t, hb=hb, n_heads=n_heads, has_s0=has_s0, n_alias=len(aliases),
                          emit_state=emit_state),
        grid=(n_seq, n_hg),
        in_specs=in_specs,
        out_specs=out_specs,
        out_shape=out_shape,
        input_output_aliases=aliases,
        scratch_shapes=[
            pltpu.VMEM((t, wb), F32), pltpu.VMEM((t, wb), F32), pltpu.VMEM((t, wb), F32),
            pltpu.VMEM((2 * hb, t, LANES), F32), pltpu.VMEM((2 * hb, t, LANES), F32),
            pltpu.VMEM((2 * hb, dk, dk), F32), pltpu.VMEM((2, t, wb), F32),
            pltpu.VMEM((t + 2 * SUBLANES, wb), F32),
        ],
        compiler_params=_cparams("parallel", "parallel"),
        name="gdn_core",
    )(*args)


def _out_kernel(x_ref, mod_ref, a_ref, w_ref, o_ref):
    y = jnp.dot(a_ref[...], w_ref[...], preferred_element_type=F32)
    o_ref[...] = x_ref[...] + mod_ref[5:6, :] * y


def _gdn_out(x, mods_l, og, w_out, row_of_tile, tm):
    m, d = x.shape
    kdim = og.shape[1]
    return pl.pallas_call(
        _out_kernel,
        grid=(m // tm,),
        in_specs=[
            pl.BlockSpec((tm, d), lambda i: (i, 0)),
            pl.BlockSpec((None, N_MOD, d), lambda i: (row_of_tile(i), 0, 0)),
            pl.BlockSpec((tm, kdim), lambda i: (i, 0)),
            pl.BlockSpec((kdim, d), lambda i: (0, 0)),
        ],
        out_specs=pl.BlockSpec((tm, d), lambda i: (i, 0)),
        out_shape=jax.ShapeDtypeStruct((m, d), F32),
        compiler_params=_cparams("parallel"),
        name="gdn_out",
    )(x, mods_l, og, w_out)


def _pool_kernel(xf_ref, xg_ref, mod_ref, g_ref, w_ref, sc_ref, *rest, t):
    o_ref, rstd_ref, diff_ref = rest[-3:]
    gi = pl.program_id(1)

    @pl.when(gi == 0)
    def _():
        xf = xf_ref[...]
        rstd_ref[...] = jnp.broadcast_to(lax.rsqrt(jnp.mean(xf * xf, axis=-1, keepdims=True) + RMS_EPS), rstd_ref.shape)

    xg = xg_ref[...]
    pg = xg.shape[1]
    rstd = rstd_ref[...]
    if pg > LANES:
        rstd = jnp.concatenate([rstd] * (pg // LANES), axis=1)
    h = (xg * rstd[:, :pg] * g_ref[...]) * (1.0 + mod_ref[4:5, :]) + mod_ref[3:4, :]
    row = lax.broadcasted_iota(jnp.int32, h.shape, 0)
    for k, win in enumerate(POOL_WINDOWS):
        @pl.when(gi == k)
        def _(win=win):
            half = win // 2
            acc = h
            for o in range(-half, win - half):
                if o == 0:
                    continue
                hs = pltpu.roll(h, (-o) % t, 0)
                valid = (row + o >= 0) if o < 0 else (row + o < t)
                acc = acc + jnp.where(valid, hs, 0.0)
            cnt = (jnp.minimum(row + (win - half), t) - jnp.maximum(row - half, 0)).astype(F32)
            diff_ref[...] = (acc / cnt - h).astype(BF16)

    y = jnp.dot(diff_ref[...], w_ref[...], preferred_element_type=F32) * sc_ref[...]
    o_ref[...] = xg + mod_ref[5:6, :] * y


def _pool(x, mods_l, g, w, scale, out_prev, *, row0, n_seq, t, mod_row0, mod_per_seq):
    m, d = x.shape
    ng, pg, _ = w.shape
    rb = row0 // t
    mod_idx = (lambda b: mod_row0 + b) if mod_per_seq else (lambda b: mod_row0)
    in_specs = [
        pl.BlockSpec((t, d), lambda b, gi: (rb + b, 0)),
        pl.BlockSpec((t, pg), lambda b, gi: (rb + b, gi)),
        pl.BlockSpec((None, N_MOD, pg), lambda b, gi: (mod_idx(b), 0, gi)),
        pl.BlockSpec((1, pg), lambda b, gi: (0, gi)),
        pl.BlockSpec((None, pg, pg), lambda b, gi: (gi, 0, 0)),
        pl.BlockSpec((1, pg), lambda b, gi: (0, gi)),
    ]
    args = [x, x, mods_l, g, w, scale]
    aliases = {}
    if out_prev is not None:
        aliases[len(args)] = 0
        in_specs.append(pl.BlockSpec(memory_space=pl.ANY))
        args.append(out_prev)
    return pl.pallas_call(
        functools.partial(_pool_kernel, t=t),
        grid=(n_seq, ng),
        in_specs=in_specs,
        out_specs=pl.BlockSpec((t, pg), lambda b, gi: (rb + b, gi)),
        out_shape=jax.ShapeDtypeStruct((m, d), F32),
        input_output_aliases=aliases,
        scratch_shapes=[pltpu.VMEM((t, LANES), F32), pltpu.VMEM((t, pg), BF16)],
        compiler_params=_cparams("parallel", "arbitrary"),
        name="pool",
    )(*args)


def _final_kernel(x_ref, g_ref, o_ref):
    x = x_ref[...]
    o_ref[...] = x * lax.rsqrt(jnp.mean(x * x, axis=-1, keepdims=True) + RMS_EPS) * g_ref[...]


def _final_norm(x, g, tm, row0, rows):
    _, d = x.shape
    rb = row0 // tm
    return pl.pallas_call(
        _final_kernel,
        grid=(rows // tm,),
        in_specs=[pl.BlockSpec((tm, d), lambda i: (rb + i, 0)), pl.BlockSpec((1, d), lambda i: (0, 0))],
        out_specs=pl.BlockSpec((tm, d), lambda i: (i, 0)),
        out_shape=jax.ShapeDtypeStruct((rows, d), F32),
        compiler_params=_cparams("parallel"),
        name="final_norm",
    )(x, g)


def _place_cols(vals, n_heads):
    row = jnp.zeros((LANES,), F32)
    for d in range(2):
        row = lax.dynamic_update_slice(row, vals[d].astype(F32), (d * 2 * n_heads,))
    return row.reshape(1, LANES)


def kernel(x_prompt, x_sample, state_gdn, c, c_ctx, w_mod, b_mod, norm_g, ffn_w_in, ffn_w_out, gdn_w_in, gdn_conv, gdn_a_log, gdn_dt_bias, gdn_norm_g, gdn_w_out, pool_w, pool_scale, final_g):
    batch, seq, d = x_prompt.shape
    dec_batch, dec_seq, _ = x_sample.shape
    depth = w_mod.shape[0]
    n_heads = gdn_a_log.shape[-1]
    dk = LANES
    qk = n_heads * dk
    f = ffn_w_out.shape[2]
    fp = FF_TILE * ((f + FF_TILE - 1) // FF_TILE)
    m_p = batch * seq
    m_s = dec_batch * dec_seq
    assert 1 + dec_batch <= MOD_ROWS and gdn_w_in.shape[2] == 4 * qk + 4 * n_heads and 4 * n_heads <= LANES
    assert m_p % dec_seq == 0
    tm = next(c_ for c_ in (512, 256, 128, 64) if m_p % c_ == 0 and dec_seq % c_ == 0)
    n_pt = m_p // tm

    def row_of_tile(i):
        return jnp.where(i < n_pt, 0, 1 + ((i - n_pt) * tm) // dec_seq)

    c8 = jnp.zeros((MOD_ROWS, d), F32).at[0].set(c_ctx).at[1:1 + dec_batch].set(c)
    mods = _modulation(c8, w_mod, b_mod).reshape(depth, MOD_ROWS, N_MOD, d)

    xs = _add_pos(x_sample, _grid_pos_embed(dec_seq, d, x_sample.dtype))
    x = jnp.concatenate([x_prompt.reshape(m_p, d), xs.reshape(m_s, d)], axis=0)

    pad_f = ((0, 0), (0, 0), (0, 0), (0, fp - f))
    w_gate_p = jnp.pad(ffn_w_in[..., :f], pad_f).astype(BF16)
    w_up_p = jnp.pad(ffn_w_in[..., f:], pad_f).astype(BF16)
    w_out_p = jnp.pad(ffn_w_out, ((0, 0), (0, 0), (0, fp - f), (0, 0))).astype(BF16)

    n_gdn = (depth + 1) // 2
    new_state = None
    for l in range(depth):
        mods_l = mods[l]
        ng = norm_g[l]

        def ffn(x, i, sub):
            return _ffn(x, mods_l, ng[sub:sub + 1], w_gate_p, w_up_p, w_out_p, l, i, row_of_tile, tm, sub)

        x = ffn(x, 0, 0)
        mi = l // 2
        if l % 2 == 0:
            w_in = gdn_w_in[mi]
            w_main = w_in[:, :4 * qk].astype(BF16)
            w_ab = jnp.pad(w_in[:, 4 * qk:], ((0, 0), (0, LANES - 4 * n_heads))).astype(BF16)
            proj, ab = _gdn_proj(x, mods_l, ng[1:2], w_main, w_ab, row_of_tile, tm)
            alog_row = _place_cols(gdn_a_log[mi], n_heads)
            dtb_row = _place_cols(gdn_dt_bias[mi], n_heads)
            ng_row = gdn_norm_g[mi].reshape(1, dk).astype(F32)
            core = functools.partial(_gdn_core, proj, ab, gdn_conv[mi], alog_row, dtb_row, ng_row, n_heads=n_heads)
            og, new_state = core(None, None, new_state, row0=0, n_seq=batch, t=seq, hb=min(16, n_heads),
                                 state_slot=(mi, n_gdn))
            (og,) = core(state_gdn[:, mi], og, None, row0=m_p, n_seq=dec_batch, t=dec_seq, hb=min(4, n_heads),
                         state_slot=None)
            x = _gdn_out(x, mods_l, og, gdn_w_out[mi].astype(BF16), row_of_tile, tm)
        else:
            pw = pool_w[mi].astype(BF16)
            ps = pool_scale[mi].reshape(1, d)
            pool = functools.partial(_pool, x, mods_l, ng[1:2], pw, ps)
            x_new = pool(None, row0=0, n_seq=batch, t=seq, mod_row0=0, mod_per_seq=False)
            x = pool(x_new, row0=m_p, n_seq=dec_batch, t=dec_seq, mod_row0=1, mod_per_seq=True)
        x = ffn(x, 1, 2)

    fg = final_g.reshape(1, d)
    y_prompt = _final_norm(x, fg, tm, 0, m_p).reshape(batch, seq, d)
    y_sample = _final_norm(x, fg, tm, m_p, m_s).reshape(dec_batch, dec_seq, d)
    return (y_prompt, y_sample, new_state.astype(state_gdn.dtype))
```

```python
import functools
import math

import jax
import jax.numpy as jnp
from jax import lax
from jax.experimental import pallas as pl
from jax.experimental.pallas import tpu as pltpu

F32 = jnp.float32
BF16 = jnp.bfloat16

RMS_EPS = 1e-6
L2_EPS = 1e-6
CHUNK = 64
CONV_K = 5
CONV_PAD = CONV_K // 2
POOL_WINDOWS = (2, 4, 8, 16)
GRID_W = 64
POS_BASE = 10000.0
N_MOD = 9
MOD_ROWS = 8
LANES = 128
SUBLANES = 8
FF_TILE = 512
VMEM_LIMIT = 56 * 1024 * 1024


def _cparams(*sem):
    return pltpu.CompilerParams(dimension_semantics=sem, vmem_limit_bytes=VMEM_LIMIT)


def _sigmoid(x):
    return jax.nn.sigmoid(x)


def _silu(x):
    return x * _sigmoid(x)


def _dot(a, b):
    return jnp.dot(a.astype(BF16), b.astype(BF16), preferred_element_type=F32)


def _dot_nt(a, b):
    return lax.dot_general(a.astype(BF16), b.astype(BF16), (((1,), (1,)), ((), ())), preferred_element_type=F32)


def _split(a):
    hi = a.astype(BF16)
    lo = (a - hi.astype(F32)).astype(BF16)
    return hi, lo


def _dot_hi(a, b):
    ah, al = _split(a)
    bh, bl = _split(b)
    d = functools.partial(jnp.dot, preferred_element_type=F32)
    return d(ah, bh) + (d(ah, bl) + d(al, bh))


def _mod_norm(x, g, scale, shift):
    ms = jnp.mean(x * x, axis=-1, keepdims=True)
    return (x * lax.rsqrt(ms + RMS_EPS) * g) * (1.0 + scale) + shift


def _mod_kernel(c_ref, w_ref, b_ref, o_ref):
    s = _silu(c_ref[...])
    o_ref[...] = _dot_hi(s, w_ref[...]) + b_ref[...]


def _modulation(c8, w_mod, b_mod):
    depth, d, nd = w_mod.shape
    tn = next(c_ for c_ in (1024, 512, 256, 128) if nd % c_ == 0)
    return pl.pallas_call(
        _mod_kernel,
        grid=(depth, nd // tn),
        in_specs=[
            pl.BlockSpec((MOD_ROWS, d), lambda l, j: (0, 0)),
            pl.BlockSpec((None, d, tn), lambda l, j: (l, 0, j)),
            pl.BlockSpec((None, 1, tn), lambda l, j: (l, 0, j)),
        ],
        out_specs=pl.BlockSpec((None, MOD_ROWS, tn), lambda l, j: (l, 0, j)),
        out_shape=jax.ShapeDtypeStruct((depth, MOD_ROWS, nd), F32),
        compiler_params=_cparams("parallel", "parallel"),
        name="modulation",
    )(c8, w_mod, b_mod.reshape(depth, 1, nd))


def _add_kernel(x_ref, p_ref, o_ref):
    o_ref[...] = x_ref[...] + p_ref[...]


def _add_pos(xs, pos):
    b, t, d = xs.shape
    return pl.pallas_call(
        _add_kernel,
        grid=(b,),
        in_specs=[pl.BlockSpec((None, t, d), lambda i: (i, 0, 0)), pl.BlockSpec((t, d), lambda i: (0, 0))],
        out_specs=pl.BlockSpec((None, t, d), lambda i: (i, 0, 0)),
        out_shape=jax.ShapeDtypeStruct(xs.shape, xs.dtype),
        compiler_params=_cparams("parallel"),
        name="add_pos",
    )(xs, pos)


def _grid_pos_embed(n_tok, d, dtype):
    idx = jnp.arange(n_tok)
    r = (idx // GRID_W).astype(F32)
    col = (idx % GRID_W).astype(F32)
    n_freq = d // 4
    freqs = jnp.exp(-math.log(POS_BASE) * jnp.arange(n_freq, dtype=F32) / n_freq)
    ar = r[:, None] * freqs
    ac = col[:, None] * freqs
    return jnp.concatenate([jnp.sin(ar), jnp.cos(ar), jnp.sin(ac), jnp.cos(ac)], axis=-1).astype(dtype)


def _prep_in_kernel(w_ref, g_ref, u_ref, *, f):
    pad = jnp.zeros((g_ref.shape[0], g_ref.shape[1] - f), BF16)
    g_ref[:, :f] = w_ref[:, :f].astype(BF16)
    u_ref[:, :f] = w_ref[:, f:].astype(BF16)
    g_ref[:, f:] = pad
    u_ref[:, f:] = pad


def _prep_out_kernel(w_ref, o_ref, *, f):
    row = pl.program_id(2) * FF_TILE + lax.broadcasted_iota(jnp.int32, w_ref.shape, 0)
    o_ref[...] = jnp.where(row < f, w_ref[...], 0.0).astype(BF16)


def _prep_ffn_weights(w_in, w_out, fp):
    depth, two, d, f2 = w_in.shape
    f = f2 // 2
    assert f % LANES == 0
    rows = 256
    spec_w = pl.BlockSpec((None, None, rows, f2), lambda l, k, r: (l, k, r, 0))
    spec_p = pl.BlockSpec((None, None, rows, fp), lambda l, k, r: (l, k, r, 0))
    padded = jax.ShapeDtypeStruct((depth, two, d, fp), BF16)
    w_gate_p, w_up_p = pl.pallas_call(
        functools.partial(_prep_in_kernel, f=f),
        grid=(depth, two, d // rows),
        in_specs=[spec_w],
        out_specs=[spec_p, spec_p],
        out_shape=[padded, padded],
        compiler_params=_cparams("parallel", "parallel", "parallel"),
        name="prep_w_in",
    )(w_in)
    w_out_p = pl.pallas_call(
        functools.partial(_prep_out_kernel, f=f),
        grid=(depth, two, fp // FF_TILE),
        in_specs=[pl.BlockSpec((None, None, FF_TILE, d), lambda l, k, j: (l, k, j, 0))],
        out_specs=pl.BlockSpec((None, None, FF_TILE, d), lambda l, k, j: (l, k, j, 0)),
        out_shape=jax.ShapeDtypeStruct((depth, two, fp, d), BF16),
        compiler_params=_cparams("parallel", "parallel", "parallel"),
        name="prep_w_out",
    )(w_out)
    return w_gate_p, w_up_p, w_out_p


def _ffn_kernel(x_ref, mod_ref, g_ref, wg_ref, wu_ref, wo_ref, o_ref, h_ref, *, sub):
    j = pl.program_id(1)

    @pl.when(j == 0)
    def _():
        h = _mod_norm(x_ref[...], g_ref[...], mod_ref[3 * sub + 1:3 * sub + 2, :], mod_ref[3 * sub:3 * sub + 1, :])
        h_ref[...] = h.astype(BF16)
        o_ref[...] = jnp.zeros_like(o_ref)

    h = h_ref[...]
    gate = jnp.dot(h, wg_ref[...], preferred_element_type=F32)
    up = jnp.dot(h, wu_ref[...], preferred_element_type=F32)
    act = (_silu(gate) * up).astype(BF16)
    o_ref[...] += jnp.dot(act, wo_ref[...], preferred_element_type=F32)

    @pl.when(j == pl.num_programs(1) - 1)
    def _():
        o_ref[...] = x_ref[...] + (0.5 * mod_ref[3 * sub + 2:3 * sub + 3, :]) * o_ref[...]


def _ffn(x, mods_l, g, w_gate_p, w_up_p, w_out_p, l, k, row_of_tile, tm, sub):
    m, d = x.shape
    fp = w_out_p.shape[2]
    nj = fp // FF_TILE
    return pl.pallas_call(
        functools.partial(_ffn_kernel, sub=sub),
        grid=(m // tm, nj),
        in_specs=[
            pl.BlockSpec((tm, d), lambda i, j: (i, 0)),
            pl.BlockSpec((None, N_MOD, d), lambda i, j: (row_of_tile(i), 0, 0)),
            pl.BlockSpec((1, d), lambda i, j: (0, 0)),
            pl.BlockSpec((None, None, d, FF_TILE), lambda i, j: (l, k, 0, j)),
            pl.BlockSpec((None, None, d, FF_TILE), lambda i, j: (l, k, 0, j)),
            pl.BlockSpec((None, None, FF_TILE, d), lambda i, j: (l, k, j, 0)),
        ],
        out_specs=pl.BlockSpec((tm, d), lambda i, j: (i, 0)),
        out_shape=jax.ShapeDtypeStruct((m, d), F32),
        scratch_shapes=[pltpu.VMEM((tm, d), BF16)],
        compiler_params=_cparams("parallel", "arbitrary"),
        name="ffn",
    )(x, mods_l, g, w_gate_p, w_up_p, w_out_p)


def _proj_kernel(x_ref, mod_ref, g_ref, w_ref, wab_ref, o_ref, ab_ref, h_ref):
    @pl.when(pl.program_id(1) == 0)
    def _():
        h = _mod_norm(x_ref[...], g_ref[...], mod_ref[4:5, :], mod_ref[3:4, :]).astype(BF16)
        h_ref[...] = h
        ab_ref[...] = jnp.dot(h, wab_ref[...], preferred_element_type=F32)

    o_ref[...] = jnp.dot(h_ref[...], w_ref[...], preferred_element_type=F32)


def _gdn_proj(x, mods_l, g, w_main, w_ab, row_of_tile, tm):
    m, d = x.shape
    n = w_main.shape[1]
    tn = 1024
    return pl.pallas_call(
        _proj_kernel,
        grid=(m // tm, n // tn),
        in_specs=[
            pl.BlockSpec((tm, d), lambda i, j: (i, 0)),
            pl.BlockSpec((None, N_MOD, d), lambda i, j: (row_of_tile(i), 0, 0)),
            pl.BlockSpec((1, d), lambda i, j: (0, 0)),
            pl.BlockSpec((d, tn), lambda i, j: (0, j)),
            pl.BlockSpec((d, LANES), lambda i, j: (0, 0)),
        ],
        out_specs=[pl.BlockSpec((tm, tn), lambda i, j: (i, j)), pl.BlockSpec((tm, LANES), lambda i, j: (i, 0))],
        out_shape=[jax.ShapeDtypeStruct((m, n), F32), jax.ShapeDtypeStruct((m, LANES), F32)],
        scratch_shapes=[pltpu.VMEM((tm, d), BF16)],
        compiler_params=_cparams("parallel", "arbitrary"),
        name="gdn_proj",
    )(x, mods_l, g, w_main, w_ab)


def _row_sum_lanes(x):
    hi, lo = _split(x)
    ones = jnp.ones((LANES, LANES), BF16)
    return jnp.dot(hi, ones, preferred_element_type=F32) + jnp.dot(lo, ones, preferred_element_type=F32)


def _conv_silu(x_ref, w_ref, o_ref, pad_ref, t):
    zeros = jnp.zeros((SUBLANES, x_ref.shape[1]), F32)
    pad_ref[0:SUBLANES, :] = zeros
    pad_ref[t + SUBLANES:t + 2 * SUBLANES, :] = zeros
    pad_ref[SUBLANES:t + SUBLANES, :] = x_ref[...]
    acc = x_ref[...] * w_ref[CONV_PAD:CONV_PAD + 1, :]
    for j in range(CONV_K):
        if j != CONV_PAD:
            r0 = SUBLANES + j - CONV_PAD
            acc = acc + pad_ref[r0:r0 + t, :] * w_ref[j:j + 1, :]
    o_ref[...] = _silu(acc)


def _chunk_cumsum(g, t, reverse):
    pos = lax.broadcasted_iota(jnp.int32, g.shape, 0) % CHUNK
    s = 1
    while s < CHUNK:
        if reverse:
            g = g + jnp.where(pos < CHUNK - s, pltpu.roll(g, t - s, 0), 0.0)
        else:
            g = g + jnp.where(pos >= s, pltpu.roll(g, s, 0), 0.0)
        s *= 2
    return g


def _gdn_kernel(*refs, t, hb, n_heads, has_s0, n_alias, emit_state):
    (q_ref, k_ref, v_ref, z_ref, ab_ref, cq_ref, ck_ref, cv_ref, alog_ref, dtb_ref, ng_ref), refs = refs[:11], refs[11:]
    if has_s0:
        s0_ref, refs = refs[0], refs[1:]
    refs = refs[n_alias:]
    og_ref, refs = refs[0], refs[1:]
    if emit_state:
        sout_ref, refs = refs[0], refs[1:]
    qn_ref, kn_ref, vc_ref, gam_ref, beta_ref, s_ref, o_ref, pad_ref = refs

    dk = LANES
    n_chunks = t // CHUNK
    hg = pl.program_id(1)

    _conv_silu(q_ref, cq_ref, qn_ref, pad_ref, t)
    _conv_silu(k_ref, ck_ref, kn_ref, pad_ref, t)
    _conv_silu(v_ref, cv_ref, vc_ref, pad_ref, t)
    for hh in range(hb):
        cs = slice(hh * dk, (hh + 1) * dk)
        qh = qn_ref[:, cs]
        kh = kn_ref[:, cs]
        qn_ref[:, cs] = qh * lax.rsqrt(_row_sum_lanes(qh * qh) + L2_EPS) * (dk ** -0.5)
        kn_ref[:, cs] = kh * lax.rsqrt(_row_sum_lanes(kh * kh) + L2_EPS)

    ab = ab_ref[...]
    pre = ab + dtb_ref[...]
    softplus = jnp.maximum(pre, 0.0) + jnp.log1p(jnp.exp(-jnp.abs(pre)))
    g_all = -jnp.exp(alog_ref[...]) * softplus
    beta_all = _sigmoid(ab)
    cum = (_chunk_cumsum(g_all, t, False), _chunk_cumsum(g_all, t, True))
    lane = lax.broadcasted_iota(jnp.int32, ab.shape, 1)
    for d in range(2):
        for hh in range(hb):
            col = d * 2 * n_heads + hg * hb + hh
            gsel = jnp.sum(jnp.where(lane == col, cum[d], 0.0), axis=-1, keepdims=True)
            bsel = jnp.sum(jnp.where(lane == col + n_heads, beta_all, 0.0), axis=-1, keepdims=True)
            gam_ref[d * hb + hh] = jnp.broadcast_to(gsel, ab.shape)
            beta_ref[d * hb + hh] = jnp.broadcast_to(bsel, ab.shape)
            s_ref[d * hb + hh] = s0_ref[d, hh] if has_s0 else jnp.zeros((dk, dk), F32)
    ii = lax.broadcasted_iota(jnp.int32, (CHUNK, LANES), 0)
    lane = lax.broadcasted_iota(jnp.int32, (CHUNK, LANES), 1)
    is_f = lane < CHUNK
    jj = jnp.where(is_f, lane, lane - CHUNK)
    ahead = jnp.where(is_f, ii - jj, jj - ii)
    incl = ahead >= 0
    strict = ahead > 0
    eye = (ii == jj).astype(F32)
    pair_masks = []
    s_blk = 1
    while s_blk < CHUNK:
        pair_masks.append(((ii // (2 * s_blk)) == (jj // (2 * s_blk))) & ((ii // s_blk) != (jj // s_blk)))
        s_blk *= 2

    def block_diag(m):
        zero = jnp.zeros_like(m)
        return jnp.concatenate([jnp.where(is_f, m, zero), jnp.where(is_f, zero, m)], axis=0)

    def diag2(top, bottom):
        return jnp.concatenate([jnp.concatenate([top, jnp.zeros_like(bottom)], axis=1),
                                jnp.concatenate([jnp.zeros_like(top), bottom], axis=1)], axis=0)

    def dot_lhs_hi(a, b):
        m = a.shape[0]
        r = jnp.dot(jnp.concatenate(_split(a), axis=0), b, preferred_element_type=F32)
        return r[:m] + r[m:]

    def dot_rhs_hi(a, b):
        n = b.shape[1]
        r = jnp.dot(a, jnp.concatenate(_split(b), axis=1), preferred_element_type=F32)
        return r[:, :n] + r[:, n:]

    def step(s, carry):
        rows_f = pl.ds(pl.multiple_of(s * CHUNK, CHUNK), CHUNK)
        rows_b = pl.ds(pl.multiple_of((n_chunks - 1 - s) * CHUNK, CHUNK), CHUNK)
        heads = range(hb)
        cs_l = [slice(hh * dk, (hh + 1) * dk) for hh in heads]
        qf = [qn_ref[rows_f, cs] for cs in cs_l]
        kf = [kn_ref[rows_f, cs] for cs in cs_l]
        vf = [vc_ref[rows_f, cs] for cs in cs_l]
        qb = [qn_ref[rows_b, cs] for cs in cs_l]
        kb = [kn_ref[rows_b, cs] for cs in cs_l]
        vb = [vc_ref[rows_b, cs] for cs in cs_l]
        gf = [gam_ref[hh, rows_f, :] for hh in heads]
        gb = [gam_ref[hb + hh, rows_b, :] for hh in heads]
        bf = [beta_ref[hh, rows_f, :] for hh in heads]
        bb = [beta_ref[hb + hh, rows_b, :] for hh in heads]

        a_l, x_l, aqk_l = [], [], []
        for hh in heads:
            gr = jnp.concatenate([gf[hh], gb[hh]], axis=0).T[:CHUNK, :]
            diff = jnp.where(is_f, gf[hh], gb[hh]) - gr
            dec_incl = jnp.where(incl, jnp.exp(jnp.where(incl, diff, 0.0)), 0.0)
            lhs = jnp.concatenate([jnp.concatenate([kf[hh], kb[hh]], axis=1),
                                   jnp.concatenate([qf[hh], qb[hh]], axis=1)], axis=0)
            kq = _dot_nt(lhs, diag2(kf[hh], kb[hh]))
            a = jnp.where(is_f, bf[hh], bb[hh]) * kq[:CHUNK] * jnp.where(strict, dec_incl, 0.0)
            a = a.astype(BF16).astype(F32)
            a_l.append(a)
            aqk_l.append(kq[CHUNK:] * dec_incl)
            x_l.append(eye - jnp.where(pair_masks[0], a, 0.0))
        for pm in pair_masks[1:]:
            t_l = [dot_lhs_hi(x, block_diag(jnp.where(pm, a, 0.0)).astype(BF16)) for x, a in zip(x_l, a_l)]
            x_l = [x - dot_rhs_hi(tx.astype(BF16), block_diag(x)) for x, tx in zip(x_l, t_l)]
        egf = [jnp.exp(g) for g in gf]
        egb = [jnp.exp(g) for g in gb]
        wu_l = []
        for hh in heads:
            r_f = jnp.concatenate([kf[hh] * (bf[hh] * egf[hh]), vf[hh] * bf[hh]], axis=1).astype(BF16)
            r_b = jnp.concatenate([kb[hh] * (bb[hh] * egb[hh]), vb[hh] * bb[hh]], axis=1).astype(BF16)
            wu_l.append(dot_lhs_hi(x_l[hh], diag2(r_f, r_b)))
        stf = [s_ref[hh] for hh in heads]
        stb = [s_ref[hb + hh] for hh in heads]
        wqf = [_dot(jnp.concatenate([wu[:, :dk], q * eg], axis=0), st) for wu, q, eg, st in zip(wu_l, qf, egf, stf)]
        wqb = [_dot(jnp.concatenate([wu[:, 2 * dk:3 * dk], q * eg], axis=0), st)
               for wu, q, eg, st in zip(wu_l, qb, egb, stb)]
        for hh in heads:
            vn_f = wu_l[hh][:, dk:2 * dk] - wqf[hh][:CHUNK]
            vn_b = wu_l[hh][:, 3 * dk:] - wqb[hh][:CHUNK]
            glf = gf[hh][CHUNK - 1:CHUNK, :]
            glb = gb[hh][0:1, :]
            kd = jnp.concatenate([kf[hh] * jnp.exp(glf - gf[hh]), kb[hh] * jnp.exp(glb - gb[hh])], axis=0)
            res = _dot(jnp.concatenate([aqk_l[hh], kd.T], axis=0), diag2(vn_f, vn_b))
            o_ref[0, rows_f, cs_l[hh]] = wqf[hh][CHUNK:] + res[:CHUNK, :dk]
            o_ref[1, rows_b, cs_l[hh]] = wqb[hh][CHUNK:] + res[:CHUNK, dk:]
            s_ref[hh] = stf[hh] * jnp.exp(glf) + res[CHUNK:, :dk]
            s_ref[hb + hh] = stb[hh] * jnp.exp(glb) + res[CHUNK:, dk:]
        return carry

    lax.fori_loop(0, n_chunks, step, 0)

    for hh in range(hb):
        cs = slice(hh * dk, (hh + 1) * dk)
        o = o_ref[0, :, cs] + o_ref[1, :, cs]
        o = o * lax.rsqrt(jnp.mean(o * o, axis=-1, keepdims=True) + RMS_EPS) * ng_ref[...] * _silu(z_ref[:, cs])
        og_ref[:, cs] = o.astype(og_ref.dtype)
    if emit_state:
        for d in range(2):
            for hh in range(hb):
                sout_ref[d, hh] = s_ref[d * hb + hh]


def _gdn_core(proj, ab, conv_w, alog_row, dtb_row, ng_row, s0, og_prev, state_prev, *, row0, n_seq, t, hb,
              n_heads, state_slot):
    dk = LANES
    wb = hb * dk
    n_hg = n_heads // hb
    rb = row0 // t
    has_s0 = s0 is not None
    emit_state = state_slot is not None

    def col_spec(part):
        return pl.BlockSpec((t, wb), lambda b, h: (rb + b, part * n_hg + h))

    def conv_spec(part):
        return pl.BlockSpec((CONV_K, wb), lambda b, h: (0, part * n_hg + h))

    row_spec = pl.BlockSpec((1, LANES), lambda b, h: (0, 0))
    in_specs = [col_spec(0), col_spec(1), col_spec(2), col_spec(3),
                pl.BlockSpec((t, LANES), lambda b, h: (rb + b, 0)),
                conv_spec(0), conv_spec(1), conv_spec(2), row_spec, row_spec, row_spec]
    args = [proj, proj, proj, proj, ab, conv_w, conv_w, conv_w, alog_row, dtb_row, ng_row]
    if has_s0:
        in_specs.append(pl.BlockSpec((None, 2, hb, dk, dk), lambda b, h: (b, 0, h, 0, 0)))
        args.append(s0)
    out_specs = [pl.BlockSpec((t, wb), lambda b, h: (rb + b, h))]
    out_shape = [jax.ShapeDtypeStruct((proj.shape[0], n_heads * dk), BF16)]
    if emit_state:
        mi, n_mix = state_slot
        out_specs.append(pl.BlockSpec((None, None, 2, hb, dk, dk), lambda b, h: (b, mi, 0, h, 0, 0)))
        out_shape.append(jax.ShapeDtypeStruct((n_seq, n_mix, 2, n_heads, dk, dk), F32))
    aliases = {}
    for out_idx, prev in enumerate((og_prev, state_prev)):
        if prev is not None:
            aliases[len(args)] = out_idx
            in_specs.append(pl.BlockSpec(memory_space=pl.ANY))
            args.append(prev)
    return pl.pallas_call(
        functools.partial(_gdn_kernel, t=t, hb=hb, n_heads=n_heads, has_s0=has_s0, n_alias=len(aliases),
                          emit_state=emit_state),
        grid=(n_seq, n_hg),
        in_specs=in_specs,
        out_specs=out_specs,
        out_shape=out_shape,
        input_output_aliases=aliases,
        scratch_shapes=[
            pltpu.VMEM((t, wb), F32), pltpu.VMEM((t, wb), F32), pltpu.VMEM((t, wb), F32),
            pltpu.VMEM((2 * hb, t, LANES), F32), pltpu.VMEM((2 * hb, t, LANES), F32),
            pltpu.VMEM((2 * hb, dk, dk), F32), pltpu.VMEM((2, t, wb), F32),
            pltpu.VMEM((t + 2 * SUBLANES, wb), F32),
        ],
        compiler_params=_cparams("parallel", "parallel"),
        name="gdn_core",
    )(*args)


def _out_kernel(x_ref, mod_ref, a_ref, w_ref, o_ref):
    y = jnp.dot(a_ref[...], w_ref[...], preferred_element_type=F32)
    o_ref[...] = x_ref[...] + mod_ref[5:6, :] * y


def _gdn_out(x, mods_l, og, w_out, row_of_tile, tm):
    m, d = x.shape
    kdim = og.shape[1]
    return pl.pallas_call(
        _out_kernel,
        grid=(m // tm,),
        in_specs=[
            pl.BlockSpec((tm, d), lambda i: (i, 0)),
            pl.BlockSpec((None, N_MOD, d), lambda i: (row_of_tile(i), 0, 0)),
            pl.BlockSpec((tm, kdim), lambda i: (i, 0)),
            pl.BlockSpec((kdim, d), lambda i: (0, 0)),
        ],
        out_specs=pl.BlockSpec((tm, d), lambda i: (i, 0)),
        out_shape=jax.ShapeDtypeStruct((m, d), F32),
        compiler_params=_cparams("parallel"),
        name="gdn_out",
    )(x, mods_l, og, w_out)


def _pool_kernel(xf_ref, xg_ref, mod_ref, g_ref, w_ref, sc_ref, *rest, t):
    o_ref, rstd_ref, diff_ref = rest[-3:]
    gi = pl.program_id(1)

    @pl.when(gi == 0)
    def _():
        xf = xf_ref[...]
        rstd_ref[...] = jnp.broadcast_to(lax.rsqrt(jnp.mean(xf * xf, axis=-1, keepdims=True) + RMS_EPS), rstd_ref.shape)

    xg = xg_ref[...]
    pg = xg.shape[1]
    rstd = rstd_ref[...]
    if pg > LANES:
        rstd = jnp.concatenate([rstd] * (pg // LANES), axis=1)
    h = (xg * rstd[:, :pg] * g_ref[...]) * (1.0 + mod_ref[4:5, :]) + mod_ref[3:4, :]
    row = lax.broadcasted_iota(jnp.int32, h.shape, 0)
    edge = max(POOL_WINDOWS)
    zeros = jnp.zeros((edge, pg), F32)
    hp = jnp.concatenate([zeros, h, zeros], axis=0)
    for k, win in enumerate(POOL_WINDOWS):
        @pl.when(gi == k)
        def _(win=win):
            half = win // 2
            n = t + 2 * edge
            acc = hp
            s = 1
            while s < win:
                acc = acc + pltpu.roll(acc, n - s, 0)
                s *= 2
            acc = acc[edge - half:edge - half + t]
            cnt = (jnp.minimum(row + (win - half), t) - jnp.maximum(row - half, 0)).astype(F32)
            diff_ref[...] = (acc / cnt - h).astype(BF16)

    y = jnp.dot(diff_ref[...], w_ref[...], preferred_element_type=F32) * sc_ref[...]
    o_ref[...] = xg + mod_ref[5:6, :] * y


def _pool(x, mods_l, g, w, scale, out_prev, *, row0, n_seq, t, mod_row0, mod_per_seq):
    m, d = x.shape
    ng, pg, _ = w.shape
    rb = row0 // t
    mod_idx = (lambda b: mod_row0 + b) if mod_per_seq else (lambda b: mod_row0)
    in_specs = [
        pl.BlockSpec((t, d), lambda b, gi: (rb + b, 0)),
        pl.BlockSpec((t, pg), lambda b, gi: (rb + b, gi)),
        pl.BlockSpec((None, N_MOD, pg), lambda b, gi: (mod_idx(b), 0, gi)),
        pl.BlockSpec((1, pg), lambda b, gi: (0, gi)),
        pl.BlockSpec((None, pg, pg), lambda b, gi: (gi, 0, 0)),
        pl.BlockSpec((1, pg), lambda b, gi: (0, gi)),
    ]
    args = [x, x, mods_l, g, w, scale]
    aliases = {}
    if out_prev is not None:
        aliases[len(args)] = 0
        in_specs.append(pl.BlockSpec(memory_space=pl.ANY))
        args.append(out_prev)
    return pl.pallas_call(
        functools.partial(_pool_kernel, t=t),
        grid=(n_seq, ng),
        in_specs=in_specs,
        out_specs=pl.BlockSpec((t, pg), lambda b, gi: (rb + b, gi)),
        out_shape=jax.ShapeDtypeStruct((m, d), F32),
        input_output_aliases=aliases,
        scratch_shapes=[pltpu.VMEM((t, LANES), F32), pltpu.VMEM((t, pg), BF16)],
        compiler_params=_cparams("parallel", "arbitrary"),
        name="pool",
    )(*args)


def _final_kernel(x_ref, g_ref, o_ref):
    x = x_ref[...]
    o_ref[...] = x * lax.rsqrt(jnp.mean(x * x, axis=-1, keepdims=True) + RMS_EPS) * g_ref[...]


def _final_norm(x, g, tm, row0, rows):
    _, d = x.shape
    rb = row0 // tm
    return pl.pallas_call(
        _final_kernel,
        grid=(rows // tm,),
        in_specs=[pl.BlockSpec((tm, d), lambda i: (rb + i, 0)), pl.BlockSpec((1, d), lambda i: (0, 0))],
        out_specs=pl.BlockSpec((tm, d), lambda i: (i, 0)),
        out_shape=jax.ShapeDtypeStruct((rows, d), F32),
        compiler_params=_cparams("parallel"),
        name="final_norm",
    )(x, g)


def _place_cols(vals, n_heads):
    row = jnp.zeros((LANES,), F32)
    for d in range(2):
        row = lax.dynamic_update_slice(row, vals[d].astype(F32), (d * 2 * n_heads,))
    return row.reshape(1, LANES)


def kernel(x_prompt, x_sample, state_gdn, c, c_ctx, w_mod, b_mod, norm_g, ffn_w_in, ffn_w_out, gdn_w_in, gdn_conv, gdn_a_log, gdn_dt_bias, gdn_norm_g, gdn_w_out, pool_w, pool_scale, final_g):
    batch, seq, d = x_prompt.shape
    dec_batch, dec_seq, _ = x_sample.shape
    depth = w_mod.shape[0]
    n_heads = gdn_a_log.shape[-1]
    dk = LANES
    qk = n_heads * dk
    f = ffn_w_out.shape[2]
    fp = FF_TILE * ((f + FF_TILE - 1) // FF_TILE)
    m_p = batch * seq
    m_s = dec_batch * dec_seq
    assert 1 + dec_batch <= MOD_ROWS and gdn_w_in.shape[2] == 4 * qk + 4 * n_heads and 4 * n_heads <= LANES
    assert m_p % dec_seq == 0
    tm = next(c_ for c_ in (512, 256, 128, 64) if m_p % c_ == 0 and dec_seq % c_ == 0)
    n_pt = m_p // tm

    def row_of_tile(i):
        return jnp.where(i < n_pt, 0, 1 + ((i - n_pt) * tm) // dec_seq)

    c8 = jnp.zeros((MOD_ROWS, d), F32).at[0].set(c_ctx).at[1:1 + dec_batch].set(c)
    mods = _modulation(c8, w_mod, b_mod).reshape(depth, MOD_ROWS, N_MOD, d)

    xs = _add_pos(x_sample, _grid_pos_embed(dec_seq, d, x_sample.dtype))
    x = jnp.concatenate([x_prompt.reshape(m_p, d), xs.reshape(m_s, d)], axis=0)

    w_gate_p, w_up_p, w_out_p = _prep_ffn_weights(ffn_w_in, ffn_w_out, fp)

    n_gdn = (depth + 1) // 2
    new_state = None
    for l in range(depth):
        mods_l = mods[l]
        ng = norm_g[l]

        def ffn(x, i, sub):
            return _ffn(x, mods_l, ng[sub:sub + 1], w_gate_p, w_up_p, w_out_p, l, i, row_of_tile, tm, sub)

        x = ffn(x, 0, 0)
        mi = l // 2
        if l % 2 == 0:
            w_in = gdn_w_in[mi]
            w_main = w_in[:, :4 * qk].astype(BF16)
            w_ab = jnp.pad(w_in[:, 4 * qk:], ((0, 0), (0, LANES - 4 * n_heads))).astype(BF16)
            proj, ab = _gdn_proj(x, mods_l, ng[1:2], w_main, w_ab, row_of_tile, tm)
            alog_row = _place_cols(gdn_a_log[mi], n_heads)
            dtb_row = _place_cols(gdn_dt_bias[mi], n_heads)
            ng_row = gdn_norm_g[mi].reshape(1, dk).astype(F32)
            core = functools.partial(_gdn_core, proj, ab, gdn_conv[mi], alog_row, dtb_row, ng_row, n_heads=n_heads)
            og, new_state = core(None, None, new_state, row0=0, n_seq=batch, t=seq, hb=min(16, n_heads),
                                 state_slot=(mi, n_gdn))
            (og,) = core(state_gdn[:, mi], og, None, row0=m_p, n_seq=dec_batch, t=dec_seq, hb=min(4, n_heads),
                         state_slot=None)
            x = _gdn_out(x, mods_l, og, gdn_w_out[mi].astype(BF16), row_of_tile, tm)
        else:
            pw = pool_w[mi].astype(BF16)
            ps = pool_scale[mi].reshape(1, d)
            pool = functools.partial(_pool, x, mods_l, ng[1:2], pw, ps)
            x_new = pool(None, row0=0, n_seq=batch, t=seq, mod_row0=0, mod_per_seq=False)
            x = pool(x_new, row0=m_p, n_seq=dec_batch, t=dec_seq, mod_row0=1, mod_per_seq=True)
        x = ffn(x, 1, 2)

    fg = final_g.reshape(1, d)
    y_prompt = _final_norm(x, fg, tm, 0, m_p).reshape(batch, seq, d)
    y_sample = _final_norm(x, fg, tm, m_p, m_s).reshape(dec_batch, dec_seq, d)
    return (y_prompt, y_sample, new_state.astype(state_gdn.dtype))
```

```python
import functools
import math

import jax
import jax.numpy as jnp
from jax import lax
from jax.experimental import pallas as pl
from jax.experimental.pallas import tpu as pltpu

F32 = jnp.float32
BF16 = jnp.bfloat16

RMS_EPS = 1e-6
L2_EPS = 1e-6
CHUNK = 64
CONV_K = 5
CONV_PAD = CONV_K // 2
POOL_WINDOWS = (2, 4, 8, 16)
GRID_W = 64
POS_BASE = 10000.0
N_MOD = 9
MOD_ROWS = 8
LANES = 128
SUBLANES = 8
FF_TILE = 512
VMEM_LIMIT = 56 * 1024 * 1024


def _cparams(*sem):
    return pltpu.CompilerParams(dimension_semantics=sem, vmem_limit_bytes=VMEM_LIMIT)


def _sigmoid(x):
    return jax.nn.sigmoid(x)


def _silu(x):
    return x * _sigmoid(x)


def _dot(a, b):
    return jnp.dot(a.astype(BF16), b.astype(BF16), preferred_element_type=F32)


def _dot_nt(a, b):
    return lax.dot_general(a.astype(BF16), b.astype(BF16), (((1,), (1,)), ((), ())), preferred_element_type=F32)


def _split(a):
    hi = a.astype(BF16)
    lo = (a - hi.astype(F32)).astype(BF16)
    return hi, lo


def _dot_hi(a, b):
    ah, al = _split(a)
    bh, bl = _split(b)
    d = functools.partial(jnp.dot, preferred_element_type=F32)
    return d(ah, bh) + (d(ah, bl) + d(al, bh))


def _mod_norm(x, g, scale, shift):
    ms = jnp.mean(x * x, axis=-1, keepdims=True)
    return (x * lax.rsqrt(ms + RMS_EPS) * g) * (1.0 + scale) + shift


def _mod_kernel(c_ref, w_ref, b_ref, o_ref):
    s = _silu(c_ref[...])
    o_ref[...] = _dot_hi(s, w_ref[...]) + b_ref[...]


def _modulation(c8, w_mod, b_mod):
    depth, d, nd = w_mod.shape
    tn = next(c_ for c_ in (1024, 512, 256, 128) if nd % c_ == 0)
    return pl.pallas_call(
        _mod_kernel,
        grid=(depth, nd // tn),
        in_specs=[
            pl.BlockSpec((MOD_ROWS, d), lambda l, j: (0, 0)),
            pl.BlockSpec((None, d, tn), lambda l, j: (l, 0, j)),
            pl.BlockSpec((None, 1, tn), lambda l, j: (l, 0, j)),
        ],
        out_specs=pl.BlockSpec((None, MOD_ROWS, tn), lambda l, j: (l, 0, j)),
        out_shape=jax.ShapeDtypeStruct((depth, MOD_ROWS, nd), F32),
        compiler_params=_cparams("parallel", "parallel"),
        name="modulation",
    )(c8, w_mod, b_mod.reshape(depth, 1, nd))


def _add_kernel(x_ref, p_ref, o_ref):
    o_ref[...] = x_ref[...] + p_ref[...]


def _add_pos(xs, pos):
    b, t, d = xs.shape
    return pl.pallas_call(
        _add_kernel,
        grid=(b,),
        in_specs=[pl.BlockSpec((None, t, d), lambda i: (i, 0, 0)), pl.BlockSpec((t, d), lambda i: (0, 0))],
        out_specs=pl.BlockSpec((None, t, d), lambda i: (i, 0, 0)),
        out_shape=jax.ShapeDtypeStruct(xs.shape, xs.dtype),
        compiler_params=_cparams("parallel"),
        name="add_pos",
    )(xs, pos)


def _grid_pos_embed(n_tok, d, dtype):
    idx = jnp.arange(n_tok)
    r = (idx // GRID_W).astype(F32)
    col = (idx % GRID_W).astype(F32)
    n_freq = d // 4
    freqs = jnp.exp(-math.log(POS_BASE) * jnp.arange(n_freq, dtype=F32) / n_freq)
    ar = r[:, None] * freqs
    ac = col[:, None] * freqs
    return jnp.concatenate([jnp.sin(ar), jnp.cos(ar), jnp.sin(ac), jnp.cos(ac)], axis=-1).astype(dtype)


def _prep_in_kernel(w_ref, g_ref, u_ref, *, f):
    pad = jnp.zeros((g_ref.shape[0], g_ref.shape[1] - f), BF16)
    g_ref[:, :f] = w_ref[:, :f].astype(BF16)
    u_ref[:, :f] = w_ref[:, f:].astype(BF16)
    g_ref[:, f:] = pad
    u_ref[:, f:] = pad


def _prep_out_kernel(w_ref, o_ref, *, f):
    row = pl.program_id(2) * FF_TILE + lax.broadcasted_iota(jnp.int32, w_ref.shape, 0)
    o_ref[...] = jnp.where(row < f, w_ref[...], 0.0).astype(BF16)


def _prep_ffn_weights(w_in, w_out, fp):
    depth, two, d, f2 = w_in.shape
    f = f2 // 2
    assert f % LANES == 0
    rows = 256
    spec_w = pl.BlockSpec((None, None, rows, f2), lambda l, k, r: (l, k, r, 0))
    spec_p = pl.BlockSpec((None, None, rows, fp), lambda l, k, r: (l, k, r, 0))
    padded = jax.ShapeDtypeStruct((depth, two, d, fp), BF16)
    w_gate_p, w_up_p = pl.pallas_call(
        functools.partial(_prep_in_kernel, f=f),
        grid=(depth, two, d // rows),
        in_specs=[spec_w],
        out_specs=[spec_p, spec_p],
        out_shape=[padded, padded],
        compiler_params=_cparams("parallel", "parallel", "parallel"),
        name="prep_w_in",
    )(w_in)
    w_out_p = pl.pallas_call(
        functools.partial(_prep_out_kernel, f=f),
        grid=(depth, two, fp // FF_TILE),
        in_specs=[pl.BlockSpec((None, None, FF_TILE, d), lambda l, k, j: (l, k, j, 0))],
        out_specs=pl.BlockSpec((None, None, FF_TILE, d), lambda l, k, j: (l, k, j, 0)),
        out_shape=jax.ShapeDtypeStruct((depth, two, fp, d), BF16),
        compiler_params=_cparams("parallel", "parallel", "parallel"),
        name="prep_w_out",
    )(w_out)
    return w_gate_p, w_up_p, w_out_p


def _ffn_kernel(x_ref, mod_ref, g_ref, wg_ref, wu_ref, wo_ref, o_ref, h_ref, *, sub):
    j = pl.program_id(1)

    @pl.when(j == 0)
    def _():
        h = _mod_norm(x_ref[...], g_ref[...], mod_ref[3 * sub + 1:3 * sub + 2, :], mod_ref[3 * sub:3 * sub + 1, :])
        h_ref[...] = h.astype(BF16)
        o_ref[...] = jnp.zeros_like(o_ref)

    h = h_ref[...]
    gate = jnp.dot(h, wg_ref[...], preferred_element_type=F32)
    up = jnp.dot(h, wu_ref[...], preferred_element_type=F32)
    act = (_silu(gate) * up).astype(BF16)
    o_ref[...] += jnp.dot(act, wo_ref[...], preferred_element_type=F32)

    @pl.when(j == pl.num_programs(1) - 1)
    def _():
        o_ref[...] = x_ref[...] + (0.5 * mod_ref[3 * sub + 2:3 * sub + 3, :]) * o_ref[...]


def _ffn(x, mods_l, g, w_gate_p, w_up_p, w_out_p, l, k, row_of_tile, tm, sub):
    m, d = x.shape
    fp = w_out_p.shape[2]
    nj = fp // FF_TILE
    return pl.pallas_call(
        functools.partial(_ffn_kernel, sub=sub),
        grid=(m // tm, nj),
        in_specs=[
            pl.BlockSpec((tm, d), lambda i, j: (i, 0)),
            pl.BlockSpec((None, N_MOD, d), lambda i, j: (row_of_tile(i), 0, 0)),
            pl.BlockSpec((1, d), lambda i, j: (0, 0)),
            pl.BlockSpec((None, None, d, FF_TILE), lambda i, j: (l, k, 0, j)),
            pl.BlockSpec((None, None, d, FF_TILE), lambda i, j: (l, k, 0, j)),
            pl.BlockSpec((None, None, FF_TILE, d), lambda i, j: (l, k, j, 0)),
        ],
        out_specs=pl.BlockSpec((tm, d), lambda i, j: (i, 0)),
        out_shape=jax.ShapeDtypeStruct((m, d), F32),
        scratch_shapes=[pltpu.VMEM((tm, d), BF16)],
        compiler_params=_cparams("parallel", "arbitrary"),
        name="ffn",
    )(x, mods_l, g, w_gate_p, w_up_p, w_out_p)


def _proj_kernel(x_ref, mod_ref, g_ref, w_ref, wab_ref, o_ref, ab_ref, h_ref):
    @pl.when(pl.program_id(1) == 0)
    def _():
        h = _mod_norm(x_ref[...], g_ref[...], mod_ref[4:5, :], mod_ref[3:4, :]).astype(BF16)
        h_ref[...] = h
        ab_ref[...] = jnp.dot(h, wab_ref[...], preferred_element_type=F32)

    o_ref[...] = jnp.dot(h_ref[...], w_ref[...], preferred_element_type=F32)


def _gdn_proj(x, mods_l, g, w_main, w_ab, row_of_tile, tm):
    m, d = x.shape
    n = w_main.shape[1]
    tn = 1024
    return pl.pallas_call(
        _proj_kernel,
        grid=(m // tm, n // tn),
        in_specs=[
            pl.BlockSpec((tm, d), lambda i, j: (i, 0)),
            pl.BlockSpec((None, N_MOD, d), lambda i, j: (row_of_tile(i), 0, 0)),
            pl.BlockSpec((1, d), lambda i, j: (0, 0)),
            pl.BlockSpec((d, tn), lambda i, j: (0, j)),
            pl.BlockSpec((d, LANES), lambda i, j: (0, 0)),
        ],
        out_specs=[pl.BlockSpec((tm, tn), lambda i, j: (i, j)), pl.BlockSpec((tm, LANES), lambda i, j: (i, 0))],
        out_shape=[jax.ShapeDtypeStruct((m, n), F32), jax.ShapeDtypeStruct((m, LANES), F32)],
        scratch_shapes=[pltpu.VMEM((tm, d), BF16)],
        compiler_params=_cparams("parallel", "arbitrary"),
        name="gdn_proj",
    )(x, mods_l, g, w_main, w_ab)


def _row_sum_lanes(x):
    hi, lo = _split(x)
    ones = jnp.ones((LANES, LANES), BF16)
    return jnp.dot(hi, ones, preferred_element_type=F32) + jnp.dot(lo, ones, preferred_element_type=F32)


def _conv_silu(x_ref, w_ref, o_ref, pad_ref, t):
    zeros = jnp.zeros((SUBLANES, x_ref.shape[1]), F32)
    pad_ref[0:SUBLANES, :] = zeros
    pad_ref[t + SUBLANES:t + 2 * SUBLANES, :] = zeros
    pad_ref[SUBLANES:t + SUBLANES, :] = x_ref[...]
    acc = x_ref[...] * w_ref[CONV_PAD:CONV_PAD + 1, :]
    for j in range(CONV_K):
        if j != CONV_PAD:
            r0 = SUBLANES + j - CONV_PAD
            acc = acc + pad_ref[r0:r0 + t, :] * w_ref[j:j + 1, :]
    o_ref[...] = _silu(acc)


def _chunk_cumsum(g, t, reverse):
    pos = lax.broadcasted_iota(jnp.int32, g.shape, 0) % CHUNK
    s = 1
    while s < CHUNK:
        if reverse:
            g = g + jnp.where(pos < CHUNK - s, pltpu.roll(g, t - s, 0), 0.0)
        else:
            g = g + jnp.where(pos >= s, pltpu.roll(g, s, 0), 0.0)
        s *= 2
    return g


def _gdn_kernel(*refs, t, hb, n_heads, has_s0, n_alias, emit_state):
    (q_ref, k_ref, v_ref, z_ref, ab_ref, cq_ref, ck_ref, cv_ref, alog_ref, dtb_ref, ng_ref), refs = refs[:11], refs[11:]
    if has_s0:
        s0_ref, refs = refs[0], refs[1:]
    refs = refs[n_alias:]
    og_ref, refs = refs[0], refs[1:]
    if emit_state:
        sout_ref, refs = refs[0], refs[1:]
    qn_ref, kn_ref, vc_ref, gam_ref, beta_ref, s_ref, o_ref, pad_ref = refs

    dk = LANES
    n_chunks = t // CHUNK
    hg = pl.program_id(1)

    _conv_silu(q_ref, cq_ref, qn_ref, pad_ref, t)
    _conv_silu(k_ref, ck_ref, kn_ref, pad_ref, t)
    _conv_silu(v_ref, cv_ref, vc_ref, pad_ref, t)
    for hh in range(hb):
        cs = slice(hh * dk, (hh + 1) * dk)
        qh = qn_ref[:, cs]
        kh = kn_ref[:, cs]
        qn_ref[:, cs] = qh * lax.rsqrt(_row_sum_lanes(qh * qh) + L2_EPS) * (dk ** -0.5)
        kn_ref[:, cs] = kh * lax.rsqrt(_row_sum_lanes(kh * kh) + L2_EPS)

    ab = ab_ref[...]
    pre = ab + dtb_ref[...]
    softplus = jnp.maximum(pre, 0.0) + jnp.log1p(jnp.exp(-jnp.abs(pre)))
    g_all = -jnp.exp(alog_ref[...]) * softplus
    beta_all = _sigmoid(ab)
    cum = (_chunk_cumsum(g_all, t, False), _chunk_cumsum(g_all, t, True))
    lane = lax.broadcasted_iota(jnp.int32, ab.shape, 1)
    for d in range(2):
        for hh in range(hb):
            col = d * 2 * n_heads + hg * hb + hh
            gsel = jnp.sum(jnp.where(lane == col, cum[d], 0.0), axis=-1, keepdims=True)
            bsel = jnp.sum(jnp.where(lane == col + n_heads, beta_all, 0.0), axis=-1, keepdims=True)
            gam_ref[d * hb + hh] = jnp.broadcast_to(gsel, ab.shape)
            beta_ref[d * hb + hh] = jnp.broadcast_to(bsel, ab.shape)
            s_ref[d * hb + hh] = s0_ref[d, hh] if has_s0 else jnp.zeros((dk, dk), F32)
    ii = lax.broadcasted_iota(jnp.int32, (CHUNK, LANES), 0)
    lane = lax.broadcasted_iota(jnp.int32, (CHUNK, LANES), 1)
    is_f = lane < CHUNK
    jj = jnp.where(is_f, lane, lane - CHUNK)
    ahead = jnp.where(is_f, ii - jj, jj - ii)
    incl = ahead >= 0
    strict = ahead > 0
    eye = (ii == jj).astype(F32)
    pair_masks = []
    s_blk = 1
    while s_blk < CHUNK:
        pair_masks.append(((ii // (2 * s_blk)) == (jj // (2 * s_blk))) & ((ii // s_blk) != (jj // s_blk)))
        s_blk *= 2

    def block_diag(m):
        zero = jnp.zeros_like(m)
        return jnp.concatenate([jnp.where(is_f, m, zero), jnp.where(is_f, zero, m)], axis=0)

    def diag2(top, bottom):
        return jnp.concatenate([jnp.concatenate([top, jnp.zeros_like(bottom)], axis=1),
                                jnp.concatenate([jnp.zeros_like(top), bottom], axis=1)], axis=0)

    def dot_lhs_hi(a, b):
        m = a.shape[0]
        r = jnp.dot(jnp.concatenate(_split(a), axis=0), b, preferred_element_type=F32)
        return r[:m] + r[m:]

    def dot_rhs_hi(a, b):
        n = b.shape[1]
        r = jnp.dot(a, jnp.concatenate(_split(b), axis=1), preferred_element_type=F32)
        return r[:, :n] + r[:, n:]

    def step(s, carry):
        rows_f = pl.ds(pl.multiple_of(s * CHUNK, CHUNK), CHUNK)
        rows_b = pl.ds(pl.multiple_of((n_chunks - 1 - s) * CHUNK, CHUNK), CHUNK)
        heads = range(hb)
        cs_l = [slice(hh * dk, (hh + 1) * dk) for hh in heads]
        qf = [qn_ref[rows_f, cs] for cs in cs_l]
        kf = [kn_ref[rows_f, cs] for cs in cs_l]
        vf = [vc_ref[rows_f, cs] for cs in cs_l]
        qb = [qn_ref[rows_b, cs] for cs in cs_l]
        kb = [kn_ref[rows_b, cs] for cs in cs_l]
        vb = [vc_ref[rows_b, cs] for cs in cs_l]
        gf = [gam_ref[hh, rows_f, :] for hh in heads]
        gb = [gam_ref[hb + hh, rows_b, :] for hh in heads]
        bf = [beta_ref[hh, rows_f, :] for hh in heads]
        bb = [beta_ref[hb + hh, rows_b, :] for hh in heads]

        a_l, x_l, aqk_l = [], [], []
        for hh in heads:
            gr = jnp.concatenate([gf[hh], gb[hh]], axis=0).T[:CHUNK, :]
            diff = jnp.where(is_f, gf[hh], gb[hh]) - gr
            dec_incl = jnp.where(incl, jnp.exp(jnp.where(incl, diff, 0.0)), 0.0)
            lhs = jnp.concatenate([jnp.concatenate([kf[hh], kb[hh]], axis=1),
                                   jnp.concatenate([qf[hh], qb[hh]], axis=1)], axis=0)
            kq = _dot_nt(lhs, diag2(kf[hh], kb[hh]))
            a = jnp.where(is_f, bf[hh], bb[hh]) * kq[:CHUNK] * jnp.where(strict, dec_incl, 0.0)
            a = a.astype(BF16).astype(F32)
            a_l.append(a)
            aqk_l.append(kq[CHUNK:] * dec_incl)
            x_l.append(eye - jnp.where(pair_masks[0], a, 0.0))
        for pm in pair_masks[1:]:
            t_l = [dot_lhs_hi(x, block_diag(jnp.where(pm, a, 0.0)).astype(BF16)) for x, a in zip(x_l, a_l)]
            x_l = [x - dot_rhs_hi(tx.astype(BF16), block_diag(x)) for x, tx in zip(x_l, t_l)]
        egf = [jnp.exp(g) for g in gf]
        egb = [jnp.exp(g) for g in gb]
        wu_l = []
        for hh in heads:
            r_f = jnp.concatenate([kf[hh] * (bf[hh] * egf[hh]), vf[hh] * bf[hh]], axis=1).astype(BF16)
            r_b = jnp.concatenate([kb[hh] * (bb[hh] * egb[hh]), vb[hh] * bb[hh]], axis=1).astype(BF16)
            wu_l.append(dot_lhs_hi(x_l[hh], diag2(r_f, r_b)))
        stf = [s_ref[hh] for hh in heads]
        stb = [s_ref[hb + hh] for hh in heads]
        wqf = [_dot(jnp.concatenate([wu[:, :dk], q * eg], axis=0), st) for wu, q, eg, st in zip(wu_l, qf, egf, stf)]
        wqb = [_dot(jnp.concatenate([wu[:, 2 * dk:3 * dk], q * eg], axis=0), st)
               for wu, q, eg, st in zip(wu_l, qb, egb, stb)]
        for hh in heads:
            vn_f = wu_l[hh][:, dk:2 * dk] - wqf[hh][:CHUNK]
            vn_b = wu_l[hh][:, 3 * dk:] - wqb[hh][:CHUNK]
            glf = gf[hh][CHUNK - 1:CHUNK, :]
            glb = gb[hh][0:1, :]
            kd = jnp.concatenate([kf[hh] * jnp.exp(glf - gf[hh]), kb[hh] * jnp.exp(glb - gb[hh])], axis=0)
            res = _dot(jnp.concatenate([aqk_l[hh], kd.T], axis=0), diag2(vn_f, vn_b))
            o_ref[0, rows_f, cs_l[hh]] = wqf[hh][CHUNK:] + res[:CHUNK, :dk]
            o_ref[1, rows_b, cs_l[hh]] = wqb[hh][CHUNK:] + res[:CHUNK, dk:]
            s_ref[hh] = stf[hh] * jnp.exp(glf) + res[CHUNK:, :dk]
            s_ref[hb + hh] = stb[hh] * jnp.exp(glb) + res[CHUNK:, dk:]
        return carry

    lax.fori_loop(0, n_chunks, step, 0)

    for hh in range(hb):
        cs = slice(hh * dk, (hh + 1) * dk)
        o = o_ref[0, :, cs] + o_ref[1, :, cs]
        o = o * lax.rsqrt(jnp.mean(o * o, axis=-1, keepdims=True) + RMS_EPS) * ng_ref[...] * _silu(z_ref[:, cs])
        og_ref[:, cs] = o.astype(og_ref.dtype)
    if emit_state:
        for d in range(2):
            for hh in range(hb):
                sout_ref[d, hh] = s_ref[d * hb + hh]


def _gdn_core(proj, ab, conv_w, alog_row, dtb_row, ng_row, s0, og_prev, state_prev, *, row0, n_seq, t, hb,
              n_heads, state_slot):
    dk = LANES
    wb = hb * dk
    n_hg = n_heads // hb
    rb = row0 // t
    has_s0 = s0 is not None
    emit_state = state_slot is not None

    def col_spec(part):
        return pl.BlockSpec((t, wb), lambda b, h: (rb + b, part * n_hg + h))

    def conv_spec(part):
        return pl.BlockSpec((CONV_K, wb), lambda b, h: (0, part * n_hg + h))

    row_spec = pl.BlockSpec((1, LANES), lambda b, h: (0, 0))
    in_specs = [col_spec(0), col_spec(1), col_spec(2), col_spec(3),
                pl.BlockSpec((t, LANES), lambda b, h: (rb + b, 0)),
                conv_spec(0), conv_spec(1), conv_spec(2), row_spec, row_spec, row_spec]
    args = [proj, proj, proj, proj, ab, conv_w, conv_w, conv_w, alog_row, dtb_row, ng_row]
    if has_s0:
        in_specs.append(pl.BlockSpec((None, 2, hb, dk, dk), lambda b, h: (b, 0, h, 0, 0)))
        args.append(s0)
    out_specs = [pl.BlockSpec((t, wb), lambda b, h: (rb + b, h))]
    out_shape = [jax.ShapeDtypeStruct((proj.shape[0], n_heads * dk), BF16)]
    if emit_state:
        mi, n_mix = state_slot
        out_specs.append(pl.BlockSpec((None, None, 2, hb, dk, dk), lambda b, h: (b, mi, 0, h, 0, 0)))
        out_shape.append(jax.ShapeDtypeStruct((n_seq, n_mix, 2, n_heads, dk, dk), F32))
    aliases = {}
    for out_idx, prev in enumerate((og_prev, state_prev)):
        if prev is not None:
            aliases[len(args)] = out_idx
            in_specs.append(pl.BlockSpec(memory_space=pl.ANY))
            args.append(prev)
    return pl.pallas_call(
        functools.partial(_gdn_kernel, t=t, hb=hb, n_heads=n_heads, has_s0=has_s0, n_alias=len(aliases),
                          emit_state=emit_state),
        grid=(n_seq, n_hg),
        in_specs=in_specs,
        out_specs=out_specs,
        out_shape=out_shape,
        input_output_aliases=aliases,
        scratch_shapes=[
            pltpu.VMEM((t, wb), F32), pltpu.VMEM((t, wb), F32), pltpu.VMEM((t, wb), F32),
            pltpu.VMEM((2 * hb, t, LANES), F32), pltpu.VMEM((2 * hb, t, LANES), F32),
            pltpu.VMEM((2 * hb, dk, dk), F32), pltpu.VMEM((2, t, wb), F32),
            pltpu.VMEM((t + 2 * SUBLANES, wb), F32),
        ],
        compiler_params=_cparams("parallel", "parallel"),
        name="gdn_core",
    )(*args)


def _out_kernel(x_ref, mod_ref, a_ref, w_ref, o_ref):
    y = jnp.dot(a_ref[...], w_ref[...], preferred_element_type=F32)
    o_ref[...] = x_ref[...] + mod_ref[5:6, :] * y


def _gdn_out(x, mods_l, og, w_out, row_of_tile, tm):
    m, d = x.shape
    kdim = og.shape[1]
    return pl.pallas_call(
        _out_kernel,
        grid=(m // tm,),
        in_specs=[
            pl.BlockSpec((tm, d), lambda i: (i, 0)),
            pl.BlockSpec((None, N_MOD, d), lambda i: (row_of_tile(i), 0, 0)),
            pl.BlockSpec((tm, kdim), lambda i: (i, 0)),
            pl.BlockSpec((kdim, d), lambda i: (0, 0)),
        ],
        out_specs=pl.BlockSpec((tm, d), lambda i: (i, 0)),
        out_shape=jax.ShapeDtypeStruct((m, d), F32),
        compiler_params=_cparams("parallel"),
        name="gdn_out",
    )(x, mods_l, og, w_out)


def _pool_kernel(x_ref, mod_ref, g_ref, w_ref, sc_ref, *rest, t):
    o_ref = rest[-1]
    x = x_ref[...]
    rstd = lax.rsqrt(jnp.mean(x * x, axis=-1, keepdims=True) + RMS_EPS)
    pg = w_ref.shape[1]
    edge = max(POOL_WINDOWS)
    zeros = jnp.zeros((edge, pg), F32)
    row = lax.broadcasted_iota(jnp.int32, (t, pg), 0)
    for gi, win in enumerate(POOL_WINDOWS):
        cs = slice(gi * pg, (gi + 1) * pg)
        xg = x_ref[:, cs]
        h = (xg * rstd * g_ref[:, cs]) * (1.0 + mod_ref[4:5, cs]) + mod_ref[3:4, cs]
        half = win // 2
        n = t + 2 * edge
        acc = jnp.concatenate([zeros, h, zeros], axis=0)
        s = 1
        while s < win:
            acc = acc + pltpu.roll(acc, n - s, 0)
            s *= 2
        acc = acc[edge - half:edge - half + t]
        cnt = (jnp.minimum(row + (win - half), t) - jnp.maximum(row - half, 0)).astype(F32)
        diff = (acc / cnt - h).astype(BF16)
        y = jnp.dot(diff, w_ref[gi], preferred_element_type=F32) * sc_ref[:, cs]
        o_ref[:, cs] = xg + mod_ref[5:6, cs] * y


def _pool(x, mods_l, g, w, scale, out_prev, *, row0, n_seq, t, mod_row0, mod_per_seq):
    m, d = x.shape
    ng, pg, _ = w.shape
    assert ng == len(POOL_WINDOWS) and ng * pg == d
    rb = row0 // t
    mod_idx = (lambda b: mod_row0 + b) if mod_per_seq else (lambda b: mod_row0)
    in_specs = [
        pl.BlockSpec((t, d), lambda b: (rb + b, 0)),
        pl.BlockSpec((None, N_MOD, d), lambda b: (mod_idx(b), 0, 0)),
        pl.BlockSpec((1, d), lambda b: (0, 0)),
        pl.BlockSpec((ng, pg, pg), lambda b: (0, 0, 0)),
        pl.BlockSpec((1, d), lambda b: (0, 0)),
    ]
    args = [x, mods_l, g, w, scale]
    aliases = {}
    if out_prev is not None:
        aliases[len(args)] = 0
        in_specs.append(pl.BlockSpec(memory_space=pl.ANY))
        args.append(out_prev)
    return pl.pallas_call(
        functools.partial(_pool_kernel, t=t),
        grid=(n_seq,),
        in_specs=in_specs,
        out_specs=pl.BlockSpec((t, d), lambda b: (rb + b, 0)),
        out_shape=jax.ShapeDtypeStruct((m, d), F32),
        input_output_aliases=aliases,
        compiler_params=_cparams("parallel"),
        name="pool",
    )(*args)


def _final_kernel(x_ref, g_ref, o_ref):
    x = x_ref[...]
    o_ref[...] = x * lax.rsqrt(jnp.mean(x * x, axis=-1, keepdims=True) + RMS_EPS) * g_ref[...]


def _final_norm(x, g, tm, row0, rows):
    _, d = x.shape
    rb = row0 // tm
    return pl.pallas_call(
        _final_kernel,
        grid=(rows // tm,),
        in_specs=[pl.BlockSpec((tm, d), lambda i: (rb + i, 0)), pl.BlockSpec((1, d), lambda i: (0, 0))],
        out_specs=pl.BlockSpec((tm, d), lambda i: (i, 0)),
        out_shape=jax.ShapeDtypeStruct((rows, d), F32),
        compiler_params=_cparams("parallel"),
        name="final_norm",
    )(x, g)


def _place_cols(vals, n_heads):
    row = jnp.zeros((LANES,), F32)
    for d in range(2):
        row = lax.dynamic_update_slice(row, vals[d].astype(F32), (d * 2 * n_heads,))
    return row.reshape(1, LANES)


def kernel(x_prompt, x_sample, state_gdn, c, c_ctx, w_mod, b_mod, norm_g, ffn_w_in, ffn_w_out, gdn_w_in, gdn_conv, gdn_a_log, gdn_dt_bias, gdn_norm_g, gdn_w_out, pool_w, pool_scale, final_g):
    batch, seq, d = x_prompt.shape
    dec_batch, dec_seq, _ = x_sample.shape
    depth = w_mod.shape[0]
    n_heads = gdn_a_log.shape[-1]
    dk = LANES
    qk = n_heads * dk
    f = ffn_w_out.shape[2]
    fp = FF_TILE * ((f + FF_TILE - 1) // FF_TILE)
    m_p = batch * seq
    m_s = dec_batch * dec_seq
    assert 1 + dec_batch <= MOD_ROWS and gdn_w_in.shape[2] == 4 * qk + 4 * n_heads and 4 * n_heads <= LANES
    assert m_p % dec_seq == 0
    def token_tile(candidates):
        return next(c_ for c_ in candidates if m_p % c_ == 0 and dec_seq % c_ == 0)

    def mod_row_of_tile(tile):
        n_prompt_tiles = m_p // tile
        return lambda i: jnp.where(i < n_prompt_tiles, 0, 1 + ((i - n_prompt_tiles) * tile) // dec_seq)

    tm = token_tile((512, 256, 128, 64))
    row_of_tile = mod_row_of_tile(tm)
    tm_proj = token_tile((1024, 512, 256, 128, 64))

    c8 = jnp.zeros((MOD_ROWS, d), F32).at[0].set(c_ctx).at[1:1 + dec_batch].set(c)
    mods = _modulation(c8, w_mod, b_mod).reshape(depth, MOD_ROWS, N_MOD, d)

    xs = _add_pos(x_sample, _grid_pos_embed(dec_seq, d, x_sample.dtype))
    x = jnp.concatenate([x_prompt.reshape(m_p, d), xs.reshape(m_s, d)], axis=0)

    w_gate_p, w_up_p, w_out_p = _prep_ffn_weights(ffn_w_in, ffn_w_out, fp)

    n_gdn = (depth + 1) // 2
    new_state = None
    for l in range(depth):
        mods_l = mods[l]
        ng = norm_g[l]

        def ffn(x, i, sub):
            return _ffn(x, mods_l, ng[sub:sub + 1], w_gate_p, w_up_p, w_out_p, l, i, row_of_tile, tm, sub)

        x = ffn(x, 0, 0)
        mi = l // 2
        if l % 2 == 0:
            w_in = gdn_w_in[mi]
            w_main = w_in[:, :4 * qk].astype(BF16)
            w_ab = jnp.pad(w_in[:, 4 * qk:], ((0, 0), (0, LANES - 4 * n_heads))).astype(BF16)
            proj, ab = _gdn_proj(x, mods_l, ng[1:2], w_main, w_ab, mod_row_of_tile(tm_proj), tm_proj)
            alog_row = _place_cols(gdn_a_log[mi], n_heads)
            dtb_row = _place_cols(gdn_dt_bias[mi], n_heads)
            ng_row = gdn_norm_g[mi].reshape(1, dk).astype(F32)
            core = functools.partial(_gdn_core, proj, ab, gdn_conv[mi], alog_row, dtb_row, ng_row, n_heads=n_heads)
            og, new_state = core(None, None, new_state, row0=0, n_seq=batch, t=seq, hb=min(16, n_heads),
                                 state_slot=(mi, n_gdn))
            (og,) = core(state_gdn[:, mi], og, None, row0=m_p, n_seq=dec_batch, t=dec_seq, hb=min(4, n_heads),
                         state_slot=None)
            x = _gdn_out(x, mods_l, og, gdn_w_out[mi].astype(BF16), row_of_tile, tm)
        else:
            pw = pool_w[mi].astype(BF16)
            ps = pool_scale[mi].reshape(1, d)
            pool = functools.partial(_pool, x, mods_l, ng[1:2], pw, ps)
            x_new = pool(None, row0=0, n_seq=batch, t=seq, mod_row0=0, mod_per_seq=False)
            x = pool(x_new, row0=m_p, n_seq=dec_batch, t=dec_seq, mod_row0=1, mod_per_seq=True)
        x = ffn(x, 1, 2)

    fg = final_g.reshape(1, d)
    y_prompt = _final_norm(x, fg, tm, 0, m_p).reshape(batch, seq, d)
    y_sample = _final_norm(x, fg, tm, m_p, m_s).reshape(dec_batch, dec_seq, d)
    return (y_prompt, y_sample, new_state.astype(state_gdn.dtype))
```

```python
import functools
import math

import jax
import jax.numpy as jnp
from jax import lax
from jax.experimental import pallas as pl
from jax.experimental.pallas import tpu as pltpu

F32 = jnp.float32
BF16 = jnp.bfloat16

RMS_EPS = 1e-6
L2_EPS = 1e-6
CHUNK = 64
CONV_K = 5
CONV_PAD = CONV_K // 2
POOL_WINDOWS = (2, 4, 8, 16)
GRID_W = 64
POS_BASE = 10000.0
N_MOD = 9
MOD_ROWS = 8
LANES = 128
SUBLANES = 8
FF_TILE = 512
VMEM_LIMIT = 56 * 1024 * 1024


def _cparams(*sem):
    return pltpu.CompilerParams(dimension_semantics=sem, vmem_limit_bytes=VMEM_LIMIT)


def _sigmoid(x):
    return jax.nn.sigmoid(x)


def _silu(x):
    return x * _sigmoid(x)


def _dot(a, b):
    return jnp.dot(a.astype(BF16), b.astype(BF16), preferred_element_type=F32)


def _dot_nt(a, b):
    return lax.dot_general(a.astype(BF16), b.astype(BF16), (((1,), (1,)), ((), ())), preferred_element_type=F32)


def _split(a):
    hi = a.astype(BF16)
    lo = (a - hi.astype(F32)).astype(BF16)
    return hi, lo


def _mod_norm(x, g, scale, shift):
    ms = jnp.mean(x * x, axis=-1, keepdims=True)
    return (x * lax.rsqrt(ms + RMS_EPS) * g) * (1.0 + scale) + shift


def _mod_kernel(c_ref, w_ref, b_ref, o_ref):
    s_hi, s_lo = _split(_silu(c_ref[...]))
    w = w_ref[...].astype(BF16)
    d = functools.partial(jnp.dot, preferred_element_type=F32)
    o_ref[...] = d(s_hi, w) + d(s_lo, w) + b_ref[...]


def _modulation(c8, w_mod, b_mod):
    depth, d, nd = w_mod.shape
    tn = next(c_ for c_ in (1024, 512, 256, 128) if nd % c_ == 0)
    return pl.pallas_call(
        _mod_kernel,
        grid=(depth, nd // tn),
        in_specs=[
            pl.BlockSpec((MOD_ROWS, d), lambda l, j: (0, 0)),
            pl.BlockSpec((None, d, tn), lambda l, j: (l, 0, j)),
            pl.BlockSpec((None, 1, tn), lambda l, j: (l, 0, j)),
        ],
        out_specs=pl.BlockSpec((None, MOD_ROWS, tn), lambda l, j: (l, 0, j)),
        out_shape=jax.ShapeDtypeStruct((depth, MOD_ROWS, nd), F32),
        compiler_params=_cparams("parallel", "parallel"),
        name="modulation",
    )(c8, w_mod, b_mod.reshape(depth, 1, nd))


def _tokens_kernel(xp_ref, xs_ref, p_ref, o_ref, *, n_prompt_tiles):
    i = pl.program_id(0)

    @pl.when(i < n_prompt_tiles)
    def _():
        o_ref[...] = xp_ref[...]

    @pl.when(i >= n_prompt_tiles)
    def _():
        o_ref[...] = xs_ref[...] + p_ref[...]


def _token_stream(xp, xs, pos, ta):
    m_p, d = xp.shape
    m_s = xs.shape[0]
    t = pos.shape[0]
    n_p = m_p // ta
    return pl.pallas_call(
        functools.partial(_tokens_kernel, n_prompt_tiles=n_p),
        grid=((m_p + m_s) // ta,),
        in_specs=[
            pl.BlockSpec((ta, d), lambda i: (jnp.minimum(i, n_p - 1), 0)),
            pl.BlockSpec((ta, d), lambda i: (jnp.maximum(i - n_p, 0), 0)),
            pl.BlockSpec((ta, d), lambda i: ((jnp.maximum(i - n_p, 0) * ta % t) // ta, 0)),
        ],
        out_specs=pl.BlockSpec((ta, d), lambda i: (i, 0)),
        out_shape=jax.ShapeDtypeStruct((m_p + m_s, d), xp.dtype),
        compiler_params=_cparams("parallel"),
        name="token_stream",
    )(xp, xs, pos)


def _grid_pos_embed(n_tok, d, dtype):
    idx = jnp.arange(n_tok)
    r = (idx // GRID_W).astype(F32)
    col = (idx % GRID_W).astype(F32)
    n_freq = d // 4
    freqs = jnp.exp(-math.log(POS_BASE) * jnp.arange(n_freq, dtype=F32) / n_freq)
    ar = r[:, None] * freqs
    ac = col[:, None] * freqs
    return jnp.concatenate([jnp.sin(ar), jnp.cos(ar), jnp.sin(ac), jnp.cos(ac)], axis=-1).astype(dtype)


def _prep_in_kernel(w_ref, g_ref, u_ref, *, f):
    pad = jnp.zeros((g_ref.shape[0], g_ref.shape[1] - f), BF16)
    g_ref[:, :f] = w_ref[:, :f].astype(BF16)
    u_ref[:, :f] = w_ref[:, f:].astype(BF16)
    g_ref[:, f:] = pad
    u_ref[:, f:] = pad


def _prep_out_kernel(w_ref, o_ref, *, f):
    row = pl.program_id(2) * FF_TILE + lax.broadcasted_iota(jnp.int32, w_ref.shape, 0)
    o_ref[...] = jnp.where(row < f, w_ref[...], 0.0).astype(BF16)


def _prep_ffn_weights(w_in, w_out, fp):
    depth, two, d, f2 = w_in.shape
    f = f2 // 2
    assert f % LANES == 0
    rows = 256
    spec_w = pl.BlockSpec((None, None, rows, f2), lambda l, k, r: (l, k, r, 0))
    spec_p = pl.BlockSpec((None, None, rows, fp), lambda l, k, r: (l, k, r, 0))
    padded = jax.ShapeDtypeStruct((depth, two, d, fp), BF16)
    w_gate_p, w_up_p = pl.pallas_call(
        functools.partial(_prep_in_kernel, f=f),
        grid=(depth, two, d // rows),
        in_specs=[spec_w],
        out_specs=[spec_p, spec_p],
        out_shape=[padded, padded],
        compiler_params=_cparams("parallel", "parallel", "parallel"),
        name="prep_w_in",
    )(w_in)
    w_out_p = pl.pallas_call(
        functools.partial(_prep_out_kernel, f=f),
        grid=(depth, two, fp // FF_TILE),
        in_specs=[pl.BlockSpec((None, None, FF_TILE, d), lambda l, k, j: (l, k, j, 0))],
        out_specs=pl.BlockSpec((None, None, FF_TILE, d), lambda l, k, j: (l, k, j, 0)),
        out_shape=jax.ShapeDtypeStruct((depth, two, fp, d), BF16),
        compiler_params=_cparams("parallel", "parallel", "parallel"),
        name="prep_w_out",
    )(w_out)
    return w_gate_p, w_up_p, w_out_p


def _ffn_kernel(x_ref, mod_ref, g_ref, wg_ref, wu_ref, wo_ref, o_ref, h_ref, *, sub):
    j = pl.program_id(1)

    @pl.when(j == 0)
    def _():
        h = _mod_norm(x_ref[...], g_ref[...], mod_ref[3 * sub + 1:3 * sub + 2, :], mod_ref[3 * sub:3 * sub + 1, :])
        h_ref[...] = h.astype(BF16)
        o_ref[...] = jnp.zeros_like(o_ref)

    h = h_ref[...]
    gate = jnp.dot(h, wg_ref[...], preferred_element_type=F32)
    up = jnp.dot(h, wu_ref[...], preferred_element_type=F32)
    act = (_silu(gate) * up).astype(BF16)
    o_ref[...] += jnp.dot(act, wo_ref[...], preferred_element_type=F32)

    @pl.when(j == pl.num_programs(1) - 1)
    def _():
        o_ref[...] = x_ref[...] + (0.5 * mod_ref[3 * sub + 2:3 * sub + 3, :]) * o_ref[...]


def _ffn(x, mods_l, g, w_gate_p, w_up_p, w_out_p, l, k, row_of_tile, tm, sub):
    m, d = x.shape
    fp = w_out_p.shape[2]
    nj = fp // FF_TILE
    return pl.pallas_call(
        functools.partial(_ffn_kernel, sub=sub),
        grid=(m // tm, nj),
        in_specs=[
            pl.BlockSpec((tm, d), lambda i, j: (i, 0)),
            pl.BlockSpec((None, N_MOD, d), lambda i, j: (row_of_tile(i), 0, 0)),
            pl.BlockSpec((1, d), lambda i, j: (0, 0)),
            pl.BlockSpec((None, None, d, FF_TILE), lambda i, j: (l, k, 0, j)),
            pl.BlockSpec((None, None, d, FF_TILE), lambda i, j: (l, k, 0, j)),
            pl.BlockSpec((None, None, FF_TILE, d), lambda i, j: (l, k, j, 0)),
        ],
        out_specs=pl.BlockSpec((tm, d), lambda i, j: (i, 0)),
        out_shape=jax.ShapeDtypeStruct((m, d), F32),
        scratch_shapes=[pltpu.VMEM((tm, d), BF16)],
        compiler_params=_cparams("parallel", "arbitrary"),
        name="ffn",
    )(x, mods_l, g, w_gate_p, w_up_p, w_out_p)


def _proj_kernel(x_ref, mod_ref, g_ref, w_ref, wab_ref, o_ref, ab_ref, h_ref):
    @pl.when(pl.program_id(1) == 0)
    def _():
        h = _mod_norm(x_ref[...], g_ref[...], mod_ref[4:5, :], mod_ref[3:4, :]).astype(BF16)
        h_ref[...] = h
        ab_ref[...] = jnp.dot(h, wab_ref[...], preferred_element_type=F32)

    o_ref[...] = jnp.dot(h_ref[...], w_ref[...], preferred_element_type=F32)


def _gdn_proj(x, mods_l, g, w_main, w_ab, row_of_tile, tm):
    m, d = x.shape
    n = w_main.shape[1]
    tn = 1024
    return pl.pallas_call(
        _proj_kernel,
        grid=(m // tm, n // tn),
        in_specs=[
            pl.BlockSpec((tm, d), lambda i, j: (i, 0)),
            pl.BlockSpec((None, N_MOD, d), lambda i, j: (row_of_tile(i), 0, 0)),
            pl.BlockSpec((1, d), lambda i, j: (0, 0)),
            pl.BlockSpec((d, tn), lambda i, j: (0, j)),
            pl.BlockSpec((d, LANES), lambda i, j: (0, 0)),
        ],
        out_specs=[pl.BlockSpec((tm, tn), lambda i, j: (i, j)), pl.BlockSpec((tm, LANES), lambda i, j: (i, 0))],
        out_shape=[jax.ShapeDtypeStruct((m, n), F32), jax.ShapeDtypeStruct((m, LANES), F32)],
        scratch_shapes=[pltpu.VMEM((tm, d), BF16)],
        compiler_params=_cparams("parallel", "arbitrary"),
        name="gdn_proj",
    )(x, mods_l, g, w_main, w_ab)


def _row_sum_lanes(x):
    return jnp.dot(x.astype(BF16), jnp.ones((LANES, LANES), BF16), preferred_element_type=F32)


def _conv_silu(x_ref, w_ref, o_ref, pad_ref, t):
    zeros = jnp.zeros((SUBLANES, x_ref.shape[1]), F32)
    pad_ref[0:SUBLANES, :] = zeros
    pad_ref[t + SUBLANES:t + 2 * SUBLANES, :] = zeros
    pad_ref[SUBLANES:t + SUBLANES, :] = x_ref[...]
    acc = x_ref[...] * w_ref[CONV_PAD:CONV_PAD + 1, :]
    for j in range(CONV_K):
        if j != CONV_PAD:
            r0 = SUBLANES + j - CONV_PAD
            acc = acc + pad_ref[r0:r0 + t, :] * w_ref[j:j + 1, :]
    o_ref[...] = _silu(acc)


def _chunk_cumsum(g, t, reverse):
    pos = lax.broadcasted_iota(jnp.int32, g.shape, 0) % CHUNK
    s = 1
    while s < CHUNK:
        if reverse:
            g = g + jnp.where(pos < CHUNK - s, pltpu.roll(g, t - s, 0), 0.0)
        else:
            g = g + jnp.where(pos >= s, pltpu.roll(g, s, 0), 0.0)
        s *= 2
    return g


def _gdn_kernel(*refs, t, hb, n_heads, has_s0, n_alias, emit_state):
    (q_ref, k_ref, v_ref, z_ref, ab_ref, cq_ref, ck_ref, cv_ref, alog_ref, dtb_ref, ng_ref), refs = refs[:11], refs[11:]
    if has_s0:
        s0_ref, refs = refs[0], refs[1:]
    refs = refs[n_alias:]
    og_ref, refs = refs[0], refs[1:]
    if emit_state:
        sout_ref, refs = refs[0], refs[1:]
    qn_ref, kn_ref, vc_ref, gam_ref, beta_ref, s_ref, o_ref, pad_ref = refs

    dk = LANES
    n_chunks = t // CHUNK
    hg = pl.program_id(1)

    _conv_silu(q_ref, cq_ref, qn_ref, pad_ref, t)
    _conv_silu(k_ref, ck_ref, kn_ref, pad_ref, t)
    _conv_silu(v_ref, cv_ref, vc_ref, pad_ref, t)
    for hh in range(hb):
        cs = slice(hh * dk, (hh + 1) * dk)
        qh = qn_ref[:, cs]
        kh = kn_ref[:, cs]
        qn_ref[:, cs] = qh * lax.rsqrt(_row_sum_lanes(qh * qh) + L2_EPS) * (dk ** -0.5)
        kn_ref[:, cs] = kh * lax.rsqrt(_row_sum_lanes(kh * kh) + L2_EPS)

    ab = ab_ref[...]
    pre = ab + dtb_ref[...]
    softplus = jnp.maximum(pre, 0.0) + jnp.log1p(jnp.exp(-jnp.abs(pre)))
    g_all = -jnp.exp(alog_ref[...]) * softplus
    beta_all = _sigmoid(ab)
    cum = (_chunk_cumsum(g_all, t, False), _chunk_cumsum(g_all, t, True))
    lane = lax.broadcasted_iota(jnp.int32, ab.shape, 1)
    for d in range(2):
        for hh in range(hb):
            col = d * 2 * n_heads + hg * hb + hh
            gsel = jnp.sum(jnp.where(lane == col, cum[d], 0.0), axis=-1, keepdims=True)
            bsel = jnp.sum(jnp.where(lane == col + n_heads, beta_all, 0.0), axis=-1, keepdims=True)
            gam_ref[d * hb + hh] = jnp.broadcast_to(gsel, ab.shape)
            beta_ref[d * hb + hh] = jnp.broadcast_to(bsel, ab.shape)
            s_ref[d * hb + hh] = s0_ref[d, hh] if has_s0 else jnp.zeros((dk, dk), F32)
    ii = lax.broadcasted_iota(jnp.int32, (CHUNK, LANES), 0)
    lane = lax.broadcasted_iota(jnp.int32, (CHUNK, LANES), 1)
    is_f = lane < CHUNK
    jj = jnp.where(is_f, lane, lane - CHUNK)
    ahead = jnp.where(is_f, ii - jj, jj - ii)
    incl = ahead >= 0
    strict = ahead > 0
    eye = (ii == jj).astype(F32)
    pair_masks = []
    s_blk = 1
    while s_blk < CHUNK:
        pair_masks.append(((ii // (2 * s_blk)) == (jj // (2 * s_blk))) & ((ii // s_blk) != (jj // s_blk)))
        s_blk *= 2

    def block_diag(m):
        zero = jnp.zeros_like(m)
        return jnp.concatenate([jnp.where(is_f, m, zero), jnp.where(is_f, zero, m)], axis=0)

    def diag2(top, bottom):
        return jnp.concatenate([jnp.concatenate([top, jnp.zeros_like(bottom)], axis=1),
                                jnp.concatenate([jnp.zeros_like(top), bottom], axis=1)], axis=0)

    def step(s, carry):
        rows_f = pl.ds(pl.multiple_of(s * CHUNK, CHUNK), CHUNK)
        rows_b = pl.ds(pl.multiple_of((n_chunks - 1 - s) * CHUNK, CHUNK), CHUNK)
        heads = range(hb)
        cs_l = [slice(hh * dk, (hh + 1) * dk) for hh in heads]
        qf = [qn_ref[rows_f, cs] for cs in cs_l]
        kf = [kn_ref[rows_f, cs] for cs in cs_l]
        vf = [vc_ref[rows_f, cs] for cs in cs_l]
        qb = [qn_ref[rows_b, cs] for cs in cs_l]
        kb = [kn_ref[rows_b, cs] for cs in cs_l]
        vb = [vc_ref[rows_b, cs] for cs in cs_l]
        gf = [gam_ref[hh, rows_f, :] for hh in heads]
        gb = [gam_ref[hb + hh, rows_b, :] for hh in heads]
        bf = [beta_ref[hh, rows_f, :] for hh in heads]
        bb = [beta_ref[hb + hh, rows_b, :] for hh in heads]

        a_l, x_l, aqk_l = [], [], []
        for hh in heads:
            gr = jnp.concatenate([gf[hh], gb[hh]], axis=0).T[:CHUNK, :]
            diff = jnp.where(is_f, gf[hh], gb[hh]) - gr
            dec_incl = jnp.where(incl, jnp.exp(jnp.where(incl, diff, 0.0)), 0.0)
            lhs = jnp.concatenate([jnp.concatenate([kf[hh], kb[hh]], axis=1),
                                   jnp.concatenate([qf[hh], qb[hh]], axis=1)], axis=0)
            kq = _dot_nt(lhs, diag2(kf[hh], kb[hh]))
            a = jnp.where(is_f, bf[hh], bb[hh]) * kq[:CHUNK] * jnp.where(strict, dec_incl, 0.0)
            a = a.astype(BF16).astype(F32)
            a_l.append(a)
            aqk_l.append(kq[CHUNK:] * dec_incl)
            x_l.append(eye - jnp.where(pair_masks[0], a, 0.0))
        for pm in pair_masks[1:]:
            t_l = [_dot(x, block_diag(jnp.where(pm, a, 0.0))) for x, a in zip(x_l, a_l)]
            x_l = [x - _dot(tx, block_diag(x)) for x, tx in zip(x_l, t_l)]
        egf = [jnp.exp(g) for g in gf]
        egb = [jnp.exp(g) for g in gb]
        wu_l = []
        for hh in heads:
            r_f = jnp.concatenate([kf[hh] * (bf[hh] * egf[hh]), vf[hh] * bf[hh]], axis=1).astype(BF16)
            r_b = jnp.concatenate([kb[hh] * (bb[hh] * egb[hh]), vb[hh] * bb[hh]], axis=1).astype(BF16)
            wu_l.append(_dot(x_l[hh], diag2(r_f, r_b)))
        stf = [s_ref[hh] for hh in heads]
        stb = [s_ref[hb + hh] for hh in heads]
        wqf = [_dot(jnp.concatenate([wu[:, :dk], q * eg], axis=0), st) for wu, q, eg, st in zip(wu_l, qf, egf, stf)]
        wqb = [_dot(jnp.concatenate([wu[:, 2 * dk:3 * dk], q * eg], axis=0), st)
               for wu, q, eg, st in zip(wu_l, qb, egb, stb)]
        for hh in heads:
            vn_f = wu_l[hh][:, dk:2 * dk] - wqf[hh][:CHUNK]
            vn_b = wu_l[hh][:, 3 * dk:] - wqb[hh][:CHUNK]
            glf = gf[hh][CHUNK - 1:CHUNK, :]
            glb = gb[hh][0:1, :]
            kd = jnp.concatenate([kf[hh] * jnp.exp(glf - gf[hh]), kb[hh] * jnp.exp(glb - gb[hh])], axis=0)
            res = _dot(jnp.concatenate([aqk_l[hh], kd.T], axis=0), diag2(vn_f, vn_b))
            o_ref[0, rows_f, cs_l[hh]] = wqf[hh][CHUNK:] + res[:CHUNK, :dk]
            o_ref[1, rows_b, cs_l[hh]] = wqb[hh][CHUNK:] + res[:CHUNK, dk:]
            s_ref[hh] = stf[hh] * jnp.exp(glf) + res[CHUNK:, :dk]
            s_ref[hb + hh] = stb[hh] * jnp.exp(glb) + res[CHUNK:, dk:]
        return carry

    lax.fori_loop(0, n_chunks, step, 0)

    for hh in range(hb):
        cs = slice(hh * dk, (hh + 1) * dk)
        o = o_ref[0, :, cs] + o_ref[1, :, cs]
        o = o * lax.rsqrt(jnp.mean(o * o, axis=-1, keepdims=True) + RMS_EPS) * ng_ref[...] * _silu(z_ref[:, cs])
        og_ref[:, cs] = o.astype(og_ref.dtype)
    if emit_state:
        for d in range(2):
            for hh in range(hb):
                sout_ref[d, hh] = s_ref[d * hb + hh]


def _gdn_core(proj, ab, conv_w, alog_row, dtb_row, ng_row, s0, og_prev, state_prev, *, row0, n_seq, t, hb,
              n_heads, state_slot):
    dk = LANES
    wb = hb * dk
    n_hg = n_heads // hb
    rb = row0 // t
    has_s0 = s0 is not None
    emit_state = state_slot is not None

    def col_spec(part):
        return pl.BlockSpec((t, wb), lambda b, h: (rb + b, part * n_hg + h))

    def conv_spec(part):
        return pl.BlockSpec((CONV_K, wb), lambda b, h: (0, part * n_hg + h))

    row_spec = pl.BlockSpec((1, LANES), lambda b, h: (0, 0))
    in_specs = [col_spec(0), col_spec(1), col_spec(2), col_spec(3),
                pl.BlockSpec((t, LANES), lambda b, h: (rb + b, 0)),
                conv_spec(0), conv_spec(1), conv_spec(2), row_spec, row_spec, row_spec]
    args = [proj, proj, proj, proj, ab, conv_w, conv_w, conv_w, alog_row, dtb_row, ng_row]
    if has_s0:
        in_specs.append(pl.BlockSpec((None, 2, hb, dk, dk), lambda b, h: (b, 0, h, 0, 0)))
        args.append(s0)
    out_specs = [pl.BlockSpec((t, wb), lambda b, h: (rb + b, h))]
    out_shape = [jax.ShapeDtypeStruct((proj.shape[0], n_heads * dk), BF16)]
    if emit_state:
        mi, n_mix = state_slot
        out_specs.append(pl.BlockSpec((None, None, 2, hb, dk, dk), lambda b, h: (b, mi, 0, h, 0, 0)))
        out_shape.append(jax.ShapeDtypeStruct((n_seq, n_mix, 2, n_heads, dk, dk), F32))
    aliases = {}
    for out_idx, prev in enumerate((og_prev, state_prev)):
        if prev is not None:
            aliases[len(args)] = out_idx
            in_specs.append(pl.BlockSpec(memory_space=pl.ANY))
            args.append(prev)
    return pl.pallas_call(
        functools.partial(_gdn_kernel, t=t, hb=hb, n_heads=n_heads, has_s0=has_s0, n_alias=len(aliases),
                          emit_state=emit_state),
        grid=(n_seq, n_hg),
        in_specs=in_specs,
        out_specs=out_specs,
        out_shape=out_shape,
        input_output_aliases=aliases,
        scratch_shapes=[
            pltpu.VMEM((t, wb), F32), pltpu.VMEM((t, wb), F32), pltpu.VMEM((t, wb), F32),
            pltpu.VMEM((2 * hb, t, LANES), F32), pltpu.VMEM((2 * hb, t, LANES), F32),
            pltpu.VMEM((2 * hb, dk, dk), F32), pltpu.VMEM((2, t, wb), F32),
            pltpu.VMEM((t + 2 * SUBLANES, wb), F32),
        ],
        compiler_params=_cparams("parallel", "parallel"),
        name="gdn_core",
    )(*args)


def _out_kernel(x_ref, mod_ref, a_ref, w_ref, o_ref):
    y = jnp.dot(a_ref[...], w_ref[...], preferred_element_type=F32)
    o_ref[...] = x_ref[...] + mod_ref[5:6, :] * y


def _gdn_out(x, mods_l, og, w_out, row_of_tile, tm):
    m, d = x.shape
    kdim = og.shape[1]
    return pl.pallas_call(
        _out_kernel,
        grid=(m // tm,),
        in_specs=[
            pl.BlockSpec((tm, d), lambda i: (i, 0)),
            pl.BlockSpec((None, N_MOD, d), lambda i: (row_of_tile(i), 0, 0)),
            pl.BlockSpec((tm, kdim), lambda i: (i, 0)),
            pl.BlockSpec((kdim, d), lambda i: (0, 0)),
        ],
        out_specs=pl.BlockSpec((tm, d), lambda i: (i, 0)),
        out_shape=jax.ShapeDtypeStruct((m, d), F32),
        compiler_params=_cparams("parallel"),
        name="gdn_out",
    )(x, mods_l, og, w_out)


def _pool_kernel(x_ref, mod_ref, g_ref, w_ref, sc_ref, *rest, t):
    o_ref = rest[-1]
    x = x_ref[...]
    rstd = lax.rsqrt(jnp.mean(x * x, axis=-1, keepdims=True) + RMS_EPS)
    pg = w_ref.shape[1]
    edge = max(POOL_WINDOWS)
    zeros = jnp.zeros((edge, pg), F32)
    row = lax.broadcasted_iota(jnp.int32, (t, pg), 0)
    for gi, win in enumerate(POOL_WINDOWS):
        cs = slice(gi * pg, (gi + 1) * pg)
        xg = x_ref[:, cs]
        h = (xg * rstd * g_ref[:, cs]) * (1.0 + mod_ref[4:5, cs]) + mod_ref[3:4, cs]
        half = win // 2
        n = t + 2 * edge
        acc = jnp.concatenate([zeros, h, zeros], axis=0)
        s = 1
        while s < win:
            acc = acc + pltpu.roll(acc, n - s, 0)
            s *= 2
        acc = acc[edge - half:edge - half + t]
        cnt = (jnp.minimum(row + (win - half), t) - jnp.maximum(row - half, 0)).astype(F32)
        diff = (acc / cnt - h).astype(BF16)
        y = jnp.dot(diff, w_ref[gi], preferred_element_type=F32) * sc_ref[:, cs]
        o_ref[:, cs] = xg + mod_ref[5:6, cs] * y


def _pool(x, mods_l, g, w, scale, out_prev, *, row0, n_seq, t, mod_row0, mod_per_seq):
    m, d = x.shape
    ng, pg, _ = w.shape
    assert ng == len(POOL_WINDOWS) and ng * pg == d
    rb = row0 // t
    mod_idx = (lambda b: mod_row0 + b) if mod_per_seq else (lambda b: mod_row0)
    in_specs = [
        pl.BlockSpec((t, d), lambda b: (rb + b, 0)),
        pl.BlockSpec((None, N_MOD, d), lambda b: (mod_idx(b), 0, 0)),
        pl.BlockSpec((1, d), lambda b: (0, 0)),
        pl.BlockSpec((ng, pg, pg), lambda b: (0, 0, 0)),
        pl.BlockSpec((1, d), lambda b: (0, 0)),
    ]
    args = [x, mods_l, g, w, scale]
    aliases = {}
    if out_prev is not None:
        aliases[len(args)] = 0
        in_specs.append(pl.BlockSpec(memory_space=pl.ANY))
        args.append(out_prev)
    return pl.pallas_call(
        functools.partial(_pool_kernel, t=t),
        grid=(n_seq,),
        in_specs=in_specs,
        out_specs=pl.BlockSpec((t, d), lambda b: (rb + b, 0)),
        out_shape=jax.ShapeDtypeStruct((m, d), F32),
        input_output_aliases=aliases,
        compiler_params=_cparams("parallel"),
        name="pool",
    )(*args)


def _final_kernel(x_ref, g_ref, o_ref):
    x = x_ref[...]
    o_ref[...] = x * lax.rsqrt(jnp.mean(x * x, axis=-1, keepdims=True) + RMS_EPS) * g_ref[...]


def _final_norm(x, g, tm, row0, rows):
    _, d = x.shape
    rb = row0 // tm
    return pl.pallas_call(
        _final_kernel,
        grid=(rows // tm,),
        in_specs=[pl.BlockSpec((tm, d), lambda i: (rb + i, 0)), pl.BlockSpec((1, d), lambda i: (0, 0))],
        out_specs=pl.BlockSpec((tm, d), lambda i: (i, 0)),
        out_shape=jax.ShapeDtypeStruct((rows, d), F32),
        compiler_params=_cparams("parallel"),
        name="final_norm",
    )(x, g)


def _place_cols(vals, n_heads):
    row = jnp.zeros((LANES,), F32)
    for d in range(2):
        row = lax.dynamic_update_slice(row, vals[d].astype(F32), (d * 2 * n_heads,))
    return row.reshape(1, LANES)


def kernel(x_prompt, x_sample, state_gdn, c, c_ctx, w_mod, b_mod, norm_g, ffn_w_in, ffn_w_out, gdn_w_in, gdn_conv, gdn_a_log, gdn_dt_bias, gdn_norm_g, gdn_w_out, pool_w, pool_scale, final_g):
    batch, seq, d = x_prompt.shape
    dec_batch, dec_seq, _ = x_sample.shape
    depth = w_mod.shape[0]
    n_heads = gdn_a_log.shape[-1]
    dk = LANES
    qk = n_heads * dk
    f = ffn_w_out.shape[2]
    fp = FF_TILE * ((f + FF_TILE - 1) // FF_TILE)
    m_p = batch * seq
    m_s = dec_batch * dec_seq
    assert 1 + dec_batch <= MOD_ROWS and gdn_w_in.shape[2] == 4 * qk + 4 * n_heads and 4 * n_heads <= LANES
    assert m_p % dec_seq == 0

    def token_tile(candidates):
        return next(c_ for c_ in candidates if m_p % c_ == 0 and dec_seq % c_ == 0)

    def mod_row_of_tile(tile):
        n_prompt_tiles = m_p // tile
        return lambda i: jnp.where(i < n_prompt_tiles, 0, 1 + ((i - n_prompt_tiles) * tile) // dec_seq)

    tm = token_tile((512, 256, 128, 64))
    row_of_tile = mod_row_of_tile(tm)
    tm_proj = token_tile((1024, 512, 256, 128, 64))

    c8 = jnp.zeros((MOD_ROWS, d), F32).at[0].set(c_ctx).at[1:1 + dec_batch].set(c)
    mods = _modulation(c8, w_mod, b_mod).reshape(depth, MOD_ROWS, N_MOD, d)

    x = _token_stream(x_prompt.reshape(m_p, d), x_sample.reshape(m_s, d),
                      _grid_pos_embed(dec_seq, d, x_sample.dtype), tm)

    w_gate_p, w_up_p, w_out_p = _prep_ffn_weights(ffn_w_in, ffn_w_out, fp)

    n_gdn = (depth + 1) // 2
    new_state = None
    for l in range(depth):
        mods_l = mods[l]
        ng = norm_g[l]

        def ffn(x, i, sub):
            return _ffn(x, mods_l, ng[sub:sub + 1], w_gate_p, w_up_p, w_out_p, l, i, row_of_tile, tm, sub)

        x = ffn(x, 0, 0)
        mi = l // 2
        if l % 2 == 0:
            w_in = gdn_w_in[mi]
            w_main = w_in[:, :4 * qk].astype(BF16)
            w_ab = jnp.pad(w_in[:, 4 * qk:], ((0, 0), (0, LANES - 4 * n_heads))).astype(BF16)
            proj, ab = _gdn_proj(x, mods_l, ng[1:2], w_main, w_ab, mod_row_of_tile(tm_proj), tm_proj)
            alog_row = _place_cols(gdn_a_log[mi], n_heads)
            dtb_row = _place_cols(gdn_dt_bias[mi], n_heads)
            ng_row = gdn_norm_g[mi].reshape(1, dk).astype(F32)
            core = functools.partial(_gdn_core, proj, ab, gdn_conv[mi], alog_row, dtb_row, ng_row, n_heads=n_heads)
            og, new_state = core(None, None, new_state, row0=0, n_seq=batch, t=seq, hb=min(16, n_heads),
                                 state_slot=(mi, n_gdn))
            (og,) = core(state_gdn[:, mi], og, None, row0=m_p, n_seq=dec_batch, t=dec_seq, hb=min(4, n_heads),
                         state_slot=None)
            x = _gdn_out(x, mods_l, og, gdn_w_out[mi].astype(BF16), row_of_tile, tm)
        else:
            pw = pool_w[mi].astype(BF16)
            ps = pool_scale[mi].reshape(1, d)
            pool = functools.partial(_pool, x, mods_l, ng[1:2], pw, ps)
            x_new = pool(None, row0=0, n_seq=batch, t=seq, mod_row0=0, mod_per_seq=False)
            x = pool(x_new, row0=m_p, n_seq=dec_batch, t=dec_seq, mod_row0=1, mod_per_seq=True)
        x = ffn(x, 1, 2)

    fg = final_g.reshape(1, d)
    y_prompt = _final_norm(x, fg, tm, 0, m_p).reshape(batch, seq, d)
    y_sample = _final_norm(x, fg, tm, m_p, m_s).reshape(dec_batch, dec_seq, d)
    return (y_prompt, y_sample, new_state.astype(state_gdn.dtype))
```

```python
import functools
import math

import jax
import jax.numpy as jnp
from jax import lax
from jax.experimental import pallas as pl
from jax.experimental.pallas import tpu as pltpu

F32 = jnp.float32
BF16 = jnp.bfloat16

RMS_EPS = 1e-6
L2_EPS = 1e-6
CHUNK = 64
CONV_K = 5
CONV_PAD = CONV_K // 2
POOL_WINDOWS = (2, 4, 8, 16)
GRID_W = 64
POS_BASE = 10000.0
N_MOD = 9
MOD_ROWS = 8
LANES = 128
SUBLANES = 8
FF_TILE = 512
VMEM_LIMIT = 56 * 1024 * 1024


def _cparams(*sem):
    return pltpu.CompilerParams(dimension_semantics=sem, vmem_limit_bytes=VMEM_LIMIT)


def _sigmoid(x):
    return jax.nn.sigmoid(x)


def _silu(x):
    return x * _sigmoid(x)


def _dot(a, b):
    return jnp.dot(a.astype(BF16), b.astype(BF16), preferred_element_type=F32)


def _dot_nt(a, b):
    return lax.dot_general(a.astype(BF16), b.astype(BF16), (((1,), (1,)), ((), ())), preferred_element_type=F32)


def _split(a):
    hi = a.astype(BF16)
    lo = (a - hi.astype(F32)).astype(BF16)
    return hi, lo


def _mod_norm(x, g, scale, shift):
    ms = jnp.mean(x * x, axis=-1, keepdims=True)
    return (x * lax.rsqrt(ms + RMS_EPS) * g) * (1.0 + scale) + shift


def _mod_kernel(c_ref, w_ref, b_ref, o_ref):
    s_hi, s_lo = _split(_silu(c_ref[...]))
    w = w_ref[...].astype(BF16)
    d = functools.partial(jnp.dot, preferred_element_type=F32)
    o_ref[...] = d(s_hi, w) + d(s_lo, w) + b_ref[...]


def _modulation(c8, w_mod, b_mod):
    depth, d, nd = w_mod.shape
    tn = next(c_ for c_ in (2048, 1024, 512, 256, 128) if nd % c_ == 0)
    return pl.pallas_call(
        _mod_kernel,
        grid=(depth, nd // tn),
        in_specs=[
            pl.BlockSpec((MOD_ROWS, d), lambda l, j: (0, 0)),
            pl.BlockSpec((None, d, tn), lambda l, j: (l, 0, j)),
            pl.BlockSpec((None, 1, tn), lambda l, j: (l, 0, j)),
        ],
        out_specs=pl.BlockSpec((None, MOD_ROWS, tn), lambda l, j: (l, 0, j)),
        out_shape=jax.ShapeDtypeStruct((depth, MOD_ROWS, nd), F32),
        compiler_params=_cparams("parallel", "parallel"),
        name="modulation",
    )(c8, w_mod, b_mod.reshape(depth, 1, nd))


def _tokens_kernel(xp_ref, xs_ref, p_ref, o_ref, *, n_prompt_tiles):
    i = pl.program_id(0)

    @pl.when(i < n_prompt_tiles)
    def _():
        o_ref[...] = xp_ref[...]

    @pl.when(i >= n_prompt_tiles)
    def _():
        o_ref[...] = xs_ref[...] + p_ref[...]


def _token_stream(xp, xs, pos, ta):
    m_p, d = xp.shape
    m_s = xs.shape[0]
    t = pos.shape[0]
    n_p = m_p // ta
    return pl.pallas_call(
        functools.partial(_tokens_kernel, n_prompt_tiles=n_p),
        grid=((m_p + m_s) // ta,),
        in_specs=[
            pl.BlockSpec((ta, d), lambda i: (jnp.minimum(i, n_p - 1), 0)),
            pl.BlockSpec((ta, d), lambda i: (jnp.maximum(i - n_p, 0), 0)),
            pl.BlockSpec((ta, d), lambda i: ((jnp.maximum(i - n_p, 0) * ta % t) // ta, 0)),
        ],
        out_specs=pl.BlockSpec((ta, d), lambda i: (i, 0)),
        out_shape=jax.ShapeDtypeStruct((m_p + m_s, d), xp.dtype),
        compiler_params=_cparams("parallel"),
        name="token_stream",
    )(xp, xs, pos)


def _grid_pos_embed(n_tok, d, dtype):
    idx = jnp.arange(n_tok)
    r = (idx // GRID_W).astype(F32)
    col = (idx % GRID_W).astype(F32)
    n_freq = d // 4
    freqs = jnp.exp(-math.log(POS_BASE) * jnp.arange(n_freq, dtype=F32) / n_freq)
    ar = r[:, None] * freqs
    ac = col[:, None] * freqs
    return jnp.concatenate([jnp.sin(ar), jnp.cos(ar), jnp.sin(ac), jnp.cos(ac)], axis=-1).astype(dtype)


def _prep_in_kernel(w_ref, g_ref, u_ref, *, f):
    pad = jnp.zeros((g_ref.shape[0], g_ref.shape[1] - f), BF16)
    g_ref[:, :f] = w_ref[:, :f].astype(BF16)
    u_ref[:, :f] = w_ref[:, f:].astype(BF16)
    g_ref[:, f:] = pad
    u_ref[:, f:] = pad


def _prep_out_kernel(w_ref, o_ref, *, f):
    row = pl.program_id(2) * FF_TILE + lax.broadcasted_iota(jnp.int32, w_ref.shape, 0)
    o_ref[...] = jnp.where(row < f, w_ref[...], 0.0).astype(BF16)


def _prep_ffn_weights(w_in, w_out, fp):
    depth, two, d, f2 = w_in.shape
    f = f2 // 2
    assert f % LANES == 0
    rows = 256
    spec_w = pl.BlockSpec((None, None, rows, f2), lambda l, k, r: (l, k, r, 0))
    spec_p = pl.BlockSpec((None, None, rows, fp), lambda l, k, r: (l, k, r, 0))
    padded = jax.ShapeDtypeStruct((depth, two, d, fp), BF16)
    w_gate_p, w_up_p = pl.pallas_call(
        functools.partial(_prep_in_kernel, f=f),
        grid=(depth, two, d // rows),
        in_specs=[spec_w],
        out_specs=[spec_p, spec_p],
        out_shape=[padded, padded],
        compiler_params=_cparams("parallel", "parallel", "parallel"),
        name="prep_w_in",
    )(w_in)
    w_out_p = pl.pallas_call(
        functools.partial(_prep_out_kernel, f=f),
        grid=(depth, two, fp // FF_TILE),
        in_specs=[pl.BlockSpec((None, None, FF_TILE, d), lambda l, k, j: (l, k, j, 0))],
        out_specs=pl.BlockSpec((None, None, FF_TILE, d), lambda l, k, j: (l, k, j, 0)),
        out_shape=jax.ShapeDtypeStruct((depth, two, fp, d), BF16),
        compiler_params=_cparams("parallel", "parallel", "parallel"),
        name="prep_w_out",
    )(w_out)
    return w_gate_p, w_up_p, w_out_p


def _ffn_kernel(x_ref, mod_ref, g_ref, wg_ref, wu_ref, wo_ref, o_ref, h_ref, *, sub):
    j = pl.program_id(1)
    last = pl.num_programs(1) - 1

    def partial_out(h):
        gate = jnp.dot(h, wg_ref[...], preferred_element_type=F32)
        up = jnp.dot(h, wu_ref[...], preferred_element_type=F32)
        act = (_silu(gate) * up).astype(BF16)
        return jnp.dot(act, wo_ref[...], preferred_element_type=F32)

    @pl.when(j == 0)
    def _():
        h = _mod_norm(x_ref[...], g_ref[...], mod_ref[3 * sub + 1:3 * sub + 2, :], mod_ref[3 * sub:3 * sub + 1, :])
        h = h.astype(BF16)
        h_ref[...] = h
        o_ref[...] = partial_out(h)

    @pl.when((j > 0) & (j < last))
    def _():
        o_ref[...] += partial_out(h_ref[...])

    @pl.when(j == last)
    def _():
        y = o_ref[...] + partial_out(h_ref[...])
        o_ref[...] = x_ref[...] + (0.5 * mod_ref[3 * sub + 2:3 * sub + 3, :]) * y


def _ffn(x, mods_l, g, w_gate_p, w_up_p, w_out_p, l, k, row_of_tile, tm, sub):
    m, d = x.shape
    fp = w_out_p.shape[2]
    nj = fp // FF_TILE
    assert nj >= 2
    return pl.pallas_call(
        functools.partial(_ffn_kernel, sub=sub),
        grid=(m // tm, nj),
        in_specs=[
            pl.BlockSpec((tm, d), lambda i, j: (i, 0)),
            pl.BlockSpec((None, N_MOD, d), lambda i, j: (row_of_tile(i), 0, 0)),
            pl.BlockSpec((1, d), lambda i, j: (0, 0)),
            pl.BlockSpec((None, None, d, FF_TILE), lambda i, j: (l, k, 0, j)),
            pl.BlockSpec((None, None, d, FF_TILE), lambda i, j: (l, k, 0, j)),
            pl.BlockSpec((None, None, FF_TILE, d), lambda i, j: (l, k, j, 0)),
        ],
        out_specs=pl.BlockSpec((tm, d), lambda i, j: (i, 0)),
        out_shape=jax.ShapeDtypeStruct((m, d), F32),
        scratch_shapes=[pltpu.VMEM((tm, d), BF16)],
        compiler_params=_cparams("parallel", "arbitrary"),
        name="ffn",
    )(x, mods_l, g, w_gate_p, w_up_p, w_out_p)


def _proj_kernel(x_ref, mod_ref, g_ref, w_ref, wab_ref, o_ref, ab_ref, h_ref):
    j = pl.program_id(1)

    @pl.when(j == 0)
    def _():
        h = _mod_norm(x_ref[...], g_ref[...], mod_ref[4:5, :], mod_ref[3:4, :]).astype(BF16)
        h_ref[...] = h
        o_ref[...] = jnp.dot(h, w_ref[...], preferred_element_type=F32)
        ab_ref[...] = jnp.dot(h, wab_ref[...], preferred_element_type=F32)

    @pl.when(j > 0)
    def _():
        o_ref[...] = jnp.dot(h_ref[...], w_ref[...], preferred_element_type=F32)


def _gdn_proj(x, mods_l, g, w_main, w_ab, row_of_tile, tm):
    m, d = x.shape
    n = w_main.shape[1]
    tn = 1024
    return pl.pallas_call(
        _proj_kernel,
        grid=(m // tm, n // tn),
        in_specs=[
            pl.BlockSpec((tm, d), lambda i, j: (i, 0)),
            pl.BlockSpec((None, N_MOD, d), lambda i, j: (row_of_tile(i), 0, 0)),
            pl.BlockSpec((1, d), lambda i, j: (0, 0)),
            pl.BlockSpec((d, tn), lambda i, j: (0, j)),
            pl.BlockSpec((d, LANES), lambda i, j: (0, 0)),
        ],
        out_specs=[pl.BlockSpec((tm, tn), lambda i, j: (i, j)), pl.BlockSpec((tm, LANES), lambda i, j: (i, 0))],
        out_shape=[jax.ShapeDtypeStruct((m, n), F32), jax.ShapeDtypeStruct((m, LANES), F32)],
        scratch_shapes=[pltpu.VMEM((tm, d), BF16)],
        compiler_params=_cparams("parallel", "arbitrary"),
        name="gdn_proj",
    )(x, mods_l, g, w_main, w_ab)


def _row_sum_lanes(x):
    return jnp.dot(x.astype(BF16), jnp.ones((LANES, LANES), BF16), preferred_element_type=F32)


def _conv_silu(x_ref, w_ref, o_ref, pad_ref, t):
    zeros = jnp.zeros((SUBLANES, x_ref.shape[1]), F32)
    pad_ref[0:SUBLANES, :] = zeros
    pad_ref[t + SUBLANES:t + 2 * SUBLANES, :] = zeros
    pad_ref[SUBLANES:t + SUBLANES, :] = x_ref[...]
    acc = x_ref[...] * w_ref[CONV_PAD:CONV_PAD + 1, :]
    for j in range(CONV_K):
        if j != CONV_PAD:
            r0 = SUBLANES + j - CONV_PAD
            acc = acc + pad_ref[r0:r0 + t, :] * w_ref[j:j + 1, :]
    o_ref[...] = _silu(acc)


def _chunk_cumsum(g, t, reverse):
    pos = lax.broadcasted_iota(jnp.int32, g.shape, 0) % CHUNK
    s = 1
    while s < CHUNK:
        if reverse:
            g = g + jnp.where(pos < CHUNK - s, pltpu.roll(g, t - s, 0), 0.0)
        else:
            g = g + jnp.where(pos >= s, pltpu.roll(g, s, 0), 0.0)
        s *= 2
    return g


def _gdn_kernel(*refs, t, hb, n_heads, has_s0, n_alias, emit_state):
    (q_ref, k_ref, v_ref, z_ref, ab_ref, cq_ref, ck_ref, cv_ref, alog_ref, dtb_ref, ng_ref), refs = refs[:11], refs[11:]
    if has_s0:
        s0_ref, refs = refs[0], refs[1:]
    refs = refs[n_alias:]
    og_ref, refs = refs[0], refs[1:]
    if emit_state:
        sout_ref, refs = refs[0], refs[1:]
    qn_ref, kn_ref, vc_ref, gam_ref, beta_ref, s_ref, o_ref, pad_ref = refs

    dk = LANES
    n_chunks = t // CHUNK
    hg = pl.program_id(1)

    _conv_silu(q_ref, cq_ref, qn_ref, pad_ref, t)
    _conv_silu(k_ref, ck_ref, kn_ref, pad_ref, t)
    _conv_silu(v_ref, cv_ref, vc_ref, pad_ref, t)
    for hh in range(hb):
        cs = slice(hh * dk, (hh + 1) * dk)
        qh = qn_ref[:, cs]
        kh = kn_ref[:, cs]
        qn_ref[:, cs] = qh * lax.rsqrt(_row_sum_lanes(qh * qh) + L2_EPS) * (dk ** -0.5)
        kn_ref[:, cs] = kh * lax.rsqrt(_row_sum_lanes(kh * kh) + L2_EPS)

    ab = ab_ref[...]
    pre = ab + dtb_ref[...]
    softplus = jnp.maximum(pre, 0.0) + jnp.log1p(jnp.exp(-jnp.abs(pre)))
    g_all = -jnp.exp(alog_ref[...]) * softplus
    beta_all = _sigmoid(ab)
    cum = (_chunk_cumsum(g_all, t, False), _chunk_cumsum(g_all, t, True))
    lane = lax.broadcasted_iota(jnp.int32, ab.shape, 1)
    for d in range(2):
        for hh in range(hb):
            col = d * 2 * n_heads + hg * hb + hh
            gsel = jnp.sum(jnp.where(lane == col, cum[d], 0.0), axis=-1, keepdims=True)
            bsel = jnp.sum(jnp.where(lane == col + n_heads, beta_all, 0.0), axis=-1, keepdims=True)
            gam_ref[d * hb + hh] = jnp.broadcast_to(gsel, ab.shape)
            beta_ref[d * hb + hh] = jnp.broadcast_to(bsel, ab.shape)
            s_ref[d * hb + hh] = s0_ref[d, hh] if has_s0 else jnp.zeros((dk, dk), F32)
    ii = lax.broadcasted_iota(jnp.int32, (CHUNK, LANES), 0)
    lane = lax.broadcasted_iota(jnp.int32, (CHUNK, LANES), 1)
    is_f = lane < CHUNK
    jj = jnp.where(is_f, lane, lane - CHUNK)
    ahead = jnp.where(is_f, ii - jj, jj - ii)
    incl = ahead >= 0
    strict = ahead > 0
    eye = (ii == jj).astype(F32)
    pair_masks = []
    s_blk = 1
    while s_blk < CHUNK:
        pair_masks.append(((ii // (2 * s_blk)) == (jj // (2 * s_blk))) & ((ii // s_blk) != (jj // s_blk)))
        s_blk *= 2

    def block_diag(m):
        zero = jnp.zeros_like(m)
        return jnp.concatenate([jnp.where(is_f, m, zero), jnp.where(is_f, zero, m)], axis=0)

    def diag2(top, bottom):
        return jnp.concatenate([jnp.concatenate([top, jnp.zeros_like(bottom)], axis=1),
                                jnp.concatenate([jnp.zeros_like(top), bottom], axis=1)], axis=0)

    def step(s, carry):
        rows_f = pl.ds(pl.multiple_of(s * CHUNK, CHUNK), CHUNK)
        rows_b = pl.ds(pl.multiple_of((n_chunks - 1 - s) * CHUNK, CHUNK), CHUNK)
        heads = range(hb)
        cs_l = [slice(hh * dk, (hh + 1) * dk) for hh in heads]
        qf = [qn_ref[rows_f, cs] for cs in cs_l]
        kf = [kn_ref[rows_f, cs] for cs in cs_l]
        vf = [vc_ref[rows_f, cs] for cs in cs_l]
        qb = [qn_ref[rows_b, cs] for cs in cs_l]
        kb = [kn_ref[rows_b, cs] for cs in cs_l]
        vb = [vc_ref[rows_b, cs] for cs in cs_l]
        gf = [gam_ref[hh, rows_f, :] for hh in heads]
        gb = [gam_ref[hb + hh, rows_b, :] for hh in heads]
        bf = [beta_ref[hh, rows_f, :] for hh in heads]
        bb = [beta_ref[hb + hh, rows_b, :] for hh in heads]

        a_l, x_l, aqk_l = [], [], []
        for hh in heads:
            gr = jnp.concatenate([gf[hh], gb[hh]], axis=0).T[:CHUNK, :]
            diff = jnp.where(is_f, gf[hh], gb[hh]) - gr
            dec_incl = jnp.where(incl, jnp.exp(jnp.where(incl, diff, 0.0)), 0.0)
            lhs = jnp.concatenate([jnp.concatenate([kf[hh], kb[hh]], axis=1),
                                   jnp.concatenate([qf[hh], qb[hh]], axis=1)], axis=0)
            kq = _dot_nt(lhs, diag2(kf[hh], kb[hh]))
            a = jnp.where(is_f, bf[hh], bb[hh]) * kq[:CHUNK] * jnp.where(strict, dec_incl, 0.0)
            a = a.astype(BF16).astype(F32)
            a_l.append(a)
            aqk_l.append(kq[CHUNK:] * dec_incl)
            x_l.append(eye - jnp.where(pair_masks[0], a, 0.0))
        for pm in pair_masks[1:]:
            t_l = [_dot(x, block_diag(jnp.where(pm, a, 0.0))) for x, a in zip(x_l, a_l)]
            x_l = [x - _dot(tx, block_diag(x)) for x, tx in zip(x_l, t_l)]
        egf = [jnp.exp(g) for g in gf]
        egb = [jnp.exp(g) for g in gb]
        wu_l = []
        for hh in heads:
            r_f = jnp.concatenate([kf[hh] * (bf[hh] * egf[hh]), vf[hh] * bf[hh]], axis=1).astype(BF16)
            r_b = jnp.concatenate([kb[hh] * (bb[hh] * egb[hh]), vb[hh] * bb[hh]], axis=1).astype(BF16)
            wu_l.append(_dot(x_l[hh], diag2(r_f, r_b)))
        stf = [s_ref[hh] for hh in heads]
        stb = [s_ref[hb + hh] for hh in heads]
        wqf = [_dot(jnp.concatenate([wu[:, :dk], q * eg], axis=0), st) for wu, q, eg, st in zip(wu_l, qf, egf, stf)]
        wqb = [_dot(jnp.concatenate([wu[:, 2 * dk:3 * dk], q * eg], axis=0), st)
               for wu, q, eg, st in zip(wu_l, qb, egb, stb)]
        for hh in heads:
            vn_f = wu_l[hh][:, dk:2 * dk] - wqf[hh][:CHUNK]
            vn_b = wu_l[hh][:, 3 * dk:] - wqb[hh][:CHUNK]
            glf = gf[hh][CHUNK - 1:CHUNK, :]
            glb = gb[hh][0:1, :]
            kd = jnp.concatenate([kf[hh] * jnp.exp(glf - gf[hh]), kb[hh] * jnp.exp(glb - gb[hh])], axis=0)
            res = _dot(jnp.concatenate([aqk_l[hh], kd.T], axis=0), diag2(vn_f, vn_b))
            o_ref[0, rows_f, cs_l[hh]] = wqf[hh][CHUNK:] + res[:CHUNK, :dk]
            o_ref[1, rows_b, cs_l[hh]] = wqb[hh][CHUNK:] + res[:CHUNK, dk:]
            s_ref[hh] = stf[hh] * jnp.exp(glf) + res[CHUNK:, :dk]
            s_ref[hb + hh] = stb[hh] * jnp.exp(glb) + res[CHUNK:, dk:]
        return carry

    lax.fori_loop(0, n_chunks, step, 0)

    for hh in range(hb):
        cs = slice(hh * dk, (hh + 1) * dk)
        o = o_ref[0, :, cs] + o_ref[1, :, cs]
        o = o * lax.rsqrt(jnp.mean(o * o, axis=-1, keepdims=True) + RMS_EPS) * ng_ref[...] * _silu(z_ref[:, cs])
        og_ref[:, cs] = o.astype(og_ref.dtype)
    if emit_state:
        for d in range(2):
            for hh in range(hb):
                sout_ref[d, hh] = s_ref[d * hb + hh]


def _gdn_core(proj, ab, conv_w, alog_row, dtb_row, ng_row, s0, og_prev, state_prev, *, row0, n_seq, t, hb,
              n_heads, state_slot):
    dk = LANES
    wb = hb * dk
    n_hg = n_heads // hb
    rb = row0 // t
    has_s0 = s0 is not None
    emit_state = state_slot is not None

    def col_spec(part):
        return pl.BlockSpec((t, wb), lambda b, h: (rb + b, part * n_hg + h))

    def conv_spec(part):
        return pl.BlockSpec((CONV_K, wb), lambda b, h: (0, part * n_hg + h))

    row_spec = pl.BlockSpec((1, LANES), lambda b, h: (0, 0))
    in_specs = [col_spec(0), col_spec(1), col_spec(2), col_spec(3),
                pl.BlockSpec((t, LANES), lambda b, h: (rb + b, 0)),
                conv_spec(0), conv_spec(1), conv_spec(2), row_spec, row_spec, row_spec]
    args = [proj, proj, proj, proj, ab, conv_w, conv_w, conv_w, alog_row, dtb_row, ng_row]
    if has_s0:
        in_specs.append(pl.BlockSpec((None, 2, hb, dk, dk), lambda b, h: (b, 0, h, 0, 0)))
        args.append(s0)
    out_specs = [pl.BlockSpec((t, wb), lambda b, h: (rb + b, h))]
    out_shape = [jax.ShapeDtypeStruct((proj.shape[0], n_heads * dk), BF16)]
    if emit_state:
        mi, n_mix = state_slot
        out_specs.append(pl.BlockSpec((None, None, 2, hb, dk, dk), lambda b, h: (b, mi, 0, h, 0, 0)))
        out_shape.append(jax.ShapeDtypeStruct((n_seq, n_mix, 2, n_heads, dk, dk), F32))
    aliases = {}
    for out_idx, prev in enumerate((og_prev, state_prev)):
        if prev is not None:
            aliases[len(args)] = out_idx
            in_specs.append(pl.BlockSpec(memory_space=pl.ANY))
            args.append(prev)
    return pl.pallas_call(
        functools.partial(_gdn_kernel, t=t, hb=hb, n_heads=n_heads, has_s0=has_s0, n_alias=len(aliases),
                          emit_state=emit_state),
        grid=(n_seq, n_hg),
        in_specs=in_specs,
        out_specs=out_specs,
        out_shape=out_shape,
        input_output_aliases=aliases,
        scratch_shapes=[
            pltpu.VMEM((t, wb), F32), pltpu.VMEM((t, wb), F32), pltpu.VMEM((t, wb), F32),
            pltpu.VMEM((2 * hb, t, LANES), F32), pltpu.VMEM((2 * hb, t, LANES), F32),
            pltpu.VMEM((2 * hb, dk, dk), F32), pltpu.VMEM((2, t, wb), F32),
            pltpu.VMEM((t + 2 * SUBLANES, wb), F32),
        ],
        compiler_params=_cparams("parallel", "parallel"),
        name="gdn_core",
    )(*args)


def _out_kernel(x_ref, mod_ref, a_ref, w_ref, o_ref):
    y = jnp.dot(a_ref[...], w_ref[...], preferred_element_type=F32)
    o_ref[...] = x_ref[...] + mod_ref[5:6, :] * y


def _gdn_out(x, mods_l, og, w_out, row_of_tile, tm):
    m, d = x.shape
    kdim = og.shape[1]
    return pl.pallas_call(
        _out_kernel,
        grid=(m // tm,),
        in_specs=[
            pl.BlockSpec((tm, d), lambda i: (i, 0)),
            pl.BlockSpec((None, N_MOD, d), lambda i: (row_of_tile(i), 0, 0)),
            pl.BlockSpec((tm, kdim), lambda i: (i, 0)),
            pl.BlockSpec((kdim, d), lambda i: (0, 0)),
        ],
        out_specs=pl.BlockSpec((tm, d), lambda i: (i, 0)),
        out_shape=jax.ShapeDtypeStruct((m, d), F32),
        compiler_params=_cparams("parallel"),
        name="gdn_out",
    )(x, mods_l, og, w_out)


def _pool_kernel(x_ref, mod_ref, g_ref, w_ref, sc_ref, *rest, t):
    o_ref = rest[-1]
    x = x_ref[...]
    rstd = lax.rsqrt(jnp.mean(x * x, axis=-1, keepdims=True) + RMS_EPS)
    pg = w_ref.shape[1]
    edge = max(POOL_WINDOWS)
    zeros = jnp.zeros((edge, pg), F32)
    row = lax.broadcasted_iota(jnp.int32, (t, pg), 0)
    for gi, win in enumerate(POOL_WINDOWS):
        cs = slice(gi * pg, (gi + 1) * pg)
        xg = x_ref[:, cs]
        h = (xg * rstd * g_ref[:, cs]) * (1.0 + mod_ref[4:5, cs]) + mod_ref[3:4, cs]
        half = win // 2
        n = t + 2 * edge
        acc = jnp.concatenate([zeros, h, zeros], axis=0)
        s = 1
        while s < win:
            acc = acc + pltpu.roll(acc, n - s, 0)
            s *= 2
        acc = acc[edge - half:edge - half + t]
        cnt = (jnp.minimum(row + (win - half), t) - jnp.maximum(row - half, 0)).astype(F32)
        diff = (acc / cnt - h).astype(BF16)
        y = jnp.dot(diff, w_ref[gi], preferred_element_type=F32) * sc_ref[:, cs]
        o_ref[:, cs] = xg + mod_ref[5:6, cs] * y


def _pool(x, mods_l, g, w, scale, out_prev, *, row0, n_seq, t, mod_row0, mod_per_seq):
    m, d = x.shape
    ng, pg, _ = w.shape
    assert ng == len(POOL_WINDOWS) and ng * pg == d
    rb = row0 // t
    mod_idx = (lambda b: mod_row0 + b) if mod_per_seq else (lambda b: mod_row0)
    in_specs = [
        pl.BlockSpec((t, d), lambda b: (rb + b, 0)),
        pl.BlockSpec((None, N_MOD, d), lambda b: (mod_idx(b), 0, 0)),
        pl.BlockSpec((1, d), lambda b: (0, 0)),
        pl.BlockSpec((ng, pg, pg), lambda b: (0, 0, 0)),
        pl.BlockSpec((1, d), lambda b: (0, 0)),
    ]
    args = [x, mods_l, g, w, scale]
    aliases = {}
    if out_prev is not None:
        aliases[len(args)] = 0
        in_specs.append(pl.BlockSpec(memory_space=pl.ANY))
        args.append(out_prev)
    return pl.pallas_call(
        functools.partial(_pool_kernel, t=t),
        grid=(n_seq,),
        in_specs=in_specs,
        out_specs=pl.BlockSpec((t, d), lambda b: (rb + b, 0)),
        out_shape=jax.ShapeDtypeStruct((m, d), F32),
        input_output_aliases=aliases,
        compiler_params=_cparams("parallel"),
        name="pool",
    )(*args)


def _final_kernel(x_ref, g_ref, o_ref):
    x = x_ref[...]
    o_ref[...] = x * lax.rsqrt(jnp.mean(x * x, axis=-1, keepdims=True) + RMS_EPS) * g_ref[...]


def _final_norm(x, g, tm, row0, rows):
    _, d = x.shape
    rb = row0 // tm
    return pl.pallas_call(
        _final_kernel,
        grid=(rows // tm,),
        in_specs=[pl.BlockSpec((tm, d), lambda i: (rb + i, 0)), pl.BlockSpec((1, d), lambda i: (0, 0))],
        out_specs=pl.BlockSpec((tm, d), lambda i: (i, 0)),
        out_shape=jax.ShapeDtypeStruct((rows, d), F32),
        compiler_params=_cparams("parallel"),
        name="final_norm",
    )(x, g)


def _place_cols(vals, n_heads):
    row = jnp.zeros((LANES,), F32)
    for d in range(2):
        row = lax.dynamic_update_slice(row, vals[d].astype(F32), (d * 2 * n_heads,))
    return row.reshape(1, LANES)


def kernel(x_prompt, x_sample, state_gdn, c, c_ctx, w_mod, b_mod, norm_g, ffn_w_in, ffn_w_out, gdn_w_in, gdn_conv, gdn_a_log, gdn_dt_bias, gdn_norm_g, gdn_w_out, pool_w, pool_scale, final_g):
    batch, seq, d = x_prompt.shape
    dec_batch, dec_seq, _ = x_sample.shape
    depth = w_mod.shape[0]
    n_heads = gdn_a_log.shape[-1]
    dk = LANES
    qk = n_heads * dk
    f = ffn_w_out.shape[2]
    fp = FF_TILE * ((f + FF_TILE - 1) // FF_TILE)
    m_p = batch * seq
    m_s = dec_batch * dec_seq
    assert 1 + dec_batch <= MOD_ROWS and gdn_w_in.shape[2] == 4 * qk + 4 * n_heads and 4 * n_heads <= LANES
    assert m_p % dec_seq == 0

    def token_tile(candidates):
        return next(c_ for c_ in candidates if m_p % c_ == 0 and dec_seq % c_ == 0)

    def mod_row_of_tile(tile):
        n_prompt_tiles = m_p // tile
        return lambda i: jnp.where(i < n_prompt_tiles, 0, 1 + ((i - n_prompt_tiles) * tile) // dec_seq)

    tm = token_tile((512, 256, 128, 64))
    row_of_tile = mod_row_of_tile(tm)
    tm_proj = token_tile((1024, 512, 256, 128, 64))

    c8 = jnp.zeros((MOD_ROWS, d), F32).at[0].set(c_ctx).at[1:1 + dec_batch].set(c)
    mods = _modulation(c8, w_mod, b_mod).reshape(depth, MOD_ROWS, N_MOD, d)

    x = _token_stream(x_prompt.reshape(m_p, d), x_sample.reshape(m_s, d),
                      _grid_pos_embed(dec_seq, d, x_sample.dtype), tm)

    w_gate_p, w_up_p, w_out_p = _prep_ffn_weights(ffn_w_in, ffn_w_out, fp)

    n_gdn = (depth + 1) // 2
    new_state = None
    for l in range(depth):
        mods_l = mods[l]
        ng = norm_g[l]

        def ffn(x, i, sub):
            return _ffn(x, mods_l, ng[sub:sub + 1], w_gate_p, w_up_p, w_out_p, l, i, row_of_tile, tm, sub)

        x = ffn(x, 0, 0)
        mi = l // 2
        if l % 2 == 0:
            w_in = gdn_w_in[mi]
            w_main = w_in[:, :4 * qk].astype(BF16)
            w_ab = jnp.pad(w_in[:, 4 * qk:], ((0, 0), (0, LANES - 4 * n_heads))).astype(BF16)
            proj, ab = _gdn_proj(x, mods_l, ng[1:2], w_main, w_ab, mod_row_of_tile(tm_proj), tm_proj)
            alog_row = _place_cols(gdn_a_log[mi], n_heads)
            dtb_row = _place_cols(gdn_dt_bias[mi], n_heads)
            ng_row = gdn_norm_g[mi].reshape(1, dk).astype(F32)
            core = functools.partial(_gdn_core, proj, ab, gdn_conv[mi], alog_row, dtb_row, ng_row, n_heads=n_heads)
            og, new_state = core(None, None, new_state, row0=0, n_seq=batch, t=seq, hb=min(16, n_heads),
                                 state_slot=(mi, n_gdn))
            (og,) = core(state_gdn[:, mi], og, None, row0=m_p, n_seq=dec_batch, t=dec_seq, hb=min(4, n_heads),
                         state_slot=None)
            x = _gdn_out(x, mods_l, og, gdn_w_out[mi].astype(BF16), row_of_tile, tm)
        else:
            pw = pool_w[mi].astype(BF16)
            ps = pool_scale[mi].reshape(1, d)
            pool = functools.partial(_pool, x, mods_l, ng[1:2], pw, ps)
            x_new = pool(None, row0=0, n_seq=batch, t=seq, mod_row0=0, mod_per_seq=False)
            x = pool(x_new, row0=m_p, n_seq=dec_batch, t=dec_seq, mod_row0=1, mod_per_seq=True)
        x = ffn(x, 1, 2)

    fg = final_g.reshape(1, d)
    y_prompt = _final_norm(x, fg, tm, 0, m_p).reshape(batch, seq, d)
    y_sample = _final_norm(x, fg, tm, m_p, m_s).reshape(dec_batch, dec_seq, d)
    return (y_prompt, y_sample, new_state.astype(state_gdn.dtype))
```

```python
import functools
import math

import jax
import jax.numpy as jnp
from jax import lax
from jax.experimental import pallas as pl
from jax.experimental.pallas import tpu as pltpu

F32 = jnp.float32
BF16 = jnp.bfloat16

RMS_EPS = 1e-6
L2_EPS = 1e-6
CHUNK = 64
CONV_K = 5
CONV_PAD = CONV_K // 2
POOL_WINDOWS = (2, 4, 8, 16)
GRID_W = 64
POS_BASE = 10000.0
N_MOD = 9
MOD_ROWS = 8
LANES = 128
SUBLANES = 8
FF_TILE = 512
VMEM_LIMIT = 60 * 1024 * 1024


def _cparams(*sem):
    return pltpu.CompilerParams(dimension_semantics=sem, vmem_limit_bytes=VMEM_LIMIT)


def _sigmoid(x):
    return jax.nn.sigmoid(x)


def _silu(x):
    return x * _sigmoid(x)


def _dot(a, b):
    return jnp.dot(a.astype(BF16), b.astype(BF16), preferred_element_type=F32)


def _dot_nt(a, b):
    return lax.dot_general(a.astype(BF16), b.astype(BF16), (((1,), (1,)), ((), ())), preferred_element_type=F32)


def _split(a):
    hi = a.astype(BF16)
    lo = (a - hi.astype(F32)).astype(BF16)
    return hi, lo


def _mod_norm(x, g, scale, shift):
    ms = jnp.mean(x * x, axis=-1, keepdims=True)
    return (x * lax.rsqrt(ms + RMS_EPS) * g) * (1.0 + scale) + shift


def _mod_kernel(c_ref, w_ref, b_ref, o_ref):
    s_hi, s_lo = _split(_silu(c_ref[...]))
    w = w_ref[...].astype(BF16)
    d = functools.partial(jnp.dot, preferred_element_type=F32)
    o_ref[...] = d(s_hi, w) + d(s_lo, w) + b_ref[...]


def _modulation(c8, w_mod, b_mod):
    depth, d, nd = w_mod.shape
    tn = next(c_ for c_ in (2048, 1024, 512, 256, 128) if nd % c_ == 0)
    return pl.pallas_call(
        _mod_kernel,
        grid=(depth, nd // tn),
        in_specs=[
            pl.BlockSpec((MOD_ROWS, d), lambda l, j: (0, 0)),
            pl.BlockSpec((None, d, tn), lambda l, j: (l, 0, j)),
            pl.BlockSpec((None, 1, tn), lambda l, j: (l, 0, j)),
        ],
        out_specs=pl.BlockSpec((None, MOD_ROWS, tn), lambda l, j: (l, 0, j)),
        out_shape=jax.ShapeDtypeStruct((depth, MOD_ROWS, nd), F32),
        compiler_params=_cparams("parallel", "parallel"),
        name="modulation",
    )(c8, w_mod, b_mod.reshape(depth, 1, nd))


def _tokens_kernel(xp_ref, xs_ref, p_ref, o_ref, *, n_prompt_tiles):
    i = pl.program_id(0)

    @pl.when(i < n_prompt_tiles)
    def _():
        o_ref[...] = xp_ref[...]

    @pl.when(i >= n_prompt_tiles)
    def _():
        o_ref[...] = xs_ref[...] + p_ref[...]


def _token_stream(xp, xs, pos, ta):
    m_p, d = xp.shape
    m_s = xs.shape[0]
    t = pos.shape[0]
    n_p = m_p // ta
    return pl.pallas_call(
        functools.partial(_tokens_kernel, n_prompt_tiles=n_p),
        grid=((m_p + m_s) // ta,),
        in_specs=[
            pl.BlockSpec((ta, d), lambda i: (jnp.minimum(i, n_p - 1), 0)),
            pl.BlockSpec((ta, d), lambda i: (jnp.maximum(i - n_p, 0), 0)),
            pl.BlockSpec((ta, d), lambda i: ((jnp.maximum(i - n_p, 0) * ta % t) // ta, 0)),
        ],
        out_specs=pl.BlockSpec((ta, d), lambda i: (i, 0)),
        out_shape=jax.ShapeDtypeStruct((m_p + m_s, d), xp.dtype),
        compiler_params=_cparams("parallel"),
        name="token_stream",
    )(xp, xs, pos)


def _grid_pos_embed(n_tok, d, dtype):
    idx = jnp.arange(n_tok)
    r = (idx // GRID_W).astype(F32)
    col = (idx % GRID_W).astype(F32)
    n_freq = d // 4
    freqs = jnp.exp(-math.log(POS_BASE) * jnp.arange(n_freq, dtype=F32) / n_freq)
    ar = r[:, None] * freqs
    ac = col[:, None] * freqs
    return jnp.concatenate([jnp.sin(ar), jnp.cos(ar), jnp.sin(ac), jnp.cos(ac)], axis=-1).astype(dtype)


def _prep_in_kernel(w_ref, g_ref, u_ref, *, f):
    pad = jnp.zeros((g_ref.shape[0], g_ref.shape[1] - f), BF16)
    g_ref[:, :f] = w_ref[:, :f].astype(BF16)
    u_ref[:, :f] = w_ref[:, f:].astype(BF16)
    g_ref[:, f:] = pad
    u_ref[:, f:] = pad


def _prep_out_kernel(w_ref, o_ref, *, f):
    row = pl.program_id(2) * FF_TILE + lax.broadcasted_iota(jnp.int32, w_ref.shape, 0)
    o_ref[...] = jnp.where(row < f, w_ref[...], 0.0).astype(BF16)


def _prep_ffn_weights(w_in, w_out, fp):
    depth, two, d, f2 = w_in.shape
    f = f2 // 2
    assert f % LANES == 0
    rows = 256
    spec_w = pl.BlockSpec((None, None, rows, f2), lambda l, k, r: (l, k, r, 0))
    spec_p = pl.BlockSpec((None, None, rows, fp), lambda l, k, r: (l, k, r, 0))
    padded = jax.ShapeDtypeStruct((depth, two, d, fp), BF16)
    w_gate_p, w_up_p = pl.pallas_call(
        functools.partial(_prep_in_kernel, f=f),
        grid=(depth, two, d // rows),
        in_specs=[spec_w],
        out_specs=[spec_p, spec_p],
        out_shape=[padded, padded],
        compiler_params=_cparams("parallel", "parallel", "parallel"),
        name="prep_w_in",
    )(w_in)
    w_out_p = pl.pallas_call(
        functools.partial(_prep_out_kernel, f=f),
        grid=(depth, two, fp // FF_TILE),
        in_specs=[pl.BlockSpec((None, None, FF_TILE, d), lambda l, k, j: (l, k, j, 0))],
        out_specs=pl.BlockSpec((None, None, FF_TILE, d), lambda l, k, j: (l, k, j, 0)),
        out_shape=jax.ShapeDtypeStruct((depth, two, fp, d), BF16),
        compiler_params=_cparams("parallel", "parallel", "parallel"),
        name="prep_w_out",
    )(w_out)
    return w_gate_p, w_up_p, w_out_p


def _ffn_kernel(x_ref, mod_ref, g_ref, wg_ref, wu_ref, wo_ref, o_ref, h_ref, *, sub):
    j = pl.program_id(1)
    last = pl.num_programs(1) - 1

    def partial_out(h):
        gate = jnp.dot(h, wg_ref[...], preferred_element_type=F32)
        up = jnp.dot(h, wu_ref[...], preferred_element_type=F32)
        act = (_silu(gate) * up).astype(BF16)
        return jnp.dot(act, wo_ref[...], preferred_element_type=F32)

    @pl.when(j == 0)
    def _():
        h = _mod_norm(x_ref[...], g_ref[...], mod_ref[3 * sub + 1:3 * sub + 2, :], mod_ref[3 * sub:3 * sub + 1, :])
        h = h.astype(BF16)
        h_ref[...] = h
        o_ref[...] = partial_out(h)

    @pl.when((j > 0) & (j < last))
    def _():
        o_ref[...] += partial_out(h_ref[...])

    @pl.when(j == last)
    def _():
        y = o_ref[...] + partial_out(h_ref[...])
        o_ref[...] = x_ref[...] + (0.5 * mod_ref[3 * sub + 2:3 * sub + 3, :]) * y


def _ffn(x, mods_l, g, w_gate_p, w_up_p, w_out_p, l, k, row_of_tile, tm, sub):
    m, d = x.shape
    fp = w_out_p.shape[2]
    nj = fp // FF_TILE
    assert nj >= 2
    return pl.pallas_call(
        functools.partial(_ffn_kernel, sub=sub),
        grid=(m // tm, nj),
        in_specs=[
            pl.BlockSpec((tm, d), lambda i, j: (i, 0)),
            pl.BlockSpec((None, N_MOD, d), lambda i, j: (row_of_tile(i), 0, 0)),
            pl.BlockSpec((1, d), lambda i, j: (0, 0)),
            pl.BlockSpec((None, None, d, FF_TILE), lambda i, j: (l, k, 0, j)),
            pl.BlockSpec((None, None, d, FF_TILE), lambda i, j: (l, k, 0, j)),
            pl.BlockSpec((None, None, FF_TILE, d), lambda i, j: (l, k, j, 0)),
        ],
        out_specs=pl.BlockSpec((tm, d), lambda i, j: (i, 0)),
        out_shape=jax.ShapeDtypeStruct((m, d), F32),
        scratch_shapes=[pltpu.VMEM((tm, d), BF16)],
        compiler_params=_cparams("parallel", "arbitrary"),
        name="ffn",
    )(x, mods_l, g, w_gate_p, w_up_p, w_out_p)


def _proj_kernel(x_ref, mod_ref, g_ref, w_ref, wab_ref, o_ref, ab_ref, h_ref):
    j = pl.program_id(1)

    @pl.when(j == 0)
    def _():
        h = _mod_norm(x_ref[...], g_ref[...], mod_ref[4:5, :], mod_ref[3:4, :]).astype(BF16)
        h_ref[...] = h
        o_ref[...] = jnp.dot(h, w_ref[...], preferred_element_type=F32)
        ab_ref[...] = jnp.dot(h, wab_ref[...], preferred_element_type=F32)

    @pl.when(j > 0)
    def _():
        o_ref[...] = jnp.dot(h_ref[...], w_ref[...], preferred_element_type=F32)


def _gdn_proj(x, mods_l, g, w_main, w_ab, row_of_tile, tm):
    m, d = x.shape
    n = w_main.shape[1]
    tn = 1024
    return pl.pallas_call(
        _proj_kernel,
        grid=(m // tm, n // tn),
        in_specs=[
            pl.BlockSpec((tm, d), lambda i, j: (i, 0)),
            pl.BlockSpec((None, N_MOD, d), lambda i, j: (row_of_tile(i), 0, 0)),
            pl.BlockSpec((1, d), lambda i, j: (0, 0)),
            pl.BlockSpec((d, tn), lambda i, j: (0, j)),
            pl.BlockSpec((d, LANES), lambda i, j: (0, 0)),
        ],
        out_specs=[pl.BlockSpec((tm, tn), lambda i, j: (i, j)), pl.BlockSpec((tm, LANES), lambda i, j: (i, 0))],
        out_shape=[jax.ShapeDtypeStruct((m, n), F32), jax.ShapeDtypeStruct((m, LANES), F32)],
        scratch_shapes=[pltpu.VMEM((tm, d), BF16)],
        compiler_params=_cparams("parallel", "arbitrary"),
        name="gdn_proj",
    )(x, mods_l, g, w_main, w_ab)


def _row_sum_lanes(x):
    return jnp.dot(x.astype(BF16), jnp.ones((LANES, LANES), BF16), preferred_element_type=F32)


def _conv_silu(x_ref, w_ref, o_ref, pad_ref, t):
    zeros = jnp.zeros((SUBLANES, x_ref.shape[1]), F32)
    pad_ref[0:SUBLANES, :] = zeros
    pad_ref[t + SUBLANES:t + 2 * SUBLANES, :] = zeros
    pad_ref[SUBLANES:t + SUBLANES, :] = x_ref[...]
    acc = x_ref[...] * w_ref[CONV_PAD:CONV_PAD + 1, :]
    for j in range(CONV_K):
        if j != CONV_PAD:
            r0 = SUBLANES + j - CONV_PAD
            acc = acc + pad_ref[r0:r0 + t, :] * w_ref[j:j + 1, :]
    o_ref[...] = _silu(acc)


def _chunk_cumsum(g, t, reverse):
    pos = lax.broadcasted_iota(jnp.int32, g.shape, 0) % CHUNK
    s = 1
    while s < CHUNK:
        if reverse:
            g = g + jnp.where(pos < CHUNK - s, pltpu.roll(g, t - s, 0), 0.0)
        else:
            g = g + jnp.where(pos >= s, pltpu.roll(g, s, 0), 0.0)
        s *= 2
    return g


def _gdn_kernel(*refs, t, hb, n_heads, has_s0, n_alias, emit_state):
    (q_ref, k_ref, v_ref, z_ref, ab_ref, cq_ref, ck_ref, cv_ref, alog_ref, dtb_ref, ng_ref), refs = refs[:11], refs[11:]
    if has_s0:
        s0_ref, refs = refs[0], refs[1:]
    refs = refs[n_alias:]
    og_ref, refs = refs[0], refs[1:]
    if emit_state:
        sout_ref, refs = refs[0], refs[1:]
    qn_ref, kn_ref, vc_ref, gam_ref, beta_ref, s_ref, o_ref, pad_ref = refs

    dk = LANES
    n_chunks = t // CHUNK
    hg = pl.program_id(1)

    _conv_silu(q_ref, cq_ref, qn_ref, pad_ref, t)
    _conv_silu(k_ref, ck_ref, kn_ref, pad_ref, t)
    _conv_silu(v_ref, cv_ref, vc_ref, pad_ref, t)
    for hh in range(hb):
        cs = slice(hh * dk, (hh + 1) * dk)
        qh = qn_ref[:, cs]
        kh = kn_ref[:, cs]
        qn_ref[:, cs] = qh * lax.rsqrt(_row_sum_lanes(qh * qh) + L2_EPS) * (dk ** -0.5)
        kn_ref[:, cs] = kh * lax.rsqrt(_row_sum_lanes(kh * kh) + L2_EPS)

    ab = ab_ref[...]
    pre = ab + dtb_ref[...]
    softplus = jnp.maximum(pre, 0.0) + jnp.log1p(jnp.exp(-jnp.abs(pre)))
    g_all = -jnp.exp(alog_ref[...]) * softplus
    beta_all = _sigmoid(ab)
    cum = (_chunk_cumsum(g_all, t, False), _chunk_cumsum(g_all, t, True))
    lane = lax.broadcasted_iota(jnp.int32, ab.shape, 1)
    for d in range(2):
        for hh in range(hb):
            col = d * 2 * n_heads + hg * hb + hh
            gsel = jnp.sum(jnp.where(lane == col, cum[d], 0.0), axis=-1, keepdims=True)
            bsel = jnp.sum(jnp.where(lane == col + n_heads, beta_all, 0.0), axis=-1, keepdims=True)
            gam_ref[d * hb + hh] = jnp.broadcast_to(gsel, ab.shape)
            beta_ref[d * hb + hh] = jnp.broadcast_to(bsel, ab.shape)
            s_ref[d * hb + hh] = s0_ref[d, hh] if has_s0 else jnp.zeros((dk, dk), F32)
    ii = lax.broadcasted_iota(jnp.int32, (CHUNK, LANES), 0)
    lane = lax.broadcasted_iota(jnp.int32, (CHUNK, LANES), 1)
    is_f = lane < CHUNK
    jj = jnp.where(is_f, lane, lane - CHUNK)
    ahead = jnp.where(is_f, ii - jj, jj - ii)
    incl = ahead >= 0
    strict = ahead > 0
    eye = (ii == jj).astype(F32)
    pair_masks = []
    s_blk = 1
    while s_blk < CHUNK:
        pair_masks.append(((ii // (2 * s_blk)) == (jj // (2 * s_blk))) & ((ii // s_blk) != (jj // s_blk)))
        s_blk *= 2

    def block_diag(m):
        zero = jnp.zeros_like(m)
        return jnp.concatenate([jnp.where(is_f, m, zero), jnp.where(is_f, zero, m)], axis=0)

    def diag2(top, bottom):
        return jnp.concatenate([jnp.concatenate([top, jnp.zeros_like(bottom)], axis=1),
                                jnp.concatenate([jnp.zeros_like(top), bottom], axis=1)], axis=0)

    def step(s, carry):
        rows_f = pl.ds(pl.multiple_of(s * CHUNK, CHUNK), CHUNK)
        rows_b = pl.ds(pl.multiple_of((n_chunks - 1 - s) * CHUNK, CHUNK), CHUNK)
        heads = range(hb)
        cs_l = [slice(hh * dk, (hh + 1) * dk) for hh in heads]
        qf = [qn_ref[rows_f, cs] for cs in cs_l]
        kf = [kn_ref[rows_f, cs] for cs in cs_l]
        vf = [vc_ref[rows_f, cs] for cs in cs_l]
        qb = [qn_ref[rows_b, cs] for cs in cs_l]
        kb = [kn_ref[rows_b, cs] for cs in cs_l]
        vb = [vc_ref[rows_b, cs] for cs in cs_l]
        gf = [gam_ref[hh, rows_f, :] for hh in heads]
        gb = [gam_ref[hb + hh, rows_b, :] for hh in heads]
        bf = [beta_ref[hh, rows_f, :] for hh in heads]
        bb = [beta_ref[hb + hh, rows_b, :] for hh in heads]

        a_l, x_l, aqk_l = [], [], []
        for hh in heads:
            gr = jnp.concatenate([gf[hh], gb[hh]], axis=0).T[:CHUNK, :]
            diff = jnp.where(is_f, gf[hh], gb[hh]) - gr
            dec_incl = jnp.where(incl, jnp.exp(jnp.where(incl, diff, 0.0)), 0.0)
            lhs = jnp.concatenate([jnp.concatenate([kf[hh], kb[hh]], axis=1),
                                   jnp.concatenate([qf[hh], qb[hh]], axis=1)], axis=0)
            kq = _dot_nt(lhs, diag2(kf[hh], kb[hh]))
            a = jnp.where(is_f, bf[hh], bb[hh]) * kq[:CHUNK] * jnp.where(strict, dec_incl, 0.0)
            a = a.astype(BF16).astype(F32)
            a_l.append(a)
            aqk_l.append(kq[CHUNK:] * dec_incl)
            x_l.append(eye - jnp.where(pair_masks[0], a, 0.0))
        for pm in pair_masks[1:]:
            t_l = [_dot(x, block_diag(jnp.where(pm, a, 0.0))) for x, a in zip(x_l, a_l)]
            x_l = [x - _dot(tx, block_diag(x)) for x, tx in zip(x_l, t_l)]
        egf = [jnp.exp(g) for g in gf]
        egb = [jnp.exp(g) for g in gb]
        wu_l = []
        for hh in heads:
            r_f = jnp.concatenate([kf[hh] * (bf[hh] * egf[hh]), vf[hh] * bf[hh]], axis=1).astype(BF16)
            r_b = jnp.concatenate([kb[hh] * (bb[hh] * egb[hh]), vb[hh] * bb[hh]], axis=1).astype(BF16)
            wu_l.append(_dot(x_l[hh], diag2(r_f, r_b)))
        stf = [s_ref[hh] for hh in heads]
        stb = [s_ref[hb + hh] for hh in heads]
        wqf = [_dot(jnp.concatenate([wu[:, :dk], q * eg], axis=0), st) for wu, q, eg, st in zip(wu_l, qf, egf, stf)]
        wqb = [_dot(jnp.concatenate([wu[:, 2 * dk:3 * dk], q * eg], axis=0), st)
               for wu, q, eg, st in zip(wu_l, qb, egb, stb)]
        for hh in heads:
            vn_f = wu_l[hh][:, dk:2 * dk] - wqf[hh][:CHUNK]
            vn_b = wu_l[hh][:, 3 * dk:] - wqb[hh][:CHUNK]
            glf = gf[hh][CHUNK - 1:CHUNK, :]
            glb = gb[hh][0:1, :]
            kd = jnp.concatenate([kf[hh] * jnp.exp(glf - gf[hh]), kb[hh] * jnp.exp(glb - gb[hh])], axis=0)
            res = _dot(jnp.concatenate([aqk_l[hh], kd.T], axis=0), diag2(vn_f, vn_b))
            o_ref[0, rows_f, cs_l[hh]] = wqf[hh][CHUNK:] + res[:CHUNK, :dk]
            o_ref[1, rows_b, cs_l[hh]] = wqb[hh][CHUNK:] + res[:CHUNK, dk:]
            s_ref[hh] = stf[hh] * jnp.exp(glf) + res[CHUNK:, :dk]
            s_ref[hb + hh] = stb[hh] * jnp.exp(glb) + res[CHUNK:, dk:]
        return carry

    lax.fori_loop(0, n_chunks, step, 0)

    for hh in range(hb):
        cs = slice(hh * dk, (hh + 1) * dk)
        o = o_ref[0, :, cs] + o_ref[1, :, cs]
        o = o * lax.rsqrt(jnp.mean(o * o, axis=-1, keepdims=True) + RMS_EPS) * ng_ref[...] * _silu(z_ref[:, cs])
        og_ref[:, cs] = o.astype(og_ref.dtype)
    if emit_state:
        for d in range(2):
            for hh in range(hb):
                sout_ref[d, hh] = s_ref[d * hb + hh]


def _gdn_core(proj, ab, conv_w, alog_row, dtb_row, ng_row, s0, og_prev, state_prev, *, row0, n_seq, t, hb,
              n_heads, state_slot):
    dk = LANES
    wb = hb * dk
    n_hg = n_heads // hb
    rb = row0 // t
    has_s0 = s0 is not None
    emit_state = state_slot is not None

    def col_spec(part):
        return pl.BlockSpec((t, wb), lambda b, h: (rb + b, part * n_hg + h))

    def conv_spec(part):
        return pl.BlockSpec((CONV_K, wb), lambda b, h: (0, part * n_hg + h))

    row_spec = pl.BlockSpec((1, LANES), lambda b, h: (0, 0))
    in_specs = [col_spec(0), col_spec(1), col_spec(2), col_spec(3),
                pl.BlockSpec((t, LANES), lambda b, h: (rb + b, 0)),
                conv_spec(0), conv_spec(1), conv_spec(2), row_spec, row_spec, row_spec]
    args = [proj, proj, proj, proj, ab, conv_w, conv_w, conv_w, alog_row, dtb_row, ng_row]
    if has_s0:
        in_specs.append(pl.BlockSpec((None, 2, hb, dk, dk), lambda b, h: (b, 0, h, 0, 0)))
        args.append(s0)
    out_specs = [pl.BlockSpec((t, wb), lambda b, h: (rb + b, h))]
    out_shape = [jax.ShapeDtypeStruct((proj.shape[0], n_heads * dk), BF16)]
    if emit_state:
        mi, n_mix = state_slot
        out_specs.append(pl.BlockSpec((None, None, 2, hb, dk, dk), lambda b, h: (b, mi, 0, h, 0, 0)))
        out_shape.append(jax.ShapeDtypeStruct((n_seq, n_mix, 2, n_heads, dk, dk), F32))
    aliases = {}
    for out_idx, prev in enumerate((og_prev, state_prev)):
        if prev is not None:
            aliases[len(args)] = out_idx
            in_specs.append(pl.BlockSpec(memory_space=pl.ANY))
            args.append(prev)
    return pl.pallas_call(
        functools.partial(_gdn_kernel, t=t, hb=hb, n_heads=n_heads, has_s0=has_s0, n_alias=len(aliases),
                          emit_state=emit_state),
        grid=(n_seq, n_hg),
        in_specs=in_specs,
        out_specs=out_specs,
        out_shape=out_shape,
        input_output_aliases=aliases,
        scratch_shapes=[
            pltpu.VMEM((t, wb), F32), pltpu.VMEM((t, wb), F32), pltpu.VMEM((t, wb), F32),
            pltpu.VMEM((2 * hb, t, LANES), F32), pltpu.VMEM((2 * hb, t, LANES), F32),
            pltpu.VMEM((2 * hb, dk, dk), F32), pltpu.VMEM((2, t, wb), F32),
            pltpu.VMEM((t + 2 * SUBLANES, wb), F32),
        ],
        compiler_params=_cparams("parallel", "parallel"),
        name="gdn_core",
    )(*args)


def _out_kernel(x_ref, mod_ref, a_ref, w_ref, o_ref):
    y = jnp.dot(a_ref[...], w_ref[...], preferred_element_type=F32)
    o_ref[...] = x_ref[...] + mod_ref[5:6, :] * y


def _gdn_out(x, mods_l, og, w_out, row_of_tile, tm):
    m, d = x.shape
    kdim = og.shape[1]
    return pl.pallas_call(
        _out_kernel,
        grid=(m // tm,),
        in_specs=[
            pl.BlockSpec((tm, d), lambda i: (i, 0)),
            pl.BlockSpec((None, N_MOD, d), lambda i: (row_of_tile(i), 0, 0)),
            pl.BlockSpec((tm, kdim), lambda i: (i, 0)),
            pl.BlockSpec((kdim, d), lambda i: (0, 0)),
        ],
        out_specs=pl.BlockSpec((tm, d), lambda i: (i, 0)),
        out_shape=jax.ShapeDtypeStruct((m, d), F32),
        compiler_params=_cparams("parallel"),
        name="gdn_out",
    )(x, mods_l, og, w_out)


def _pool_kernel(x_ref, mod_ref, g_ref, w_ref, sc_ref, *rest, t):
    o_ref = rest[-1]
    x = x_ref[...]
    rstd = lax.rsqrt(jnp.mean(x * x, axis=-1, keepdims=True) + RMS_EPS)
    pg = w_ref.shape[1]
    edge = max(POOL_WINDOWS)
    zeros = jnp.zeros((edge, pg), F32)
    row = lax.broadcasted_iota(jnp.int32, (t, pg), 0)
    for gi, win in enumerate(POOL_WINDOWS):
        cs = slice(gi * pg, (gi + 1) * pg)
        xg = x_ref[:, cs]
        h = (xg * rstd * g_ref[:, cs]) * (1.0 + mod_ref[4:5, cs]) + mod_ref[3:4, cs]
        half = win // 2
        n = t + 2 * edge
        acc = jnp.concatenate([zeros, h, zeros], axis=0)
        s = 1
        while s < win:
            acc = acc + pltpu.roll(acc, n - s, 0)
            s *= 2
        acc = acc[edge - half:edge - half + t]
        cnt = (jnp.minimum(row + (win - half), t) - jnp.maximum(row - half, 0)).astype(F32)
        diff = (acc / cnt - h).astype(BF16)
        y = jnp.dot(diff, w_ref[gi], preferred_element_type=F32) * sc_ref[:, cs]
        o_ref[:, cs] = xg + mod_ref[5:6, cs] * y


def _pool(x, mods_l, g, w, scale, out_prev, *, row0, n_seq, t, mod_row0, mod_per_seq):
    m, d = x.shape
    ng, pg, _ = w.shape
    assert ng == len(POOL_WINDOWS) and ng * pg == d
    rb = row0 // t
    mod_idx = (lambda b: mod_row0 + b) if mod_per_seq else (lambda b: mod_row0)
    in_specs = [
        pl.BlockSpec((t, d), lambda b: (rb + b, 0)),
        pl.BlockSpec((None, N_MOD, d), lambda b: (mod_idx(b), 0, 0)),
        pl.BlockSpec((1, d), lambda b: (0, 0)),
        pl.BlockSpec((ng, pg, pg), lambda b: (0, 0, 0)),
        pl.BlockSpec((1, d), lambda b: (0, 0)),
    ]
    args = [x, mods_l, g, w, scale]
    aliases = {}
    if out_prev is not None:
        aliases[len(args)] = 0
        in_specs.append(pl.BlockSpec(memory_space=pl.ANY))
        args.append(out_prev)
    return pl.pallas_call(
        functools.partial(_pool_kernel, t=t),
        grid=(n_seq,),
        in_specs=in_specs,
        out_specs=pl.BlockSpec((t, d), lambda b: (rb + b, 0)),
        out_shape=jax.ShapeDtypeStruct((m, d), F32),
        input_output_aliases=aliases,
        compiler_params=_cparams("parallel"),
        name="pool",
    )(*args)


def _final_kernel(x_ref, g_ref, o_ref):
    x = x_ref[...]
    o_ref[...] = x * lax.rsqrt(jnp.mean(x * x, axis=-1, keepdims=True) + RMS_EPS) * g_ref[...]


def _final_norm(x, g, tm, row0, rows):
    _, d = x.shape
    rb = row0 // tm
    return pl.pallas_call(
        _final_kernel,
        grid=(rows // tm,),
        in_specs=[pl.BlockSpec((tm, d), lambda i: (rb + i, 0)), pl.BlockSpec((1, d), lambda i: (0, 0))],
        out_specs=pl.BlockSpec((tm, d), lambda i: (i, 0)),
        out_shape=jax.ShapeDtypeStruct((rows, d), F32),
        compiler_params=_cparams("parallel"),
        name="final_norm",
    )(x, g)


def _place_cols(vals, n_heads):
    row = jnp.zeros((LANES,), F32)
    for d in range(2):
        row = lax.dynamic_update_slice(row, vals[d].astype(F32), (d * 2 * n_heads,))
    return row.reshape(1, LANES)


def kernel(x_prompt, x_sample, state_gdn, c, c_ctx, w_mod, b_mod, norm_g, ffn_w_in, ffn_w_out, gdn_w_in, gdn_conv, gdn_a_log, gdn_dt_bias, gdn_norm_g, gdn_w_out, pool_w, pool_scale, final_g):
    batch, seq, d = x_prompt.shape
    dec_batch, dec_seq, _ = x_sample.shape
    depth = w_mod.shape[0]
    n_heads = gdn_a_log.shape[-1]
    dk = LANES
    qk = n_heads * dk
    f = ffn_w_out.shape[2]
    fp = FF_TILE * ((f + FF_TILE - 1) // FF_TILE)
    m_p = batch * seq
    m_s = dec_batch * dec_seq
    assert 1 + dec_batch <= MOD_ROWS and gdn_w_in.shape[2] == 4 * qk + 4 * n_heads and 4 * n_heads <= LANES
    assert m_p % dec_seq == 0

    def token_tile(candidates):
        return next(c_ for c_ in candidates if m_p % c_ == 0 and dec_seq % c_ == 0)

    def mod_row_of_tile(tile):
        n_prompt_tiles = m_p // tile
        return lambda i: jnp.where(i < n_prompt_tiles, 0, 1 + ((i - n_prompt_tiles) * tile) // dec_seq)

    tm = token_tile((512, 256, 128, 64))
    row_of_tile = mod_row_of_tile(tm)
    tm_proj = token_tile((1024, 512, 256, 128, 64))

    c8 = jnp.zeros((MOD_ROWS, d), F32).at[0].set(c_ctx).at[1:1 + dec_batch].set(c)
    mods = _modulation(c8, w_mod, b_mod).reshape(depth, MOD_ROWS, N_MOD, d)

    x = _token_stream(x_prompt.reshape(m_p, d), x_sample.reshape(m_s, d),
                      _grid_pos_embed(dec_seq, d, x_sample.dtype), tm)

    w_gate_p, w_up_p, w_out_p = _prep_ffn_weights(ffn_w_in, ffn_w_out, fp)

    n_gdn = (depth + 1) // 2
    new_state = None
    for l in range(depth):
        mods_l = mods[l]
        ng = norm_g[l]

        def ffn(x, i, sub):
            return _ffn(x, mods_l, ng[sub:sub + 1], w_gate_p, w_up_p, w_out_p, l, i, mod_row_of_tile(tm_proj),
                        tm_proj, sub)

        x = ffn(x, 0, 0)
        mi = l // 2
        if l % 2 == 0:
            w_in = gdn_w_in[mi]
            w_main = w_in[:, :4 * qk].astype(BF16)
            w_ab = jnp.pad(w_in[:, 4 * qk:], ((0, 0), (0, LANES - 4 * n_heads))).astype(BF16)
            proj, ab = _gdn_proj(x, mods_l, ng[1:2], w_main, w_ab, mod_row_of_tile(tm_proj), tm_proj)
            alog_row = _place_cols(gdn_a_log[mi], n_heads)
            dtb_row = _place_cols(gdn_dt_bias[mi], n_heads)
            ng_row = gdn_norm_g[mi].reshape(1, dk).astype(F32)
            core = functools.partial(_gdn_core, proj, ab, gdn_conv[mi], alog_row, dtb_row, ng_row, n_heads=n_heads)
            og, new_state = core(None, None, new_state, row0=0, n_seq=batch, t=seq, hb=min(16, n_heads),
                                 state_slot=(mi, n_gdn))
            (og,) = core(state_gdn[:, mi], og, None, row0=m_p, n_seq=dec_batch, t=dec_seq, hb=min(4, n_heads),
                         state_slot=None)
            x = _gdn_out(x, mods_l, og, gdn_w_out[mi].astype(BF16), row_of_tile, tm)
        else:
            pw = pool_w[mi].astype(BF16)
            ps = pool_scale[mi].reshape(1, d)
            pool = functools.partial(_pool, x, mods_l, ng[1:2], pw, ps)
            x_new = pool(None, row0=0, n_seq=batch, t=seq, mod_row0=0, mod_per_seq=False)
            x = pool(x_new, row0=m_p, n_seq=dec_batch, t=dec_seq, mod_row0=1, mod_per_seq=True)
        x = ffn(x, 1, 2)

    fg = final_g.reshape(1, d)
    y_prompt = _final_norm(x, fg, tm, 0, m_p).reshape(batch, seq, d)
    y_sample = _final_norm(x, fg, tm, m_p, m_s).reshape(dec_batch, dec_seq, d)
    return (y_prompt, y_sample, new_state.astype(state_gdn.dtype))
```

```python
import functools
import math

import jax
import jax.numpy as jnp
from jax import lax
from jax.experimental import pallas as pl
from jax.experimental.pallas import tpu as pltpu

F32 = jnp.float32
BF16 = jnp.bfloat16

RMS_EPS = 1e-6
L2_EPS = 1e-6
CHUNK = 64
CONV_K = 5
CONV_PAD = CONV_K // 2
POOL_WINDOWS = (2, 4, 8, 16)
GRID_W = 64
POS_BASE = 10000.0
N_MOD = 9
MOD_ROWS = 8
LANES = 128
SUBLANES = 8
FF_TILE = 512
UNROLL_CHUNKS = 4
VMEM_LIMIT = 60 * 1024 * 1024


def _cparams(*sem):
    return pltpu.CompilerParams(dimension_semantics=sem, vmem_limit_bytes=VMEM_LIMIT)


def _sigmoid(x):
    return jax.nn.sigmoid(x)


def _silu(x):
    return x * _sigmoid(x)


def _dot(a, b):
    return jnp.dot(a.astype(BF16), b.astype(BF16), preferred_element_type=F32)


def _dot_nt(a, b):
    return lax.dot_general(a.astype(BF16), b.astype(BF16), (((1,), (1,)), ((), ())), preferred_element_type=F32)


def _split(a):
    hi = a.astype(BF16)
    lo = (a - hi.astype(F32)).astype(BF16)
    return hi, lo


def _mod_norm(x, g, scale, shift):
    ms = jnp.mean(x * x, axis=-1, keepdims=True)
    return (x * lax.rsqrt(ms + RMS_EPS) * g) * (1.0 + scale) + shift


def _mod_kernel(c_ref, w_ref, b_ref, o_ref):
    s_hi, s_lo = _split(_silu(c_ref[...]))
    w = w_ref[...].astype(BF16)
    d = functools.partial(jnp.dot, preferred_element_type=F32)
    o_ref[...] = d(s_hi, w) + d(s_lo, w) + b_ref[...]


def _modulation(c8, w_mod, b_mod):
    depth, d, nd = w_mod.shape
    tn = next(c_ for c_ in (2048, 1024, 512, 256, 128) if nd % c_ == 0)
    return pl.pallas_call(
        _mod_kernel,
        grid=(depth, nd // tn),
        in_specs=[
            pl.BlockSpec((MOD_ROWS, d), lambda l, j: (0, 0)),
            pl.BlockSpec((None, d, tn), lambda l, j: (l, 0, j)),
            pl.BlockSpec((None, 1, tn), lambda l, j: (l, 0, j)),
        ],
        out_specs=pl.BlockSpec((None, MOD_ROWS, tn), lambda l, j: (l, 0, j)),
        out_shape=jax.ShapeDtypeStruct((depth, MOD_ROWS, nd), F32),
        compiler_params=_cparams("parallel", "parallel"),
        name="modulation",
    )(c8, w_mod, b_mod.reshape(depth, 1, nd))


def _tokens_kernel(xp_ref, xs_ref, p_ref, o_ref, *, n_prompt_tiles):
    i = pl.program_id(0)

    @pl.when(i < n_prompt_tiles)
    def _():
        o_ref[...] = xp_ref[...]

    @pl.when(i >= n_prompt_tiles)
    def _():
        o_ref[...] = xs_ref[...] + p_ref[...]


def _token_stream(xp, xs, pos, ta):
    m_p, d = xp.shape
    m_s = xs.shape[0]
    t = pos.shape[0]
    n_p = m_p // ta
    return pl.pallas_call(
        functools.partial(_tokens_kernel, n_prompt_tiles=n_p),
        grid=((m_p + m_s) // ta,),
        in_specs=[
            pl.BlockSpec((ta, d), lambda i: (jnp.minimum(i, n_p - 1), 0)),
            pl.BlockSpec((ta, d), lambda i: (jnp.maximum(i - n_p, 0), 0)),
            pl.BlockSpec((ta, d), lambda i: ((jnp.maximum(i - n_p, 0) * ta % t) // ta, 0)),
        ],
        out_specs=pl.BlockSpec((ta, d), lambda i: (i, 0)),
        out_shape=jax.ShapeDtypeStruct((m_p + m_s, d), xp.dtype),
        compiler_params=_cparams("parallel"),
        name="token_stream",
    )(xp, xs, pos)


def _grid_pos_embed(n_tok, d, dtype):
    idx = jnp.arange(n_tok)
    r = (idx // GRID_W).astype(F32)
    col = (idx % GRID_W).astype(F32)
    n_freq = d // 4
    freqs = jnp.exp(-math.log(POS_BASE) * jnp.arange(n_freq, dtype=F32) / n_freq)
    ar = r[:, None] * freqs
    ac = col[:, None] * freqs
    return jnp.concatenate([jnp.sin(ar), jnp.cos(ar), jnp.sin(ac), jnp.cos(ac)], axis=-1).astype(dtype)


def _prep_in_kernel(w_ref, g_ref, u_ref, *, f):
    pad = jnp.zeros((g_ref.shape[0], g_ref.shape[1] - f), BF16)
    g_ref[:, :f] = w_ref[:, :f].astype(BF16)
    u_ref[:, :f] = w_ref[:, f:].astype(BF16)
    g_ref[:, f:] = pad
    u_ref[:, f:] = pad


def _prep_out_kernel(w_ref, o_ref, *, f):
    row = pl.program_id(2) * FF_TILE + lax.broadcasted_iota(jnp.int32, w_ref.shape, 0)
    o_ref[...] = jnp.where(row < f, w_ref[...], 0.0).astype(BF16)


def _prep_ffn_weights(w_in, w_out, fp):
    depth, two, d, f2 = w_in.shape
    f = f2 // 2
    assert f % LANES == 0
    rows = 256
    spec_w = pl.BlockSpec((None, None, rows, f2), lambda l, k, r: (l, k, r, 0))
    spec_p = pl.BlockSpec((None, None, rows, fp), lambda l, k, r: (l, k, r, 0))
    padded = jax.ShapeDtypeStruct((depth, two, d, fp), BF16)
    w_gate_p, w_up_p = pl.pallas_call(
        functools.partial(_prep_in_kernel, f=f),
        grid=(depth, two, d // rows),
        in_specs=[spec_w],
        out_specs=[spec_p, spec_p],
        out_shape=[padded, padded],
        compiler_params=_cparams("parallel", "parallel", "parallel"),
        name="prep_w_in",
    )(w_in)
    w_out_p = pl.pallas_call(
        functools.partial(_prep_out_kernel, f=f),
        grid=(depth, two, fp // FF_TILE),
        in_specs=[pl.BlockSpec((None, None, FF_TILE, d), lambda l, k, j: (l, k, j, 0))],
        out_specs=pl.BlockSpec((None, None, FF_TILE, d), lambda l, k, j: (l, k, j, 0)),
        out_shape=jax.ShapeDtypeStruct((depth, two, fp, d), BF16),
        compiler_params=_cparams("parallel", "parallel", "parallel"),
        name="prep_w_out",
    )(w_out)
    return w_gate_p, w_up_p, w_out_p


def _ffn_kernel(x_ref, mod_ref, g_ref, wg_ref, wu_ref, wo_ref, o_ref, h_ref, *, sub):
    j = pl.program_id(1)
    last = pl.num_programs(1) - 1

    def partial_out(h):
        gate = jnp.dot(h, wg_ref[...], preferred_element_type=F32)
        up = jnp.dot(h, wu_ref[...], preferred_element_type=F32)
        act = (_silu(gate) * up).astype(BF16)
        return jnp.dot(act, wo_ref[...], preferred_element_type=F32)

    @pl.when(j == 0)
    def _():
        h = _mod_norm(x_ref[...], g_ref[...], mod_ref[3 * sub + 1:3 * sub + 2, :], mod_ref[3 * sub:3 * sub + 1, :])
        h = h.astype(BF16)
        h_ref[...] = h
        o_ref[...] = partial_out(h)

    @pl.when((j > 0) & (j < last))
    def _():
        o_ref[...] += partial_out(h_ref[...])

    @pl.when(j == last)
    def _():
        y = o_ref[...] + partial_out(h_ref[...])
        o_ref[...] = x_ref[...] + (0.5 * mod_ref[3 * sub + 2:3 * sub + 3, :]) * y


def _ffn(x, mods_l, g, w_gate_p, w_up_p, w_out_p, l, k, row_of_tile, tm, sub):
    m, d = x.shape
    fp = w_out_p.shape[2]
    nj = fp // FF_TILE
    assert nj >= 2
    return pl.pallas_call(
        functools.partial(_ffn_kernel, sub=sub),
        grid=(m // tm, nj),
        in_specs=[
            pl.BlockSpec((tm, d), lambda i, j: (i, 0)),
            pl.BlockSpec((None, N_MOD, d), lambda i, j: (row_of_tile(i), 0, 0)),
            pl.BlockSpec((1, d), lambda i, j: (0, 0)),
            pl.BlockSpec((None, None, d, FF_TILE), lambda i, j: (l, k, 0, j)),
            pl.BlockSpec((None, None, d, FF_TILE), lambda i, j: (l, k, 0, j)),
            pl.BlockSpec((None, None, FF_TILE, d), lambda i, j: (l, k, j, 0)),
        ],
        out_specs=pl.BlockSpec((tm, d), lambda i, j: (i, 0)),
        out_shape=jax.ShapeDtypeStruct((m, d), F32),
        scratch_shapes=[pltpu.VMEM((tm, d), BF16)],
        compiler_params=_cparams("parallel", "arbitrary"),
        name="ffn",
    )(x, mods_l, g, w_gate_p, w_up_p, w_out_p)


def _proj_kernel(x_ref, mod_ref, g_ref, w_ref, wab_ref, o_ref, ab_ref, h_ref):
    j = pl.program_id(1)

    @pl.when(j == 0)
    def _():
        h = _mod_norm(x_ref[...], g_ref[...], mod_ref[4:5, :], mod_ref[3:4, :]).astype(BF16)
        h_ref[...] = h
        o_ref[...] = jnp.dot(h, w_ref[...], preferred_element_type=F32)
        ab_ref[...] = jnp.dot(h, wab_ref[...], preferred_element_type=F32)

    @pl.when(j > 0)
    def _():
        o_ref[...] = jnp.dot(h_ref[...], w_ref[...], preferred_element_type=F32)


def _gdn_proj(x, mods_l, g, w_main, w_ab, row_of_tile, tm):
    m, d = x.shape
    n = w_main.shape[1]
    tn = next(c_ for c_ in (2048, 1024, 512, 256, 128) if n % c_ == 0)
    return pl.pallas_call(
        _proj_kernel,
        grid=(m // tm, n // tn),
        in_specs=[
            pl.BlockSpec((tm, d), lambda i, j: (i, 0)),
            pl.BlockSpec((None, N_MOD, d), lambda i, j: (row_of_tile(i), 0, 0)),
            pl.BlockSpec((1, d), lambda i, j: (0, 0)),
            pl.BlockSpec((d, tn), lambda i, j: (0, j)),
            pl.BlockSpec((d, LANES), lambda i, j: (0, 0)),
        ],
        out_specs=[pl.BlockSpec((tm, tn), lambda i, j: (i, j)), pl.BlockSpec((tm, LANES), lambda i, j: (i, 0))],
        out_shape=[jax.ShapeDtypeStruct((m, n), F32), jax.ShapeDtypeStruct((m, LANES), F32)],
        scratch_shapes=[pltpu.VMEM((tm, d), BF16)],
        compiler_params=_cparams("parallel", "arbitrary"),
        name="gdn_proj",
    )(x, mods_l, g, w_main, w_ab)


def _row_sum_lanes(x):
    return jnp.dot(x.astype(BF16), jnp.ones((LANES, LANES), BF16), preferred_element_type=F32)


def _conv_silu(x_ref, w_ref, o_ref, pad_ref, t):
    zeros = jnp.zeros((SUBLANES, x_ref.shape[1]), F32)
    pad_ref[0:SUBLANES, :] = zeros
    pad_ref[t + SUBLANES:t + 2 * SUBLANES, :] = zeros
    pad_ref[SUBLANES:t + SUBLANES, :] = x_ref[...]
    acc = x_ref[...] * w_ref[CONV_PAD:CONV_PAD + 1, :]
    for j in range(CONV_K):
        if j != CONV_PAD:
            r0 = SUBLANES + j - CONV_PAD
            acc = acc + pad_ref[r0:r0 + t, :] * w_ref[j:j + 1, :]
    o_ref[...] = _silu(acc)


def _chunk_cumsum(g, t, reverse):
    pos = lax.broadcasted_iota(jnp.int32, g.shape, 0) % CHUNK
    s = 1
    while s < CHUNK:
        if reverse:
            g = g + jnp.where(pos < CHUNK - s, pltpu.roll(g, t - s, 0), 0.0)
        else:
            g = g + jnp.where(pos >= s, pltpu.roll(g, s, 0), 0.0)
        s *= 2
    return g


def _gdn_kernel(*refs, t, hb, n_heads, has_s0, n_alias, emit_state):
    (q_ref, k_ref, v_ref, z_ref, ab_ref, cq_ref, ck_ref, cv_ref, alog_ref, dtb_ref, ng_ref), refs = refs[:11], refs[11:]
    if has_s0:
        s0_ref, refs = refs[0], refs[1:]
    refs = refs[n_alias:]
    og_ref, refs = refs[0], refs[1:]
    if emit_state:
        sout_ref, refs = refs[0], refs[1:]
    qn_ref, kn_ref, vc_ref, gam_ref, beta_ref, s_ref, o_ref, pad_ref = refs

    dk = LANES
    n_chunks = t // CHUNK
    hg = pl.program_id(1)

    _conv_silu(q_ref, cq_ref, qn_ref, pad_ref, t)
    _conv_silu(k_ref, ck_ref, kn_ref, pad_ref, t)
    _conv_silu(v_ref, cv_ref, vc_ref, pad_ref, t)
    for hh in range(hb):
        cs = slice(hh * dk, (hh + 1) * dk)
        qh = qn_ref[:, cs]
        kh = kn_ref[:, cs]
        qn_ref[:, cs] = qh * lax.rsqrt(_row_sum_lanes(qh * qh) + L2_EPS) * (dk ** -0.5)
        kn_ref[:, cs] = kh * lax.rsqrt(_row_sum_lanes(kh * kh) + L2_EPS)

    ab = ab_ref[...]
    pre = ab + dtb_ref[...]
    softplus = jnp.maximum(pre, 0.0) + jnp.log1p(jnp.exp(-jnp.abs(pre)))
    g_all = -jnp.exp(alog_ref[...]) * softplus
    beta_all = _sigmoid(ab)
    cum = (_chunk_cumsum(g_all, t, False), _chunk_cumsum(g_all, t, True))
    lane = lax.broadcasted_iota(jnp.int32, ab.shape, 1)
    for d in range(2):
        for hh in range(hb):
            col = d * 2 * n_heads + hg * hb + hh
            gsel = jnp.sum(jnp.where(lane == col, cum[d], 0.0), axis=-1, keepdims=True)
            bsel = jnp.sum(jnp.where(lane == col + n_heads, beta_all, 0.0), axis=-1, keepdims=True)
            gam_ref[d * hb + hh] = jnp.broadcast_to(gsel, ab.shape)
            beta_ref[d * hb + hh] = jnp.broadcast_to(bsel, ab.shape)
            s_ref[d * hb + hh] = s0_ref[d, hh] if has_s0 else jnp.zeros((dk, dk), F32)
    ii = lax.broadcasted_iota(jnp.int32, (CHUNK, LANES), 0)
    lane = lax.broadcasted_iota(jnp.int32, (CHUNK, LANES), 1)
    is_f = lane < CHUNK
    jj = jnp.where(is_f, lane, lane - CHUNK)
    ahead = jnp.where(is_f, ii - jj, jj - ii)
    incl = ahead >= 0
    strict = ahead > 0
    eye = (ii == jj).astype(F32)
    pair_masks = []
    s_blk = 1
    while s_blk < CHUNK:
        pair_masks.append(((ii // (2 * s_blk)) == (jj // (2 * s_blk))) & ((ii // s_blk) != (jj // s_blk)))
        s_blk *= 2

    def block_diag(m):
        zero = jnp.zeros_like(m)
        return jnp.concatenate([jnp.where(is_f, m, zero), jnp.where(is_f, zero, m)], axis=0)

    def diag2(top, bottom):
        return jnp.concatenate([jnp.concatenate([top, jnp.zeros_like(bottom)], axis=1),
                                jnp.concatenate([jnp.zeros_like(top), bottom], axis=1)], axis=0)

    def step(s, carry):
        if isinstance(s, int):
            rows_f = slice(s * CHUNK, (s + 1) * CHUNK)
            rows_b = slice((n_chunks - 1 - s) * CHUNK, (n_chunks - s) * CHUNK)
        else:
            rows_f = pl.ds(pl.multiple_of(s * CHUNK, CHUNK), CHUNK)
            rows_b = pl.ds(pl.multiple_of((n_chunks - 1 - s) * CHUNK, CHUNK), CHUNK)
        heads = range(hb)
        cs_l = [slice(hh * dk, (hh + 1) * dk) for hh in heads]
        qf = [qn_ref[rows_f, cs] for cs in cs_l]
        kf = [kn_ref[rows_f, cs] for cs in cs_l]
        vf = [vc_ref[rows_f, cs] for cs in cs_l]
        qb = [qn_ref[rows_b, cs] for cs in cs_l]
        kb = [kn_ref[rows_b, cs] for cs in cs_l]
        vb = [vc_ref[rows_b, cs] for cs in cs_l]
        gf = [gam_ref[hh, rows_f, :] for hh in heads]
        gb = [gam_ref[hb + hh, rows_b, :] for hh in heads]
        bf = [beta_ref[hh, rows_f, :] for hh in heads]
        bb = [beta_ref[hb + hh, rows_b, :] for hh in heads]

        a_l, x_l, aqk_l = [], [], []
        for hh in heads:
            gr = jnp.concatenate([gf[hh], gb[hh]], axis=0).T[:CHUNK, :]
            diff = jnp.where(is_f, gf[hh], gb[hh]) - gr
            dec_incl = jnp.where(incl, jnp.exp(jnp.where(incl, diff, 0.0)), 0.0)
            lhs = jnp.concatenate([jnp.concatenate([kf[hh], kb[hh]], axis=1),
                                   jnp.concatenate([qf[hh], qb[hh]], axis=1)], axis=0)
            kq = _dot_nt(lhs, diag2(kf[hh], kb[hh]))
            a = jnp.where(is_f, bf[hh], bb[hh]) * kq[:CHUNK] * jnp.where(strict, dec_incl, 0.0)
            a = a.astype(BF16).astype(F32)
            a_l.append(a)
            aqk_l.append(kq[CHUNK:] * dec_incl)
            x_l.append(eye - jnp.where(pair_masks[0], a, 0.0))
        for pm in pair_masks[1:]:
            t_l = [_dot(x, block_diag(jnp.where(pm, a, 0.0))) for x, a in zip(x_l, a_l)]
            x_l = [x - _dot(tx, block_diag(x)) for x, tx in zip(x_l, t_l)]
        egf = [jnp.exp(g) for g in gf]
        egb = [jnp.exp(g) for g in gb]
        wu_l = []
        for hh in heads:
            r_f = jnp.concatenate([kf[hh] * (bf[hh] * egf[hh]), vf[hh] * bf[hh]], axis=1).astype(BF16)
            r_b = jnp.concatenate([kb[hh] * (bb[hh] * egb[hh]), vb[hh] * bb[hh]], axis=1).astype(BF16)
            wu_l.append(_dot(x_l[hh], diag2(r_f, r_b)))
        stf = [s_ref[hh] for hh in heads]
        stb = [s_ref[hb + hh] for hh in heads]
        wqf = [_dot(jnp.concatenate([wu[:, :dk], q * eg], axis=0), st) for wu, q, eg, st in zip(wu_l, qf, egf, stf)]
        wqb = [_dot(jnp.concatenate([wu[:, 2 * dk:3 * dk], q * eg], axis=0), st)
               for wu, q, eg, st in zip(wu_l, qb, egb, stb)]
        for hh in heads:
            vn_f = wu_l[hh][:, dk:2 * dk] - wqf[hh][:CHUNK]
            vn_b = wu_l[hh][:, 3 * dk:] - wqb[hh][:CHUNK]
            glf = gf[hh][CHUNK - 1:CHUNK, :]
            glb = gb[hh][0:1, :]
            kd = jnp.concatenate([kf[hh] * jnp.exp(glf - gf[hh]), kb[hh] * jnp.exp(glb - gb[hh])], axis=0)
            res = _dot(jnp.concatenate([aqk_l[hh], kd.T], axis=0), diag2(vn_f, vn_b))
            o_ref[0, rows_f, cs_l[hh]] = wqf[hh][CHUNK:] + res[:CHUNK, :dk]
            o_ref[1, rows_b, cs_l[hh]] = wqb[hh][CHUNK:] + res[:CHUNK, dk:]
            s_ref[hh] = stf[hh] * jnp.exp(glf) + res[CHUNK:, :dk]
            s_ref[hb + hh] = stb[hh] * jnp.exp(glb) + res[CHUNK:, dk:]
        return carry

    if n_chunks <= UNROLL_CHUNKS:
        for s in range(n_chunks):
            step(s, 0)
    else:
        lax.fori_loop(0, n_chunks, step, 0)

    for hh in range(hb):
        cs = slice(hh * dk, (hh + 1) * dk)
        o = o_ref[0, :, cs] + o_ref[1, :, cs]
        o = o * lax.rsqrt(jnp.mean(o * o, axis=-1, keepdims=True) + RMS_EPS) * ng_ref[...] * _silu(z_ref[:, cs])
        og_ref[:, cs] = o.astype(og_ref.dtype)
    if emit_state:
        for d in range(2):
            for hh in range(hb):
                sout_ref[d, hh] = s_ref[d * hb + hh]


def _gdn_core(proj, ab, conv_w, alog_row, dtb_row, ng_row, s0, og_prev, state_prev, *, row0, n_seq, t, hb,
              n_heads, state_slot):
    dk = LANES
    wb = hb * dk
    n_hg = n_heads // hb
    rb = row0 // t
    has_s0 = s0 is not None
    emit_state = state_slot is not None

    def col_spec(part):
        return pl.BlockSpec((t, wb), lambda b, h: (rb + b, part * n_hg + h))

    def conv_spec(part):
        return pl.BlockSpec((CONV_K, wb), lambda b, h: (0, part * n_hg + h))

    row_spec = pl.BlockSpec((1, LANES), lambda b, h: (0, 0))
    in_specs = [col_spec(0), col_spec(1), col_spec(2), col_spec(3),
                pl.BlockSpec((t, LANES), lambda b, h: (rb + b, 0)),
                conv_spec(0), conv_spec(1), conv_spec(2), row_spec, row_spec, row_spec]
    args = [proj, proj, proj, proj, ab, conv_w, conv_w, conv_w, alog_row, dtb_row, ng_row]
    if has_s0:
        in_specs.append(pl.BlockSpec((None, 2, hb, dk, dk), lambda b, h: (b, 0, h, 0, 0)))
        args.append(s0)
    out_specs = [pl.BlockSpec((t, wb), lambda b, h: (rb + b, h))]
    out_shape = [jax.ShapeDtypeStruct((proj.shape[0], n_heads * dk), BF16)]
    if emit_state:
        mi, n_mix = state_slot
        out_specs.append(pl.BlockSpec((None, None, 2, hb, dk, dk), lambda b, h: (b, mi, 0, h, 0, 0)))
        out_shape.append(jax.ShapeDtypeStruct((n_seq, n_mix, 2, n_heads, dk, dk), F32))
    aliases = {}
    for out_idx, prev in enumerate((og_prev, state_prev)):
        if prev is not None:
            aliases[len(args)] = out_idx
            in_specs.append(pl.BlockSpec(memory_space=pl.ANY))
            args.append(prev)
    return pl.pallas_call(
        functools.partial(_gdn_kernel, t=t, hb=hb, n_heads=n_heads, has_s0=has_s0, n_alias=len(aliases),
                          emit_state=emit_state),
        grid=(n_seq, n_hg),
        in_specs=in_specs,
        out_specs=out_specs,
        out_shape=out_shape,
        input_output_aliases=aliases,
        scratch_shapes=[
            pltpu.VMEM((t, wb), F32), pltpu.VMEM((t, wb), F32), pltpu.VMEM((t, wb), F32),
            pltpu.VMEM((2 * hb, t, LANES), F32), pltpu.VMEM((2 * hb, t, LANES), F32),
            pltpu.VMEM((2 * hb, dk, dk), F32), pltpu.VMEM((2, t, wb), F32),
            pltpu.VMEM((t + 2 * SUBLANES, wb), F32),
        ],
        compiler_params=_cparams("parallel", "parallel"),
        name="gdn_core",
    )(*args)


def _out_kernel(x_ref, mod_ref, a_ref, w_ref, o_ref):
    y = jnp.dot(a_ref[...], w_ref[...], preferred_element_type=F32)
    o_ref[...] = x_ref[...] + mod_ref[5:6, :] * y


def _gdn_out(x, mods_l, og, w_out, row_of_tile, tm):
    m, d = x.shape
    kdim = og.shape[1]
    return pl.pallas_call(
        _out_kernel,
        grid=(m // tm,),
        in_specs=[
            pl.BlockSpec((tm, d), lambda i: (i, 0)),
            pl.BlockSpec((None, N_MOD, d), lambda i: (row_of_tile(i), 0, 0)),
            pl.BlockSpec((tm, kdim), lambda i: (i, 0)),
            pl.BlockSpec((kdim, d), lambda i: (0, 0)),
        ],
        out_specs=pl.BlockSpec((tm, d), lambda i: (i, 0)),
        out_shape=jax.ShapeDtypeStruct((m, d), F32),
        compiler_params=_cparams("parallel"),
        name="gdn_out",
    )(x, mods_l, og, w_out)


def _pool_kernel(x_ref, mod_ref, g_ref, w_ref, sc_ref, *rest, t):
    o_ref = rest[-1]
    x = x_ref[...]
    rstd = lax.rsqrt(jnp.mean(x * x, axis=-1, keepdims=True) + RMS_EPS)
    pg = w_ref.shape[1]
    edge = max(POOL_WINDOWS)
    zeros = jnp.zeros((edge, pg), F32)
    row = lax.broadcasted_iota(jnp.int32, (t, pg), 0)
    for gi, win in enumerate(POOL_WINDOWS):
        cs = slice(gi * pg, (gi + 1) * pg)
        xg = x_ref[:, cs]
        h = (xg * rstd * g_ref[:, cs]) * (1.0 + mod_ref[4:5, cs]) + mod_ref[3:4, cs]
        half = win // 2
        n = t + 2 * edge
        acc = jnp.concatenate([zeros, h, zeros], axis=0)
        s = 1
        while s < win:
            acc = acc + pltpu.roll(acc, n - s, 0)
            s *= 2
        acc = acc[edge - half:edge - half + t]
        cnt = (jnp.minimum(row + (win - half), t) - jnp.maximum(row - half, 0)).astype(F32)
        diff = (acc / cnt - h).astype(BF16)
        y = jnp.dot(diff, w_ref[gi], preferred_element_type=F32) * sc_ref[:, cs]
        o_ref[:, cs] = xg + mod_ref[5:6, cs] * y


def _pool(x, mods_l, g, w, scale, out_prev, *, row0, n_seq, t, mod_row0, mod_per_seq):
    m, d = x.shape
    ng, pg, _ = w.shape
    assert ng == len(POOL_WINDOWS) and ng * pg == d
    rb = row0 // t
    mod_idx = (lambda b: mod_row0 + b) if mod_per_seq else (lambda b: mod_row0)
    in_specs = [
        pl.BlockSpec((t, d), lambda b: (rb + b, 0)),
        pl.BlockSpec((None, N_MOD, d), lambda b: (mod_idx(b), 0, 0)),
        pl.BlockSpec((1, d), lambda b: (0, 0)),
        pl.BlockSpec((ng, pg, pg), lambda b: (0, 0, 0)),
        pl.BlockSpec((1, d), lambda b: (0, 0)),
    ]
    args = [x, mods_l, g, w, scale]
    aliases = {}
    if out_prev is not None:
        aliases[len(args)] = 0
        in_specs.append(pl.BlockSpec(memory_space=pl.ANY))
        args.append(out_prev)
    return pl.pallas_call(
        functools.partial(_pool_kernel, t=t),
        grid=(n_seq,),
        in_specs=in_specs,
        out_specs=pl.BlockSpec((t, d), lambda b: (rb + b, 0)),
        out_shape=jax.ShapeDtypeStruct((m, d), F32),
        input_output_aliases=aliases,
        compiler_params=_cparams("parallel"),
        name="pool",
    )(*args)


def _final_kernel(x_ref, g_ref, o_ref):
    x = x_ref[...]
    o_ref[...] = x * lax.rsqrt(jnp.mean(x * x, axis=-1, keepdims=True) + RMS_EPS) * g_ref[...]


def _final_norm(x, g, tm, row0, rows):
    _, d = x.shape
    rb = row0 // tm
    return pl.pallas_call(
        _final_kernel,
        grid=(rows // tm,),
        in_specs=[pl.BlockSpec((tm, d), lambda i: (rb + i, 0)), pl.BlockSpec((1, d), lambda i: (0, 0))],
        out_specs=pl.BlockSpec((tm, d), lambda i: (i, 0)),
        out_shape=jax.ShapeDtypeStruct((rows, d), F32),
        compiler_params=_cparams("parallel"),
        name="final_norm",
    )(x, g)


def _place_cols(vals, n_heads):
    row = jnp.zeros((LANES,), F32)
    for d in range(2):
        row = lax.dynamic_update_slice(row, vals[d].astype(F32), (d * 2 * n_heads,))
    return row.reshape(1, LANES)


def kernel(x_prompt, x_sample, state_gdn, c, c_ctx, w_mod, b_mod, norm_g, ffn_w_in, ffn_w_out, gdn_w_in, gdn_conv, gdn_a_log, gdn_dt_bias, gdn_norm_g, gdn_w_out, pool_w, pool_scale, final_g):
    batch, seq, d = x_prompt.shape
    dec_batch, dec_seq, _ = x_sample.shape
    depth = w_mod.shape[0]
    n_heads = gdn_a_log.shape[-1]
    dk = LANES
    qk = n_heads * dk
    f = ffn_w_out.shape[2]
    fp = FF_TILE * ((f + FF_TILE - 1) // FF_TILE)
    m_p = batch * seq
    m_s = dec_batch * dec_seq
    assert 1 + dec_batch <= MOD_ROWS and gdn_w_in.shape[2] == 4 * qk + 4 * n_heads and 4 * n_heads <= LANES
    assert m_p % dec_seq == 0

    def token_tile(candidates):
        return next(c_ for c_ in candidates if m_p % c_ == 0 and dec_seq % c_ == 0)

    def mod_row_of_tile(tile):
        n_prompt_tiles = m_p // tile
        return lambda i: jnp.where(i < n_prompt_tiles, 0, 1 + ((i - n_prompt_tiles) * tile) // dec_seq)

    tm = token_tile((512, 256, 128, 64))
    row_of_tile = mod_row_of_tile(tm)
    tm_proj = token_tile((1024, 512, 256, 128, 64))

    c8 = jnp.zeros((MOD_ROWS, d), F32).at[0].set(c_ctx).at[1:1 + dec_batch].set(c)
    mods = _modulation(c8, w_mod, b_mod).reshape(depth, MOD_ROWS, N_MOD, d)

    x = _token_stream(x_prompt.reshape(m_p, d), x_sample.reshape(m_s, d),
                      _grid_pos_embed(dec_seq, d, x_sample.dtype), tm)

    w_gate_p, w_up_p, w_out_p = _prep_ffn_weights(ffn_w_in, ffn_w_out, fp)

    n_gdn = (depth + 1) // 2
    new_state = None
    for l in range(depth):
        mods_l = mods[l]
        ng = norm_g[l]

        def ffn(x, i, sub):
            return _ffn(x, mods_l, ng[sub:sub + 1], w_gate_p, w_up_p, w_out_p, l, i, mod_row_of_tile(tm_proj),
                        tm_proj, sub)

        x = ffn(x, 0, 0)
        mi = l // 2
        if l % 2 == 0:
            w_in = gdn_w_in[mi]
            w_main = w_in[:, :4 * qk].astype(BF16)
            w_ab = jnp.pad(w_in[:, 4 * qk:], ((0, 0), (0, LANES - 4 * n_heads))).astype(BF16)
            proj, ab = _gdn_proj(x, mods_l, ng[1:2], w_main, w_ab, mod_row_of_tile(tm_proj), tm_proj)
            alog_row = _place_cols(gdn_a_log[mi], n_heads)
            dtb_row = _place_cols(gdn_dt_bias[mi], n_heads)
            ng_row = gdn_norm_g[mi].reshape(1, dk).astype(F32)
            core = functools.partial(_gdn_core, proj, ab, gdn_conv[mi], alog_row, dtb_row, ng_row, n_heads=n_heads)
            og, new_state = core(None, None, new_state, row0=0, n_seq=batch, t=seq, hb=min(16, n_heads),
                                 state_slot=(mi, n_gdn))
            (og,) = core(state_gdn[:, mi], og, None, row0=m_p, n_seq=dec_batch, t=dec_seq, hb=min(4, n_heads),
                         state_slot=None)
            x = _gdn_out(x, mods_l, og, gdn_w_out[mi].astype(BF16), row_of_tile, tm)
        else:
            pw = pool_w[mi].astype(BF16)
            ps = pool_scale[mi].reshape(1, d)
            pool = functools.partial(_pool, x, mods_l, ng[1:2], pw, ps)
            x_new = pool(None, row0=0, n_seq=batch, t=seq, mod_row0=0, mod_per_seq=False)
            x = pool(x_new, row0=m_p, n_seq=dec_batch, t=dec_seq, mod_row0=1, mod_per_seq=True)
        x = ffn(x, 1, 2)

    fg = final_g.reshape(1, d)
    y_prompt = _final_norm(x, fg, tm, 0, m_p).reshape(batch, seq, d)
    y_sample = _final_norm(x, fg, tm, m_p, m_s).reshape(dec_batch, dec_seq, d)
    return (y_prompt, y_sample, new_state.astype(state_gdn.dtype))
```

```python
import functools
import math

import jax
import jax.numpy as jnp
from jax import lax
from jax.experimental import pallas as pl
from jax.experimental.pallas import tpu as pltpu

F32 = jnp.float32
BF16 = jnp.bfloat16

RMS_EPS = 1e-6
L2_EPS = 1e-6
CHUNK = 64
CONV_K = 5
CONV_PAD = CONV_K // 2
POOL_WINDOWS = (2, 4, 8, 16)
GRID_W = 64
POS_BASE = 10000.0
N_MOD = 9
MOD_ROWS = 8
LANES = 128
SUBLANES = 8
FF_TILE = 512
UNROLL_CHUNKS = 4
LOOP_CHUNKS = 4
VMEM_LIMIT = 60 * 1024 * 1024


def _cparams(*sem):
    return pltpu.CompilerParams(dimension_semantics=sem, vmem_limit_bytes=VMEM_LIMIT)


def _sigmoid(x):
    return jax.nn.sigmoid(x)


def _silu(x):
    return x * _sigmoid(x)


def _dot(a, b):
    return jnp.dot(a.astype(BF16), b.astype(BF16), preferred_element_type=F32)


def _dot_nt(a, b):
    return lax.dot_general(a.astype(BF16), b.astype(BF16), (((1,), (1,)), ((), ())), preferred_element_type=F32)


def _split(a):
    hi = a.astype(BF16)
    lo = (a - hi.astype(F32)).astype(BF16)
    return hi, lo


def _mod_norm(x, g, scale, shift):
    ms = jnp.mean(x * x, axis=-1, keepdims=True)
    return (x * lax.rsqrt(ms + RMS_EPS) * g) * (1.0 + scale) + shift


def _mod_kernel(c_ref, w_ref, b_ref, o_ref):
    s_hi, s_lo = _split(_silu(c_ref[...]))
    w = w_ref[...].astype(BF16)
    d = functools.partial(jnp.dot, preferred_element_type=F32)
    o_ref[...] = d(s_hi, w) + d(s_lo, w) + b_ref[...]


def _modulation(c8, w_mod, b_mod):
    depth, d, nd = w_mod.shape
    tn = next(c_ for c_ in (2048, 1024, 512, 256, 128) if nd % c_ == 0)
    return pl.pallas_call(
        _mod_kernel,
        grid=(depth, nd // tn),
        in_specs=[
            pl.BlockSpec((MOD_ROWS, d), lambda l, j: (0, 0)),
            pl.BlockSpec((None, d, tn), lambda l, j: (l, 0, j)),
            pl.BlockSpec((None, 1, tn), lambda l, j: (l, 0, j)),
        ],
        out_specs=pl.BlockSpec((None, MOD_ROWS, tn), lambda l, j: (l, 0, j)),
        out_shape=jax.ShapeDtypeStruct((depth, MOD_ROWS, nd), F32),
        compiler_params=_cparams("parallel", "parallel"),
        name="modulation",
    )(c8, w_mod, b_mod.reshape(depth, 1, nd))


def _tokens_kernel(xp_ref, xs_ref, p_ref, o_ref, *, n_prompt_tiles):
    i = pl.program_id(0)

    @pl.when(i < n_prompt_tiles)
    def _():
        o_ref[...] = xp_ref[...]

    @pl.when(i >= n_prompt_tiles)
    def _():
        o_ref[...] = xs_ref[...] + p_ref[...]


def _token_stream(xp, xs, pos, ta):
    m_p, d = xp.shape
    m_s = xs.shape[0]
    t = pos.shape[0]
    n_p = m_p // ta
    return pl.pallas_call(
        functools.partial(_tokens_kernel, n_prompt_tiles=n_p),
        grid=((m_p + m_s) // ta,),
        in_specs=[
            pl.BlockSpec((ta, d), lambda i: (jnp.minimum(i, n_p - 1), 0)),
            pl.BlockSpec((ta, d), lambda i: (jnp.maximum(i - n_p, 0), 0)),
            pl.BlockSpec((ta, d), lambda i: ((jnp.maximum(i - n_p, 0) * ta % t) // ta, 0)),
        ],
        out_specs=pl.BlockSpec((ta, d), lambda i: (i, 0)),
        out_shape=jax.ShapeDtypeStruct((m_p + m_s, d), xp.dtype),
        compiler_params=_cparams("parallel"),
        name="token_stream",
    )(xp, xs, pos)


def _grid_pos_embed(n_tok, d, dtype):
    idx = jnp.arange(n_tok)
    r = (idx // GRID_W).astype(F32)
    col = (idx % GRID_W).astype(F32)
    n_freq = d // 4
    freqs = jnp.exp(-math.log(POS_BASE) * jnp.arange(n_freq, dtype=F32) / n_freq)
    ar = r[:, None] * freqs
    ac = col[:, None] * freqs
    return jnp.concatenate([jnp.sin(ar), jnp.cos(ar), jnp.sin(ac), jnp.cos(ac)], axis=-1).astype(dtype)


def _prep_in_kernel(w_ref, g_ref, u_ref, *, f):
    pad = jnp.zeros((g_ref.shape[0], g_ref.shape[1] - f), BF16)
    g_ref[:, :f] = w_ref[:, :f].astype(BF16)
    u_ref[:, :f] = w_ref[:, f:].astype(BF16)
    g_ref[:, f:] = pad
    u_ref[:, f:] = pad


def _prep_out_kernel(w_ref, o_ref, *, f):
    row = pl.program_id(2) * FF_TILE + lax.broadcasted_iota(jnp.int32, w_ref.shape, 0)
    o_ref[...] = jnp.where(row < f, w_ref[...], 0.0).astype(BF16)


def _prep_ffn_weights(w_in, w_out, fp):
    depth, two, d, f2 = w_in.shape
    f = f2 // 2
    assert f % LANES == 0
    rows = 256
    spec_w = pl.BlockSpec((None, None, rows, f2), lambda l, k, r: (l, k, r, 0))
    spec_p = pl.BlockSpec((None, None, rows, fp), lambda l, k, r: (l, k, r, 0))
    padded = jax.ShapeDtypeStruct((depth, two, d, fp), BF16)
    w_gate_p, w_up_p = pl.pallas_call(
        functools.partial(_prep_in_kernel, f=f),
        grid=(depth, two, d // rows),
        in_specs=[spec_w],
        out_specs=[spec_p, spec_p],
        out_shape=[padded, padded],
        compiler_params=_cparams("parallel", "parallel", "parallel"),
        name="prep_w_in",
    )(w_in)
    w_out_p = pl.pallas_call(
        functools.partial(_prep_out_kernel, f=f),
        grid=(depth, two, fp // FF_TILE),
        in_specs=[pl.BlockSpec((None, None, FF_TILE, d), lambda l, k, j: (l, k, j, 0))],
        out_specs=pl.BlockSpec((None, None, FF_TILE, d), lambda l, k, j: (l, k, j, 0)),
        out_shape=jax.ShapeDtypeStruct((depth, two, fp, d), BF16),
        compiler_params=_cparams("parallel", "parallel", "parallel"),
        name="prep_w_out",
    )(w_out)
    return w_gate_p, w_up_p, w_out_p


def _ffn_kernel(x_ref, mod_ref, g_ref, wg_ref, wu_ref, wo_ref, o_ref, h_ref, *, sub):
    j = pl.program_id(1)
    last = pl.num_programs(1) - 1

    def partial_out(h):
        gate = jnp.dot(h, wg_ref[...], preferred_element_type=F32)
        up = jnp.dot(h, wu_ref[...], preferred_element_type=F32)
        act = (_silu(gate) * up).astype(BF16)
        return jnp.dot(act, wo_ref[...], preferred_element_type=F32)

    @pl.when(j == 0)
    def _():
        h = _mod_norm(x_ref[...], g_ref[...], mod_ref[3 * sub + 1:3 * sub + 2, :], mod_ref[3 * sub:3 * sub + 1, :])
        h = h.astype(BF16)
        h_ref[...] = h
        o_ref[...] = partial_out(h)

    @pl.when((j > 0) & (j < last))
    def _():
        o_ref[...] += partial_out(h_ref[...])

    @pl.when(j == last)
    def _():
        y = o_ref[...] + partial_out(h_ref[...])
        o_ref[...] = x_ref[...] + (0.5 * mod_ref[3 * sub + 2:3 * sub + 3, :]) * y


def _ffn(x, mods_l, g, w_gate_p, w_up_p, w_out_p, l, k, row_of_tile, tm, sub):
    m, d = x.shape
    fp = w_out_p.shape[2]
    nj = fp // FF_TILE
    assert nj >= 2
    return pl.pallas_call(
        functools.partial(_ffn_kernel, sub=sub),
        grid=(m // tm, nj),
        in_specs=[
            pl.BlockSpec((tm, d), lambda i, j: (i, 0)),
            pl.BlockSpec((None, N_MOD, d), lambda i, j: (row_of_tile(i), 0, 0)),
            pl.BlockSpec((1, d), lambda i, j: (0, 0)),
            pl.BlockSpec((None, None, d, FF_TILE), lambda i, j: (l, k, 0, j)),
            pl.BlockSpec((None, None, d, FF_TILE), lambda i, j: (l, k, 0, j)),
            pl.BlockSpec((None, None, FF_TILE, d), lambda i, j: (l, k, j, 0)),
        ],
        out_specs=pl.BlockSpec((tm, d), lambda i, j: (i, 0)),
        out_shape=jax.ShapeDtypeStruct((m, d), F32),
        scratch_shapes=[pltpu.VMEM((tm, d), BF16)],
        compiler_params=_cparams("parallel", "arbitrary"),
        name="ffn",
    )(x, mods_l, g, w_gate_p, w_up_p, w_out_p)


def _proj_kernel(x_ref, mod_ref, g_ref, w_ref, wab_ref, o_ref, ab_ref, h_ref):
    j = pl.program_id(1)

    @pl.when(j == 0)
    def _():
        h = _mod_norm(x_ref[...], g_ref[...], mod_ref[4:5, :], mod_ref[3:4, :]).astype(BF16)
        h_ref[...] = h
        o_ref[...] = jnp.dot(h, w_ref[...], preferred_element_type=F32)
        ab_ref[...] = jnp.dot(h, wab_ref[...], preferred_element_type=F32)

    @pl.when(j > 0)
    def _():
        o_ref[...] = jnp.dot(h_ref[...], w_ref[...], preferred_element_type=F32)


def _gdn_proj(x, mods_l, g, w_main, w_ab, row_of_tile, tm):
    m, d = x.shape
    n = w_main.shape[1]
    tn = next(c_ for c_ in (2048, 1024, 512, 256, 128) if n % c_ == 0)
    return pl.pallas_call(
        _proj_kernel,
        grid=(m // tm, n // tn),
        in_specs=[
            pl.BlockSpec((tm, d), lambda i, j: (i, 0)),
            pl.BlockSpec((None, N_MOD, d), lambda i, j: (row_of_tile(i), 0, 0)),
            pl.BlockSpec((1, d), lambda i, j: (0, 0)),
            pl.BlockSpec((d, tn), lambda i, j: (0, j)),
            pl.BlockSpec((d, LANES), lambda i, j: (0, 0)),
        ],
        out_specs=[pl.BlockSpec((tm, tn), lambda i, j: (i, j)), pl.BlockSpec((tm, LANES), lambda i, j: (i, 0))],
        out_shape=[jax.ShapeDtypeStruct((m, n), F32), jax.ShapeDtypeStruct((m, LANES), F32)],
        scratch_shapes=[pltpu.VMEM((tm, d), BF16)],
        compiler_params=_cparams("parallel", "arbitrary"),
        name="gdn_proj",
    )(x, mods_l, g, w_main, w_ab)


def _row_sum_lanes(x):
    return jnp.dot(x.astype(BF16), jnp.ones((LANES, LANES), BF16), preferred_element_type=F32)


def _conv_silu(x_ref, w_ref, o_ref, pad_ref, t):
    zeros = jnp.zeros((SUBLANES, x_ref.shape[1]), F32)
    pad_ref[0:SUBLANES, :] = zeros
    pad_ref[t + SUBLANES:t + 2 * SUBLANES, :] = zeros
    pad_ref[SUBLANES:t + SUBLANES, :] = x_ref[...]
    acc = x_ref[...] * w_ref[CONV_PAD:CONV_PAD + 1, :]
    for j in range(CONV_K):
        if j != CONV_PAD:
            r0 = SUBLANES + j - CONV_PAD
            acc = acc + pad_ref[r0:r0 + t, :] * w_ref[j:j + 1, :]
    o_ref[...] = _silu(acc)


def _chunk_cumsum(g, t, reverse):
    pos = lax.broadcasted_iota(jnp.int32, g.shape, 0) % CHUNK
    s = 1
    while s < CHUNK:
        if reverse:
            g = g + jnp.where(pos < CHUNK - s, pltpu.roll(g, t - s, 0), 0.0)
        else:
            g = g + jnp.where(pos >= s, pltpu.roll(g, s, 0), 0.0)
        s *= 2
    return g


def _gdn_kernel(*refs, t, hb, n_heads, has_s0, n_alias, emit_state):
    (q_ref, k_ref, v_ref, z_ref, ab_ref, cq_ref, ck_ref, cv_ref, alog_ref, dtb_ref, ng_ref), refs = refs[:11], refs[11:]
    if has_s0:
        s0_ref, refs = refs[0], refs[1:]
    refs = refs[n_alias:]
    og_ref, refs = refs[0], refs[1:]
    if emit_state:
        sout_ref, refs = refs[0], refs[1:]
    qn_ref, kn_ref, vc_ref, gam_ref, beta_ref, s_ref, o_ref, pad_ref = refs

    dk = LANES
    n_chunks = t // CHUNK
    hg = pl.program_id(1)

    _conv_silu(q_ref, cq_ref, qn_ref, pad_ref, t)
    _conv_silu(k_ref, ck_ref, kn_ref, pad_ref, t)
    _conv_silu(v_ref, cv_ref, vc_ref, pad_ref, t)
    for hh in range(hb):
        cs = slice(hh * dk, (hh + 1) * dk)
        qh = qn_ref[:, cs]
        kh = kn_ref[:, cs]
        qn_ref[:, cs] = qh * lax.rsqrt(_row_sum_lanes(qh * qh) + L2_EPS) * (dk ** -0.5)
        kn_ref[:, cs] = kh * lax.rsqrt(_row_sum_lanes(kh * kh) + L2_EPS)

    ab = ab_ref[...]
    pre = ab + dtb_ref[...]
    softplus = jnp.maximum(pre, 0.0) + jnp.log1p(jnp.exp(-jnp.abs(pre)))
    g_all = -jnp.exp(alog_ref[...]) * softplus
    beta_all = _sigmoid(ab)
    cum = (_chunk_cumsum(g_all, t, False), _chunk_cumsum(g_all, t, True))
    lane = lax.broadcasted_iota(jnp.int32, ab.shape, 1)
    for d in range(2):
        for hh in range(hb):
            col = d * 2 * n_heads + hg * hb + hh
            gsel = jnp.sum(jnp.where(lane == col, cum[d], 0.0), axis=-1, keepdims=True)
            bsel = jnp.sum(jnp.where(lane == col + n_heads, beta_all, 0.0), axis=-1, keepdims=True)
            gam_ref[d * hb + hh] = jnp.broadcast_to(gsel, ab.shape)
            beta_ref[d * hb + hh] = jnp.broadcast_to(bsel, ab.shape)
            s_ref[d * hb + hh] = s0_ref[d, hh] if has_s0 else jnp.zeros((dk, dk), F32)
    ii = lax.broadcasted_iota(jnp.int32, (CHUNK, LANES), 0)
    lane = lax.broadcasted_iota(jnp.int32, (CHUNK, LANES), 1)
    is_f = lane < CHUNK
    jj = jnp.where(is_f, lane, lane - CHUNK)
    ahead = jnp.where(is_f, ii - jj, jj - ii)
    incl = ahead >= 0
    strict = ahead > 0
    eye = (ii == jj).astype(F32)
    pair_masks = []
    s_blk = 1
    while s_blk < CHUNK:
        pair_masks.append(((ii // (2 * s_blk)) == (jj // (2 * s_blk))) & ((ii // s_blk) != (jj // s_blk)))
        s_blk *= 2

    def block_diag(m):
        zero = jnp.zeros_like(m)
        return jnp.concatenate([jnp.where(is_f, m, zero), jnp.where(is_f, zero, m)], axis=0)

    def diag2(top, bottom):
        return jnp.concatenate([jnp.concatenate([top, jnp.zeros_like(bottom)], axis=1),
                                jnp.concatenate([jnp.zeros_like(top), bottom], axis=1)], axis=0)

    def chunk_rows(s):
        if isinstance(s, int):
            return slice(s * CHUNK, (s + 1) * CHUNK), slice((n_chunks - 1 - s) * CHUNK, (n_chunks - s) * CHUNK)
        return (pl.ds(pl.multiple_of(s * CHUNK, CHUNK), CHUNK),
                pl.ds(pl.multiple_of((n_chunks - 1 - s) * CHUNK, CHUNK), CHUNK))

    def steps(chunks):
        rows = [chunk_rows(s) for s in chunks]
        items = [(c, hh) for c in range(len(chunks)) for hh in range(hb)]
        cs_l = [slice(hh * dk, (hh + 1) * dk) for _, hh in items]
        qf = [qn_ref[rows[c][0], cs] for (c, _), cs in zip(items, cs_l)]
        kf = [kn_ref[rows[c][0], cs] for (c, _), cs in zip(items, cs_l)]
        vf = [vc_ref[rows[c][0], cs] for (c, _), cs in zip(items, cs_l)]
        qb = [qn_ref[rows[c][1], cs] for (c, _), cs in zip(items, cs_l)]
        kb = [kn_ref[rows[c][1], cs] for (c, _), cs in zip(items, cs_l)]
        vb = [vc_ref[rows[c][1], cs] for (c, _), cs in zip(items, cs_l)]
        gf = [gam_ref[hh, rows[c][0], :] for c, hh in items]
        gb = [gam_ref[hb + hh, rows[c][1], :] for c, hh in items]
        bf = [beta_ref[hh, rows[c][0], :] for c, hh in items]
        bb = [beta_ref[hb + hh, rows[c][1], :] for c, hh in items]
        n_items = len(items)

        a_l, x_l, aqk_l = [], [], []
        for it in range(n_items):
            gr = jnp.concatenate([gf[it], gb[it]], axis=0).T[:CHUNK, :]
            diff = jnp.where(is_f, gf[it], gb[it]) - gr
            dec_incl = jnp.where(incl, jnp.exp(jnp.where(incl, diff, 0.0)), 0.0)
            lhs = jnp.concatenate([jnp.concatenate([kf[it], kb[it]], axis=1),
                                   jnp.concatenate([qf[it], qb[it]], axis=1)], axis=0)
            kq = _dot_nt(lhs, diag2(kf[it], kb[it]))
            a = jnp.where(is_f, bf[it], bb[it]) * kq[:CHUNK] * jnp.where(strict, dec_incl, 0.0)
            a = a.astype(BF16).astype(F32)
            a_l.append(a)
            aqk_l.append(kq[CHUNK:] * dec_incl)
            x_l.append(eye - jnp.where(pair_masks[0], a, 0.0))
        for pm in pair_masks[1:]:
            t_l = [_dot(x, block_diag(jnp.where(pm, a, 0.0))) for x, a in zip(x_l, a_l)]
            x_l = [x - _dot(tx, block_diag(x)) for x, tx in zip(x_l, t_l)]
        egf = [jnp.exp(g) for g in gf]
        egb = [jnp.exp(g) for g in gb]
        wu_l = []
        for it in range(n_items):
            r_f = jnp.concatenate([kf[it] * (bf[it] * egf[it]), vf[it] * bf[it]], axis=1).astype(BF16)
            r_b = jnp.concatenate([kb[it] * (bb[it] * egb[it]), vb[it] * bb[it]], axis=1).astype(BF16)
            wu_l.append(_dot(x_l[it], diag2(r_f, r_b)))
        for c in range(len(chunks)):
            its = [it for it, (ci, _) in enumerate(items) if ci == c]
            stf = {it: s_ref[items[it][1]] for it in its}
            stb = {it: s_ref[hb + items[it][1]] for it in its}
            wqf = {it: _dot(jnp.concatenate([wu_l[it][:, :dk], qf[it] * egf[it]], axis=0), stf[it]) for it in its}
            wqb = {it: _dot(jnp.concatenate([wu_l[it][:, 2 * dk:3 * dk], qb[it] * egb[it]], axis=0), stb[it])
                   for it in its}
            for it in its:
                hh = items[it][1]
                vn_f = wu_l[it][:, dk:2 * dk] - wqf[it][:CHUNK]
                vn_b = wu_l[it][:, 3 * dk:] - wqb[it][:CHUNK]
                glf = gf[it][CHUNK - 1:CHUNK, :]
                glb = gb[it][0:1, :]
                kd = jnp.concatenate([kf[it] * jnp.exp(glf - gf[it]), kb[it] * jnp.exp(glb - gb[it])], axis=0)
                res = _dot(jnp.concatenate([aqk_l[it], kd.T], axis=0), diag2(vn_f, vn_b))
                o_ref[0, rows[c][0], cs_l[it]] = wqf[it][CHUNK:] + res[:CHUNK, :dk]
                o_ref[1, rows[c][1], cs_l[it]] = wqb[it][CHUNK:] + res[:CHUNK, dk:]
                s_ref[hh] = stf[it] * jnp.exp(glf) + res[CHUNK:, :dk]
                s_ref[hb + hh] = stb[it] * jnp.exp(glb) + res[CHUNK:, dk:]

    if n_chunks <= UNROLL_CHUNKS:
        for s in range(n_chunks):
            steps([s])
    else:
        assert n_chunks % LOOP_CHUNKS == 0

        def body(i, carry):
            steps([i * LOOP_CHUNKS + k for k in range(LOOP_CHUNKS)])
            return carry

        lax.fori_loop(0, n_chunks // LOOP_CHUNKS, body, 0)

    for hh in range(hb):
        cs = slice(hh * dk, (hh + 1) * dk)
        o = o_ref[0, :, cs] + o_ref[1, :, cs]
        o = o * lax.rsqrt(jnp.mean(o * o, axis=-1, keepdims=True) + RMS_EPS) * ng_ref[...] * _silu(z_ref[:, cs])
        og_ref[:, cs] = o.astype(og_ref.dtype)
    if emit_state:
        for d in range(2):
            for hh in range(hb):
                sout_ref[d, hh] = s_ref[d * hb + hh]


def _gdn_core(proj, ab, conv_w, alog_row, dtb_row, ng_row, s0, og_prev, state_prev, *, row0, n_seq, t, hb,
              n_heads, state_slot):
    dk = LANES
    wb = hb * dk
    n_hg = n_heads // hb
    rb = row0 // t
    has_s0 = s0 is not None
    emit_state = state_slot is not None

    def col_spec(part):
        return pl.BlockSpec((t, wb), lambda b, h: (rb + b, part * n_hg + h))

    def conv_spec(part):
        return pl.BlockSpec((CONV_K, wb), lambda b, h: (0, part * n_hg + h))

    row_spec = pl.BlockSpec((1, LANES), lambda b, h: (0, 0))
    in_specs = [col_spec(0), col_spec(1), col_spec(2), col_spec(3),
                pl.BlockSpec((t, LANES), lambda b, h: (rb + b, 0)),
                conv_spec(0), conv_spec(1), conv_spec(2), row_spec, row_spec, row_spec]
    args = [proj, proj, proj, proj, ab, conv_w, conv_w, conv_w, alog_row, dtb_row, ng_row]
    if has_s0:
        in_specs.append(pl.BlockSpec((None, 2, hb, dk, dk), lambda b, h: (b, 0, h, 0, 0)))
        args.append(s0)
    out_specs = [pl.BlockSpec((t, wb), lambda b, h: (rb + b, h))]
    out_shape = [jax.ShapeDtypeStruct((proj.shape[0], n_heads * dk), BF16)]
    if emit_state:
        mi, n_mix = state_slot
        out_specs.append(pl.BlockSpec((None, None, 2, hb, dk, dk), lambda b, h: (b, mi, 0, h, 0, 0)))
        out_shape.append(jax.ShapeDtypeStruct((n_seq, n_mix, 2, n_heads, dk, dk), F32))
    aliases = {}
    for out_idx, prev in enumerate((og_prev, state_prev)):
        if prev is not None:
            aliases[len(args)] = out_idx
            in_specs.append(pl.BlockSpec(memory_space=pl.ANY))
            args.append(prev)
    return pl.pallas_call(
        functools.partial(_gdn_kernel, t=t, hb=hb, n_heads=n_heads, has_s0=has_s0, n_alias=len(aliases),
                          emit_state=emit_state),
        grid=(n_seq, n_hg),
        in_specs=in_specs,
        out_specs=out_specs,
        out_shape=out_shape,
        input_output_aliases=aliases,
        scratch_shapes=[
            pltpu.VMEM((t, wb), F32), pltpu.VMEM((t, wb), F32), pltpu.VMEM((t, wb), F32),
            pltpu.VMEM((2 * hb, t, LANES), F32), pltpu.VMEM((2 * hb, t, LANES), F32),
            pltpu.VMEM((2 * hb, dk, dk), F32), pltpu.VMEM((2, t, wb), F32),
            pltpu.VMEM((t + 2 * SUBLANES, wb), F32),
        ],
        compiler_params=_cparams("parallel", "parallel"),
        name="gdn_core",
    )(*args)


def _out_kernel(x_ref, mod_ref, a_ref, w_ref, o_ref):
    y = jnp.dot(a_ref[...], w_ref[...], preferred_element_type=F32)
    o_ref[...] = x_ref[...] + mod_ref[5:6, :] * y


def _gdn_out(x, mods_l, og, w_out, row_of_tile, tm):
    m, d = x.shape
    kdim = og.shape[1]
    return pl.pallas_call(
        _out_kernel,
        grid=(m // tm,),
        in_specs=[
            pl.BlockSpec((tm, d), lambda i: (i, 0)),
            pl.BlockSpec((None, N_MOD, d), lambda i: (row_of_tile(i), 0, 0)),
            pl.BlockSpec((tm, kdim), lambda i: (i, 0)),
            pl.BlockSpec((kdim, d), lambda i: (0, 0)),
        ],
        out_specs=pl.BlockSpec((tm, d), lambda i: (i, 0)),
        out_shape=jax.ShapeDtypeStruct((m, d), F32),
        compiler_params=_cparams("parallel"),
        name="gdn_out",
    )(x, mods_l, og, w_out)


def _pool_kernel(x_ref, mod_ref, g_ref, w_ref, sc_ref, *rest, t):
    o_ref = rest[-1]
    x = x_ref[...]
    rstd = lax.rsqrt(jnp.mean(x * x, axis=-1, keepdims=True) + RMS_EPS)
    pg = w_ref.shape[1]
    edge = max(POOL_WINDOWS)
    zeros = jnp.zeros((edge, pg), F32)
    row = lax.broadcasted_iota(jnp.int32, (t, pg), 0)
    for gi, win in enumerate(POOL_WINDOWS):
        cs = slice(gi * pg, (gi + 1) * pg)
        xg = x_ref[:, cs]
        h = (xg * rstd * g_ref[:, cs]) * (1.0 + mod_ref[4:5, cs]) + mod_ref[3:4, cs]
        half = win // 2
        n = t + 2 * edge
        acc = jnp.concatenate([zeros, h, zeros], axis=0)
        s = 1
        while s < win:
            acc = acc + pltpu.roll(acc, n - s, 0)
            s *= 2
        acc = acc[edge - half:edge - half + t]
        cnt = (jnp.minimum(row + (win - half), t) - jnp.maximum(row - half, 0)).astype(F32)
        diff = (acc / cnt - h).astype(BF16)
        y = jnp.dot(diff, w_ref[gi], preferred_element_type=F32) * sc_ref[:, cs]
        o_ref[:, cs] = xg + mod_ref[5:6, cs] * y


def _pool(x, mods_l, g, w, scale, out_prev, *, row0, n_seq, t, mod_row0, mod_per_seq):
    m, d = x.shape
    ng, pg, _ = w.shape
    assert ng == len(POOL_WINDOWS) and ng * pg == d
    rb = row0 // t
    mod_idx = (lambda b: mod_row0 + b) if mod_per_seq else (lambda b: mod_row0)
    in_specs = [
        pl.BlockSpec((t, d), lambda b: (rb + b, 0)),
        pl.BlockSpec((None, N_MOD, d), lambda b: (mod_idx(b), 0, 0)),
        pl.BlockSpec((1, d), lambda b: (0, 0)),
        pl.BlockSpec((ng, pg, pg), lambda b: (0, 0, 0)),
        pl.BlockSpec((1, d), lambda b: (0, 0)),
    ]
    args = [x, mods_l, g, w, scale]
    aliases = {}
    if out_prev is not None:
        aliases[len(args)] = 0
        in_specs.append(pl.BlockSpec(memory_space=pl.ANY))
        args.append(out_prev)
    return pl.pallas_call(
        functools.partial(_pool_kernel, t=t),
        grid=(n_seq,),
        in_specs=in_specs,
        out_specs=pl.BlockSpec((t, d), lambda b: (rb + b, 0)),
        out_shape=jax.ShapeDtypeStruct((m, d), F32),
        input_output_aliases=aliases,
        compiler_params=_cparams("parallel"),
        name="pool",
    )(*args)


def _final_kernel(x_ref, g_ref, o_ref):
    x = x_ref[...]
    o_ref[...] = x * lax.rsqrt(jnp.mean(x * x, axis=-1, keepdims=True) + RMS_EPS) * g_ref[...]


def _final_norm(x, g, tm, row0, rows):
    _, d = x.shape
    rb = row0 // tm
    return pl.pallas_call(
        _final_kernel,
        grid=(rows // tm,),
        in_specs=[pl.BlockSpec((tm, d), lambda i: (rb + i, 0)), pl.BlockSpec((1, d), lambda i: (0, 0))],
        out_specs=pl.BlockSpec((tm, d), lambda i: (i, 0)),
        out_shape=jax.ShapeDtypeStruct((rows, d), F32),
        compiler_params=_cparams("parallel"),
        name="final_norm",
    )(x, g)


def _place_cols(vals, n_heads):
    row = jnp.zeros((LANES,), F32)
    for d in range(2):
        row = lax.dynamic_update_slice(row, vals[d].astype(F32), (d * 2 * n_heads,))
    return row.reshape(1, LANES)


def kernel(x_prompt, x_sample, state_gdn, c, c_ctx, w_mod, b_mod, norm_g, ffn_w_in, ffn_w_out, gdn_w_in, gdn_conv, gdn_a_log, gdn_dt_bias, gdn_norm_g, gdn_w_out, pool_w, pool_scale, final_g):
    batch, seq, d = x_prompt.shape
    dec_batch, dec_seq, _ = x_sample.shape
    depth = w_mod.shape[0]
    n_heads = gdn_a_log.shape[-1]
    dk = LANES
    qk = n_heads * dk
    f = ffn_w_out.shape[2]
    fp = FF_TILE * ((f + FF_TILE - 1) // FF_TILE)
    m_p = batch * seq
    m_s = dec_batch * dec_seq
    assert 1 + dec_batch <= MOD_ROWS and gdn_w_in.shape[2] == 4 * qk + 4 * n_heads and 4 * n_heads <= LANES
    assert m_p % dec_seq == 0

    def token_tile(candidates):
        return next(c_ for c_ in candidates if m_p % c_ == 0 and dec_seq % c_ == 0)

    def mod_row_of_tile(tile):
        n_prompt_tiles = m_p // tile
        return lambda i: jnp.where(i < n_prompt_tiles, 0, 1 + ((i - n_prompt_tiles) * tile) // dec_seq)

    tm = token_tile((512, 256, 128, 64))
    row_of_tile = mod_row_of_tile(tm)
    tm_proj = token_tile((1024, 512, 256, 128, 64))

    c8 = jnp.zeros((MOD_ROWS, d), F32).at[0].set(c_ctx).at[1:1 + dec_batch].set(c)
    mods = _modulation(c8, w_mod, b_mod).reshape(depth, MOD_ROWS, N_MOD, d)

    x = _token_stream(x_prompt.reshape(m_p, d), x_sample.reshape(m_s, d),
                      _grid_pos_embed(dec_seq, d, x_sample.dtype), tm)

    w_gate_p, w_up_p, w_out_p = _prep_ffn_weights(ffn_w_in, ffn_w_out, fp)

    n_gdn = (depth + 1) // 2
    new_state = None
    for l in range(depth):
        mods_l = mods[l]
        ng = norm_g[l]

        def ffn(x, i, sub):
            return _ffn(x, mods_l, ng[sub:sub + 1], w_gate_p, w_up_p, w_out_p, l, i, mod_row_of_tile(tm_proj),
                        tm_proj, sub)

        x = ffn(x, 0, 0)
        mi = l // 2
        if l % 2 == 0:
            w_in = gdn_w_in[mi]
            w_main = w_in[:, :4 * qk].astype(BF16)
            w_ab = jnp.pad(w_in[:, 4 * qk:], ((0, 0), (0, LANES - 4 * n_heads))).astype(BF16)
            proj, ab = _gdn_proj(x, mods_l, ng[1:2], w_main, w_ab, mod_row_of_tile(tm_proj), tm_proj)
            alog_row = _place_cols(gdn_a_log[mi], n_heads)
            dtb_row = _place_cols(gdn_dt_bias[mi], n_heads)
            ng_row = gdn_norm_g[mi].reshape(1, dk).astype(F32)
            core = functools.partial(_gdn_core, proj, ab, gdn_conv[mi], alog_row, dtb_row, ng_row, n_heads=n_heads)
            og, new_state = core(None, None, new_state, row0=0, n_seq=batch, t=seq, hb=min(16, n_heads),
                                 state_slot=(mi, n_gdn))
            (og,) = core(state_gdn[:, mi], og, None, row0=m_p, n_seq=dec_batch, t=dec_seq, hb=min(4, n_heads),
                         state_slot=None)
            x = _gdn_out(x, mods_l, og, gdn_w_out[mi].astype(BF16), row_of_tile, tm)
        else:
            pw = pool_w[mi].astype(BF16)
            ps = pool_scale[mi].reshape(1, d)
            pool = functools.partial(_pool, x, mods_l, ng[1:2], pw, ps)
            x_new = pool(None, row0=0, n_seq=batch, t=seq, mod_row0=0, mod_per_seq=False)
            x = pool(x_new, row0=m_p, n_seq=dec_batch, t=dec_seq, mod_row0=1, mod_per_seq=True)
        x = ffn(x, 1, 2)

    fg = final_g.reshape(1, d)
    y_prompt = _final_norm(x, fg, tm, 0, m_p).reshape(batch, seq, d)
    y_sample = _final_norm(x, fg, tm, m_p, m_s).reshape(dec_batch, dec_seq, d)
    return (y_prompt, y_sample, new_state.astype(state_gdn.dtype))
```

```python
import functools
import math

import jax
import jax.numpy as jnp
from jax import lax
from jax.experimental import pallas as pl
from jax.experimental.pallas import tpu as pltpu

F32 = jnp.float32
BF16 = jnp.bfloat16

RMS_EPS = 1e-6
L2_EPS = 1e-6
CHUNK = 64
CONV_K = 5
CONV_PAD = CONV_K // 2
POOL_WINDOWS = (2, 4, 8, 16)
GRID_W = 64
POS_BASE = 10000.0
N_MOD = 9
MOD_ROWS = 8
LANES = 128
SUBLANES = 8
FF_TILE = 512
UNROLL_CHUNKS = 4
LOOP_CHUNKS = 4
VMEM_LIMIT = 60 * 1024 * 1024


def _cparams(*sem):
    return pltpu.CompilerParams(dimension_semantics=sem, vmem_limit_bytes=VMEM_LIMIT)


def _sigmoid(x):
    return jax.nn.sigmoid(x)


def _silu(x):
    return x * _sigmoid(x)


def _dot(a, b):
    return jnp.dot(a.astype(BF16), b.astype(BF16), preferred_element_type=F32)


def _dot_nt(a, b):
    return lax.dot_general(a.astype(BF16), b.astype(BF16), (((1,), (1,)), ((), ())), preferred_element_type=F32)


def _split(a):
    hi = a.astype(BF16)
    lo = (a - hi.astype(F32)).astype(BF16)
    return hi, lo


def _mod_norm(x, g, scale, shift):
    ms = jnp.mean(x * x, axis=-1, keepdims=True)
    return (x * lax.rsqrt(ms + RMS_EPS) * g) * (1.0 + scale) + shift


MOD_BUFFERS = 3


def _mod_kernel(c_ref, w_hbm, b_ref, o_ref, wbuf, sems, *, nj, tn):
    step = pl.program_id(0)
    n_steps = pl.num_programs(0)

    def tile_copy(t):
        slot = t % MOD_BUFFERS
        return pltpu.make_async_copy(w_hbm.at[t // nj, :, pl.ds(pl.multiple_of((t % nj) * tn, tn), tn)],
                                     wbuf.at[slot], sems.at[slot])

    @pl.when(step == 0)
    def _():
        for t in range(MOD_BUFFERS - 1):
            tile_copy(t).start()

    @pl.when(step + MOD_BUFFERS - 1 < n_steps)
    def _():
        tile_copy(step + MOD_BUFFERS - 1).start()

    tile_copy(step).wait()
    s_hi, s_lo = _split(_silu(c_ref[...]))
    w = wbuf[step % MOD_BUFFERS].astype(BF16)
    d = functools.partial(jnp.dot, preferred_element_type=F32)
    o_ref[...] = d(s_hi, w) + d(s_lo, w) + b_ref[...]


def _modulation(c8, w_mod, b_mod):
    depth, d, nd = w_mod.shape
    tn = next(c_ for c_ in (1024, 512, 256, 128) if nd % c_ == 0)
    nj = nd // tn
    assert depth * nj >= MOD_BUFFERS - 1
    return pl.pallas_call(
        functools.partial(_mod_kernel, nj=nj, tn=tn),
        grid=(depth * nj,),
        in_specs=[
            pl.BlockSpec((MOD_ROWS, d), lambda s: (0, 0)),
            pl.BlockSpec(memory_space=pl.ANY),
            pl.BlockSpec((None, 1, tn), lambda s: (s // nj, 0, s % nj)),
        ],
        out_specs=pl.BlockSpec((None, MOD_ROWS, tn), lambda s: (s // nj, 0, s % nj)),
        out_shape=jax.ShapeDtypeStruct((depth, MOD_ROWS, nd), F32),
        scratch_shapes=[pltpu.VMEM((MOD_BUFFERS, d, tn), F32), pltpu.SemaphoreType.DMA((MOD_BUFFERS,))],
        compiler_params=_cparams("arbitrary"),
        name="modulation",
    )(c8, w_mod, b_mod.reshape(depth, 1, nd))


def _tokens_kernel(xp_ref, xs_ref, p_ref, o_ref, *, n_prompt_tiles):
    i = pl.program_id(0)

    @pl.when(i < n_prompt_tiles)
    def _():
        o_ref[...] = xp_ref[...]

    @pl.when(i >= n_prompt_tiles)
    def _():
        o_ref[...] = xs_ref[...] + p_ref[...]


def _token_stream(xp, xs, pos, ta):
    m_p, d = xp.shape
    m_s = xs.shape[0]
    t = pos.shape[0]
    n_p = m_p // ta
    return pl.pallas_call(
        functools.partial(_tokens_kernel, n_prompt_tiles=n_p),
        grid=((m_p + m_s) // ta,),
        in_specs=[
            pl.BlockSpec((ta, d), lambda i: (jnp.minimum(i, n_p - 1), 0)),
            pl.BlockSpec((ta, d), lambda i: (jnp.maximum(i - n_p, 0), 0)),
            pl.BlockSpec((ta, d), lambda i: ((jnp.maximum(i - n_p, 0) * ta % t) // ta, 0)),
        ],
        out_specs=pl.BlockSpec((ta, d), lambda i: (i, 0)),
        out_shape=jax.ShapeDtypeStruct((m_p + m_s, d), xp.dtype),
        compiler_params=_cparams("parallel"),
        name="token_stream",
    )(xp, xs, pos)


def _grid_pos_embed(n_tok, d, dtype):
    idx = jnp.arange(n_tok)
    r = (idx // GRID_W).astype(F32)
    col = (idx % GRID_W).astype(F32)
    n_freq = d // 4
    freqs = jnp.exp(-math.log(POS_BASE) * jnp.arange(n_freq, dtype=F32) / n_freq)
    ar = r[:, None] * freqs
    ac = col[:, None] * freqs
    return jnp.concatenate([jnp.sin(ar), jnp.cos(ar), jnp.sin(ac), jnp.cos(ac)], axis=-1).astype(dtype)


def _prep_in_kernel(w_ref, g_ref, u_ref, *, f):
    pad = jnp.zeros((g_ref.shape[0], g_ref.shape[1] - f), BF16)
    g_ref[:, :f] = w_ref[:, :f].astype(BF16)
    u_ref[:, :f] = w_ref[:, f:].astype(BF16)
    g_ref[:, f:] = pad
    u_ref[:, f:] = pad


def _prep_out_kernel(w_ref, o_ref, *, f):
    row = pl.program_id(2) * FF_TILE + lax.broadcasted_iota(jnp.int32, w_ref.shape, 0)
    o_ref[...] = jnp.where(row < f, w_ref[...], 0.0).astype(BF16)


def _prep_ffn_weights(w_in, w_out, fp):
    depth, two, d, f2 = w_in.shape
    f = f2 // 2
    assert f % LANES == 0
    rows = 256
    spec_w = pl.BlockSpec((None, None, rows, f2), lambda l, k, r: (l, k, r, 0))
    spec_p = pl.BlockSpec((None, None, rows, fp), lambda l, k, r: (l, k, r, 0))
    padded = jax.ShapeDtypeStruct((depth, two, d, fp), BF16)
    w_gate_p, w_up_p = pl.pallas_call(
        functools.partial(_prep_in_kernel, f=f),
        grid=(depth, two, d // rows),
        in_specs=[spec_w],
        out_specs=[spec_p, spec_p],
        out_shape=[padded, padded],
        compiler_params=_cparams("parallel", "parallel", "parallel"),
        name="prep_w_in",
    )(w_in)
    w_out_p = pl.pallas_call(
        functools.partial(_prep_out_kernel, f=f),
        grid=(depth, two, fp // FF_TILE),
        in_specs=[pl.BlockSpec((None, None, FF_TILE, d), lambda l, k, j: (l, k, j, 0))],
        out_specs=pl.BlockSpec((None, None, FF_TILE, d), lambda l, k, j: (l, k, j, 0)),
        out_shape=jax.ShapeDtypeStruct((depth, two, fp, d), BF16),
        compiler_params=_cparams("parallel", "parallel", "parallel"),
        name="prep_w_out",
    )(w_out)
    return w_gate_p, w_up_p, w_out_p


def _ffn_kernel(x_ref, mod_ref, g_ref, wg_ref, wu_ref, wo_ref, o_ref, h_ref, *, sub):
    j = pl.program_id(1)
    last = pl.num_programs(1) - 1

    def partial_out(h):
        gate = jnp.dot(h, wg_ref[...], preferred_element_type=F32)
        up = jnp.dot(h, wu_ref[...], preferred_element_type=F32)
        act = (_silu(gate) * up).astype(BF16)
        return jnp.dot(act, wo_ref[...], preferred_element_type=F32)

    @pl.when(j == 0)
    def _():
        h = _mod_norm(x_ref[...], g_ref[...], mod_ref[3 * sub + 1:3 * sub + 2, :], mod_ref[3 * sub:3 * sub + 1, :])
        h = h.astype(BF16)
        h_ref[...] = h
        o_ref[...] = partial_out(h)

    @pl.when((j > 0) & (j < last))
    def _():
        o_ref[...] += partial_out(h_ref[...])

    @pl.when(j == last)
    def _():
        y = o_ref[...] + partial_out(h_ref[...])
        o_ref[...] = x_ref[...] + (0.5 * mod_ref[3 * sub + 2:3 * sub + 3, :]) * y


def _ffn(x, mods_l, g, w_gate_p, w_up_p, w_out_p, l, k, row_of_tile, tm, sub):
    m, d = x.shape
    fp = w_out_p.shape[2]
    nj = fp // FF_TILE
    assert nj >= 2
    return pl.pallas_call(
        functools.partial(_ffn_kernel, sub=sub),
        grid=(m // tm, nj),
        in_specs=[
            pl.BlockSpec((tm, d), lambda i, j: (i, 0)),
            pl.BlockSpec((None, N_MOD, d), lambda i, j: (row_of_tile(i), 0, 0)),
            pl.BlockSpec((1, d), lambda i, j: (0, 0)),
            pl.BlockSpec((None, None, d, FF_TILE), lambda i, j: (l, k, 0, j)),
            pl.BlockSpec((None, None, d, FF_TILE), lambda i, j: (l, k, 0, j)),
            pl.BlockSpec((None, None, FF_TILE, d), lambda i, j: (l, k, j, 0)),
        ],
        out_specs=pl.BlockSpec((tm, d), lambda i, j: (i, 0)),
        out_shape=jax.ShapeDtypeStruct((m, d), F32),
        scratch_shapes=[pltpu.VMEM((tm, d), BF16)],
        compiler_params=_cparams("parallel", "arbitrary"),
        name="ffn",
    )(x, mods_l, g, w_gate_p, w_up_p, w_out_p)


def _proj_kernel(x_ref, mod_ref, g_ref, w_ref, wab_ref, o_ref, ab_ref, h_ref):
    j = pl.program_id(1)

    @pl.when(j == 0)
    def _():
        h = _mod_norm(x_ref[...], g_ref[...], mod_ref[4:5, :], mod_ref[3:4, :]).astype(BF16)
        h_ref[...] = h
        o_ref[...] = jnp.dot(h, w_ref[...], preferred_element_type=F32)
        ab_ref[...] = jnp.dot(h, wab_ref[...], preferred_element_type=F32)

    @pl.when(j > 0)
    def _():
        o_ref[...] = jnp.dot(h_ref[...], w_ref[...], preferred_element_type=F32)


def _gdn_proj(x, mods_l, g, w_main, w_ab, row_of_tile, tm):
    m, d = x.shape
    n = w_main.shape[1]
    tn = next(c_ for c_ in (2048, 1024, 512, 256, 128) if n % c_ == 0)
    return pl.pallas_call(
        _proj_kernel,
        grid=(m // tm, n // tn),
        in_specs=[
            pl.BlockSpec((tm, d), lambda i, j: (i, 0)),
            pl.BlockSpec((None, N_MOD, d), lambda i, j: (row_of_tile(i), 0, 0)),
            pl.BlockSpec((1, d), lambda i, j: (0, 0)),
            pl.BlockSpec((d, tn), lambda i, j: (0, j)),
            pl.BlockSpec((d, LANES), lambda i, j: (0, 0)),
        ],
        out_specs=[pl.BlockSpec((tm, tn), lambda i, j: (i, j)), pl.BlockSpec((tm, LANES), lambda i, j: (i, 0))],
        out_shape=[jax.ShapeDtypeStruct((m, n), F32), jax.ShapeDtypeStruct((m, LANES), F32)],
        scratch_shapes=[pltpu.VMEM((tm, d), BF16)],
        compiler_params=_cparams("parallel", "arbitrary"),
        name="gdn_proj",
    )(x, mods_l, g, w_main, w_ab)


def _row_sum_lanes(x):
    return jnp.dot(x.astype(BF16), jnp.ones((LANES, LANES), BF16), preferred_element_type=F32)


def _conv_silu(x_ref, w_ref, o_ref, pad_ref, t):
    zeros = jnp.zeros((SUBLANES, x_ref.shape[1]), F32)
    pad_ref[0:SUBLANES, :] = zeros
    pad_ref[t + SUBLANES:t + 2 * SUBLANES, :] = zeros
    pad_ref[SUBLANES:t + SUBLANES, :] = x_ref[...]
    acc = x_ref[...] * w_ref[CONV_PAD:CONV_PAD + 1, :]
    for j in range(CONV_K):
        if j != CONV_PAD:
            r0 = SUBLANES + j - CONV_PAD
            acc = acc + pad_ref[r0:r0 + t, :] * w_ref[j:j + 1, :]
    o_ref[...] = _silu(acc)


def _chunk_cumsum(g, t, reverse):
    pos = lax.broadcasted_iota(jnp.int32, g.shape, 0) % CHUNK
    s = 1
    while s < CHUNK:
        if reverse:
            g = g + jnp.where(pos < CHUNK - s, pltpu.roll(g, t - s, 0), 0.0)
        else:
            g = g + jnp.where(pos >= s, pltpu.roll(g, s, 0), 0.0)
        s *= 2
    return g


def _gdn_kernel(*refs, t, hb, n_heads, has_s0, n_alias, emit_state):
    (q_ref, k_ref, v_ref, z_ref, ab_ref, cq_ref, ck_ref, cv_ref, alog_ref, dtb_ref, ng_ref), refs = refs[:11], refs[11:]
    if has_s0:
        s0_ref, refs = refs[0], refs[1:]
    refs = refs[n_alias:]
    og_ref, refs = refs[0], refs[1:]
    if emit_state:
        sout_ref, refs = refs[0], refs[1:]
    qn_ref, kn_ref, vc_ref, gam_ref, beta_ref, s_ref, o_ref, pad_ref = refs

    dk = LANES
    n_chunks = t // CHUNK
    hg = pl.program_id(1)

    _conv_silu(q_ref, cq_ref, qn_ref, pad_ref, t)
    _conv_silu(k_ref, ck_ref, kn_ref, pad_ref, t)
    _conv_silu(v_ref, cv_ref, vc_ref, pad_ref, t)
    for hh in range(hb):
        cs = slice(hh * dk, (hh + 1) * dk)
        qh = qn_ref[:, cs]
        kh = kn_ref[:, cs]
        qn_ref[:, cs] = qh * lax.rsqrt(_row_sum_lanes(qh * qh) + L2_EPS) * (dk ** -0.5)
        kn_ref[:, cs] = kh * lax.rsqrt(_row_sum_lanes(kh * kh) + L2_EPS)

    ab = ab_ref[...]
    pre = ab + dtb_ref[...]
    softplus = jnp.maximum(pre, 0.0) + jnp.log1p(jnp.exp(-jnp.abs(pre)))
    g_all = -jnp.exp(alog_ref[...]) * softplus
    beta_all = _sigmoid(ab)
    cum = (_chunk_cumsum(g_all, t, False), _chunk_cumsum(g_all, t, True))
    lane = lax.broadcasted_iota(jnp.int32, ab.shape, 1)
    for d in range(2):
        for hh in range(hb):
            col = d * 2 * n_heads + hg * hb + hh
            gsel = jnp.sum(jnp.where(lane == col, cum[d], 0.0), axis=-1, keepdims=True)
            bsel = jnp.sum(jnp.where(lane == col + n_heads, beta_all, 0.0), axis=-1, keepdims=True)
            gam_ref[d * hb + hh] = jnp.broadcast_to(gsel, ab.shape)
            beta_ref[d * hb + hh] = jnp.broadcast_to(bsel, ab.shape)
            s_ref[d * hb + hh] = s0_ref[d, hh] if has_s0 else jnp.zeros((dk, dk), F32)
    ii = lax.broadcasted_iota(jnp.int32, (CHUNK, LANES), 0)
    lane = lax.broadcasted_iota(jnp.int32, (CHUNK, LANES), 1)
    is_f = lane < CHUNK
    jj = jnp.where(is_f, lane, lane - CHUNK)
    ahead = jnp.where(is_f, ii - jj, jj - ii)
    incl = ahead >= 0
    strict = ahead > 0
    eye = (ii == jj).astype(F32)
    pair_masks = []
    s_blk = 1
    while s_blk < CHUNK:
        pair_masks.append(((ii // (2 * s_blk)) == (jj // (2 * s_blk))) & ((ii // s_blk) != (jj // s_blk)))
        s_blk *= 2

    def block_diag(m):
        zero = jnp.zeros_like(m)
        return jnp.concatenate([jnp.where(is_f, m, zero), jnp.where(is_f, zero, m)], axis=0)

    def diag2(top, bottom):
        return jnp.concatenate([jnp.concatenate([top, jnp.zeros_like(bottom)], axis=1),
                                jnp.concatenate([jnp.zeros_like(top), bottom], axis=1)], axis=0)

    def chunk_rows(s):
        if isinstance(s, int):
            return slice(s * CHUNK, (s + 1) * CHUNK), slice((n_chunks - 1 - s) * CHUNK, (n_chunks - s) * CHUNK)
        return (pl.ds(pl.multiple_of(s * CHUNK, CHUNK), CHUNK),
                pl.ds(pl.multiple_of((n_chunks - 1 - s) * CHUNK, CHUNK), CHUNK))

    def steps(chunks):
        rows = [chunk_rows(s) for s in chunks]
        items = [(c, hh) for c in range(len(chunks)) for hh in range(hb)]
        cs_l = [slice(hh * dk, (hh + 1) * dk) for _, hh in items]
        qf = [qn_ref[rows[c][0], cs] for (c, _), cs in zip(items, cs_l)]
        kf = [kn_ref[rows[c][0], cs] for (c, _), cs in zip(items, cs_l)]
        vf = [vc_ref[rows[c][0], cs] for (c, _), cs in zip(items, cs_l)]
        qb = [qn_ref[rows[c][1], cs] for (c, _), cs in zip(items, cs_l)]
        kb = [kn_ref[rows[c][1], cs] for (c, _), cs in zip(items, cs_l)]
        vb = [vc_ref[rows[c][1], cs] for (c, _), cs in zip(items, cs_l)]
        gf = [gam_ref[hh, rows[c][0], :] for c, hh in items]
        gb = [gam_ref[hb + hh, rows[c][1], :] for c, hh in items]
        bf = [beta_ref[hh, rows[c][0], :] for c, hh in items]
        bb = [beta_ref[hb + hh, rows[c][1], :] for c, hh in items]
        n_items = len(items)

        a_l, x_l, aqk_l = [], [], []
        for it in range(n_items):
            gr = jnp.concatenate([gf[it], gb[it]], axis=0).T[:CHUNK, :]
            diff = jnp.where(is_f, gf[it], gb[it]) - gr
            dec_incl = jnp.where(incl, jnp.exp(jnp.where(incl, diff, 0.0)), 0.0)
            lhs = jnp.concatenate([jnp.concatenate([kf[it], kb[it]], axis=1),
                                   jnp.concatenate([qf[it], qb[it]], axis=1)], axis=0)
            kq = _dot_nt(lhs, diag2(kf[it], kb[it]))
            a = jnp.where(is_f, bf[it], bb[it]) * kq[:CHUNK] * jnp.where(strict, dec_incl, 0.0)
            a = a.astype(BF16).astype(F32)
            a_l.append(a)
            aqk_l.append(kq[CHUNK:] * dec_incl)
            x_l.append(eye - jnp.where(pair_masks[0], a, 0.0))
        for pm in pair_masks[1:]:
            t_l = [_dot(x, block_diag(jnp.where(pm, a, 0.0))) for x, a in zip(x_l, a_l)]
            x_l = [x - _dot(tx, block_diag(x)) for x, tx in zip(x_l, t_l)]
        egf = [jnp.exp(g) for g in gf]
        egb = [jnp.exp(g) for g in gb]
        wu_l = []
        for it in range(n_items):
            r_f = jnp.concatenate([kf[it] * (bf[it] * egf[it]), vf[it] * bf[it]], axis=1).astype(BF16)
            r_b = jnp.concatenate([kb[it] * (bb[it] * egb[it]), vb[it] * bb[it]], axis=1).astype(BF16)
            wu_l.append(_dot(x_l[it], diag2(r_f, r_b)))
        for c in range(len(chunks)):
            its = [it for it, (ci, _) in enumerate(items) if ci == c]
            stf = {it: s_ref[items[it][1]] for it in its}
            stb = {it: s_ref[hb + items[it][1]] for it in its}
            wqf = {it: _dot(jnp.concatenate([wu_l[it][:, :dk], qf[it] * egf[it]], axis=0), stf[it]) for it in its}
            wqb = {it: _dot(jnp.concatenate([wu_l[it][:, 2 * dk:3 * dk], qb[it] * egb[it]], axis=0), stb[it])
                   for it in its}
            for it in its:
                hh = items[it][1]
                vn_f = wu_l[it][:, dk:2 * dk] - wqf[it][:CHUNK]
                vn_b = wu_l[it][:, 3 * dk:] - wqb[it][:CHUNK]
                glf = gf[it][CHUNK - 1:CHUNK, :]
                glb = gb[it][0:1, :]
                kd = jnp.concatenate([kf[it] * jnp.exp(glf - gf[it]), kb[it] * jnp.exp(glb - gb[it])], axis=0)
                res = _dot(jnp.concatenate([aqk_l[it], kd.T], axis=0), diag2(vn_f, vn_b))
                o_ref[0, rows[c][0], cs_l[it]] = wqf[it][CHUNK:] + res[:CHUNK, :dk]
                o_ref[1, rows[c][1], cs_l[it]] = wqb[it][CHUNK:] + res[:CHUNK, dk:]
                s_ref[hh] = stf[it] * jnp.exp(glf) + res[CHUNK:, :dk]
                s_ref[hb + hh] = stb[it] * jnp.exp(glb) + res[CHUNK:, dk:]

    if n_chunks <= UNROLL_CHUNKS:
        for s in range(n_chunks):
            steps([s])
    else:
        assert n_chunks % LOOP_CHUNKS == 0

        def body(i, carry):
            steps([i * LOOP_CHUNKS + k for k in range(LOOP_CHUNKS)])
            return carry

        lax.fori_loop(0, n_chunks // LOOP_CHUNKS, body, 0)

    for hh in range(hb):
        cs = slice(hh * dk, (hh + 1) * dk)
        o = o_ref[0, :, cs] + o_ref[1, :, cs]
        o = o * lax.rsqrt(jnp.mean(o * o, axis=-1, keepdims=True) + RMS_EPS) * ng_ref[...] * _silu(z_ref[:, cs])
        og_ref[:, cs] = o.astype(og_ref.dtype)
    if emit_state:
        for d in range(2):
            for hh in range(hb):
                sout_ref[d, hh] = s_ref[d * hb + hh]


def _gdn_core(proj, ab, conv_w, alog_row, dtb_row, ng_row, s0, og_prev, state_prev, *, row0, n_seq, t, hb,
              n_heads, state_slot):
    dk = LANES
    wb = hb * dk
    n_hg = n_heads // hb
    rb = row0 // t
    has_s0 = s0 is not None
    emit_state = state_slot is not None

    def col_spec(part):
        return pl.BlockSpec((t, wb), lambda b, h: (rb + b, part * n_hg + h))

    def conv_spec(part):
        return pl.BlockSpec((CONV_K, wb), lambda b, h: (0, part * n_hg + h))

    row_spec = pl.BlockSpec((1, LANES), lambda b, h: (0, 0))
    in_specs = [col_spec(0), col_spec(1), col_spec(2), col_spec(3),
                pl.BlockSpec((t, LANES), lambda b, h: (rb + b, 0)),
                conv_spec(0), conv_spec(1), conv_spec(2), row_spec, row_spec, row_spec]
    args = [proj, proj, proj, proj, ab, conv_w, conv_w, conv_w, alog_row, dtb_row, ng_row]
    if has_s0:
        in_specs.append(pl.BlockSpec((None, 2, hb, dk, dk), lambda b, h: (b, 0, h, 0, 0)))
        args.append(s0)
    out_specs = [pl.BlockSpec((t, wb), lambda b, h: (rb + b, h))]
    out_shape = [jax.ShapeDtypeStruct((proj.shape[0], n_heads * dk), BF16)]
    if emit_state:
        mi, n_mix = state_slot
        out_specs.append(pl.BlockSpec((None, None, 2, hb, dk, dk), lambda b, h: (b, mi, 0, h, 0, 0)))
        out_shape.append(jax.ShapeDtypeStruct((n_seq, n_mix, 2, n_heads, dk, dk), F32))
    aliases = {}
    for out_idx, prev in enumerate((og_prev, state_prev)):
        if prev is not None:
            aliases[len(args)] = out_idx
            in_specs.append(pl.BlockSpec(memory_space=pl.ANY))
            args.append(prev)
    return pl.pallas_call(
        functools.partial(_gdn_kernel, t=t, hb=hb, n_heads=n_heads, has_s0=has_s0, n_alias=len(aliases),
                          emit_state=emit_state),
        grid=(n_seq, n_hg),
        in_specs=in_specs,
        out_specs=out_specs,
        out_shape=out_shape,
        input_output_aliases=aliases,
        scratch_shapes=[
            pltpu.VMEM((t, wb), F32), pltpu.VMEM((t, wb), F32), pltpu.VMEM((t, wb), F32),
            pltpu.VMEM((2 * hb, t, LANES), F32), pltpu.VMEM((2 * hb, t, LANES), F32),
            pltpu.VMEM((2 * hb, dk, dk), F32), pltpu.VMEM((2, t, wb), F32),
            pltpu.VMEM((t + 2 * SUBLANES, wb), F32),
        ],
        compiler_params=_cparams("parallel", "parallel"),
        name="gdn_core",
    )(*args)


def _out_kernel(x_ref, mod_ref, a_ref, w_ref, o_ref):
    y = jnp.dot(a_ref[...], w_ref[...], preferred_element_type=F32)
    o_ref[...] = x_ref[...] + mod_ref[5:6, :] * y


def _gdn_out(x, mods_l, og, w_out, row_of_tile, tm):
    m, d = x.shape
    kdim = og.shape[1]
    return pl.pallas_call(
        _out_kernel,
        grid=(m // tm,),
        in_specs=[
            pl.BlockSpec((tm, d), lambda i: (i, 0)),
            pl.BlockSpec((None, N_MOD, d), lambda i: (row_of_tile(i), 0, 0)),
            pl.BlockSpec((tm, kdim), lambda i: (i, 0)),
            pl.BlockSpec((kdim, d), lambda i: (0, 0)),
        ],
        out_specs=pl.BlockSpec((tm, d), lambda i: (i, 0)),
        out_shape=jax.ShapeDtypeStruct((m, d), F32),
        compiler_params=_cparams("parallel"),
        name="gdn_out",
    )(x, mods_l, og, w_out)


def _pool_kernel(x_ref, mod_ref, g_ref, w_ref, sc_ref, *rest, t):
    o_ref = rest[-1]
    x = x_ref[...]
    rstd = lax.rsqrt(jnp.mean(x * x, axis=-1, keepdims=True) + RMS_EPS)
    pg = w_ref.shape[1]
    edge = max(POOL_WINDOWS)
    zeros = jnp.zeros((edge, pg), F32)
    row = lax.broadcasted_iota(jnp.int32, (t, pg), 0)
    for gi, win in enumerate(POOL_WINDOWS):
        cs = slice(gi * pg, (gi + 1) * pg)
        xg = x_ref[:, cs]
        h = (xg * rstd * g_ref[:, cs]) * (1.0 + mod_ref[4:5, cs]) + mod_ref[3:4, cs]
        half = win // 2
        n = t + 2 * edge
        acc = jnp.concatenate([zeros, h, zeros], axis=0)
        s = 1
        while s < win:
            acc = acc + pltpu.roll(acc, n - s, 0)
            s *= 2
        acc = acc[edge - half:edge - half + t]
        cnt = (jnp.minimum(row + (win - half), t) - jnp.maximum(row - half, 0)).astype(F32)
        diff = (acc / cnt - h).astype(BF16)
        y = jnp.dot(diff, w_ref[gi], preferred_element_type=F32) * sc_ref[:, cs]
        o_ref[:, cs] = xg + mod_ref[5:6, cs] * y


def _pool(x, mods_l, g, w, scale, out_prev, *, row0, n_seq, t, mod_row0, mod_per_seq):
    m, d = x.shape
    ng, pg, _ = w.shape
    assert ng == len(POOL_WINDOWS) and ng * pg == d
    rb = row0 // t
    mod_idx = (lambda b: mod_row0 + b) if mod_per_seq else (lambda b: mod_row0)
    in_specs = [
        pl.BlockSpec((t, d), lambda b: (rb + b, 0)),
        pl.BlockSpec((None, N_MOD, d), lambda b: (mod_idx(b), 0, 0)),
        pl.BlockSpec((1, d), lambda b: (0, 0)),
        pl.BlockSpec((ng, pg, pg), lambda b: (0, 0, 0)),
        pl.BlockSpec((1, d), lambda b: (0, 0)),
    ]
    args = [x, mods_l, g, w, scale]
    aliases = {}
    if out_prev is not None:
        aliases[len(args)] = 0
        in_specs.append(pl.BlockSpec(memory_space=pl.ANY))
        args.append(out_prev)
    return pl.pallas_call(
        functools.partial(_pool_kernel, t=t),
        grid=(n_seq,),
        in_specs=in_specs,
        out_specs=pl.BlockSpec((t, d), lambda b: (rb + b, 0)),
        out_shape=jax.ShapeDtypeStruct((m, d), F32),
        input_output_aliases=aliases,
        compiler_params=_cparams("parallel"),
        name="pool",
    )(*args)


def _final_kernel(x_ref, g_ref, o_ref):
    x = x_ref[...]
    o_ref[...] = x * lax.rsqrt(jnp.mean(x * x, axis=-1, keepdims=True) + RMS_EPS) * g_ref[...]


def _final_norm(x, g, tm, row0, rows):
    _, d = x.shape
    rb = row0 // tm
    return pl.pallas_call(
        _final_kernel,
        grid=(rows // tm,),
        in_specs=[pl.BlockSpec((tm, d), lambda i: (rb + i, 0)), pl.BlockSpec((1, d), lambda i: (0, 0))],
        out_specs=pl.BlockSpec((tm, d), lambda i: (i, 0)),
        out_shape=jax.ShapeDtypeStruct((rows, d), F32),
        compiler_params=_cparams("parallel"),
        name="final_norm",
    )(x, g)


def _place_cols(vals, n_heads):
    row = jnp.zeros((LANES,), F32)
    for d in range(2):
        row = lax.dynamic_update_slice(row, vals[d].astype(F32), (d * 2 * n_heads,))
    return row.reshape(1, LANES)


def kernel(x_prompt, x_sample, state_gdn, c, c_ctx, w_mod, b_mod, norm_g, ffn_w_in, ffn_w_out, gdn_w_in, gdn_conv, gdn_a_log, gdn_dt_bias, gdn_norm_g, gdn_w_out, pool_w, pool_scale, final_g):
    batch, seq, d = x_prompt.shape
    dec_batch, dec_seq, _ = x_sample.shape
    depth = w_mod.shape[0]
    n_heads = gdn_a_log.shape[-1]
    dk = LANES
    qk = n_heads * dk
    f = ffn_w_out.shape[2]
    fp = FF_TILE * ((f + FF_TILE - 1) // FF_TILE)
    m_p = batch * seq
    m_s = dec_batch * dec_seq
    assert 1 + dec_batch <= MOD_ROWS and gdn_w_in.shape[2] == 4 * qk + 4 * n_heads and 4 * n_heads <= LANES
    assert m_p % dec_seq == 0

    def token_tile(candidates):
        return next(c_ for c_ in candidates if m_p % c_ == 0 and dec_seq % c_ == 0)

    def mod_row_of_tile(tile):
        n_prompt_tiles = m_p // tile
        return lambda i: jnp.where(i < n_prompt_tiles, 0, 1 + ((i - n_prompt_tiles) * tile) // dec_seq)

    tm = token_tile((512, 256, 128, 64))
    row_of_tile = mod_row_of_tile(tm)
    tm_proj = token_tile((1024, 512, 256, 128, 64))

    c8 = jnp.zeros((MOD_ROWS, d), F32).at[0].set(c_ctx).at[1:1 + dec_batch].set(c)
    mods = _modulation(c8, w_mod, b_mod).reshape(depth, MOD_ROWS, N_MOD, d)

    x = _token_stream(x_prompt.reshape(m_p, d), x_sample.reshape(m_s, d),
                      _grid_pos_embed(dec_seq, d, x_sample.dtype), tm)

    w_gate_p, w_up_p, w_out_p = _prep_ffn_weights(ffn_w_in, ffn_w_out, fp)

    n_gdn = (depth + 1) // 2
    new_state = None
    for l in range(depth):
        mods_l = mods[l]
        ng = norm_g[l]

        def ffn(x, i, sub):
            return _ffn(x, mods_l, ng[sub:sub + 1], w_gate_p, w_up_p, w_out_p, l, i, mod_row_of_tile(tm_proj),
                        tm_proj, sub)

        x = ffn(x, 0, 0)
        mi = l // 2
        if l % 2 == 0:
            w_in = gdn_w_in[mi]
            w_main = w_in[:, :4 * qk].astype(BF16)
            w_ab = jnp.pad(w_in[:, 4 * qk:], ((0, 0), (0, LANES - 4 * n_heads))).astype(BF16)
            proj, ab = _gdn_proj(x, mods_l, ng[1:2], w_main, w_ab, mod_row_of_tile(tm_proj), tm_proj)
            alog_row = _place_cols(gdn_a_log[mi], n_heads)
            dtb_row = _place_cols(gdn_dt_bias[mi], n_heads)
            ng_row = gdn_norm_g[mi].reshape(1, dk).astype(F32)
            core = functools.partial(_gdn_core, proj, ab, gdn_conv[mi], alog_row, dtb_row, ng_row, n_heads=n_heads)
            og, new_state = core(None, None, new_state, row0=0, n_seq=batch, t=seq, hb=min(16, n_heads),
                                 state_slot=(mi, n_gdn))
            (og,) = core(state_gdn[:, mi], og, None, row0=m_p, n_seq=dec_batch, t=dec_seq, hb=min(4, n_heads),
                         state_slot=None)
            x = _gdn_out(x, mods_l, og, gdn_w_out[mi].astype(BF16), row_of_tile, tm)
        else:
            pw = pool_w[mi].astype(BF16)
            ps = pool_scale[mi].reshape(1, d)
            pool = functools.partial(_pool, x, mods_l, ng[1:2], pw, ps)
            x_new = pool(None, row0=0, n_seq=batch, t=seq, mod_row0=0, mod_per_seq=False)
            x = pool(x_new, row0=m_p, n_seq=dec_batch, t=dec_seq, mod_row0=1, mod_per_seq=True)
        x = ffn(x, 1, 2)

    fg = final_g.reshape(1, d)
    y_prompt = _final_norm(x, fg, tm, 0, m_p).reshape(batch, seq, d)
    y_sample = _final_norm(x, fg, tm, m_p, m_s).reshape(dec_batch, dec_seq, d)
    return (y_prompt, y_sample, new_state.astype(state_gdn.dtype))
```
